```python
import functools
import jax, jax.numpy as jnp
from jax import lax
import numpy as np

D_MODEL = 2048
BATCH = 2
SEQ = 4096
DEPTH = 1
DEC_BATCH = 8
DEC_SEQ = 4
PAST_LEN = 16384
PAGE_SIZE = 128

N_HEADS = 8
HEAD_DIM = 128
ATTN_WIDTH = N_HEADS * HEAD_DIM
CONV_WIDTH = D_MODEL - ATTN_WIDTH
MIX_WIDTH = ATTN_WIDTH + CONV_WIDTH
IN_COLS = 3 * ATTN_WIDTH + 2 * CONV_WIDTH
CONV_K = 31
CONV_BUF = CONV_K - 1
MOBA_BLOCK = 256
MOBA_TOPK = 3
Q_CHUNK = 32
D_FF = ((8 * D_MODEL + 3 * 256 - 1) // (3 * 256)) * 256
EPS = 1e-6
NEG = -1e30

kernel_name = 'hymba_moba_conformer_conv_adaln_step'


def rmsnorm(x, g):
    xf = x.astype(jnp.float32)
    y = xf * lax.rsqrt(jnp.mean(xf * xf, axis=-1, keepdims=True) + EPS)
    return (y * g.astype(jnp.float32)).astype(x.dtype)


def layernorm(x, g, b):
    xf = x.astype(jnp.float32)
    mu = jnp.mean(xf, axis=-1, keepdims=True)
    var = jnp.mean(jnp.square(xf - mu), axis=-1, keepdims=True)
    y = (xf - mu) * lax.rsqrt(var + EPS)
    return (y * g.astype(jnp.float32) + b.astype(jnp.float32)).astype(x.dtype)


def alibi_slopes():
    return 2.0 ** (-8.0 * (jnp.arange(N_HEADS, dtype=jnp.float32) + 1.0) / N_HEADS)


def select_blocks(q, kmean, t_pos):
    s = jnp.einsum('bhqd,bhnd->bhqn', q.astype(jnp.float32), kmean.astype(jnp.float32))
    nb = kmean.shape[2]
    n_full = t_pos // MOBA_BLOCK
    s = jnp.where(jnp.arange(nb)[None, :] < n_full[:, None], s, NEG)
    if nb < MOBA_TOPK:
        s = jnp.pad(s, ((0, 0), (0, 0), (0, 0), (0, MOBA_TOPK - nb)), constant_values=NEG)
    _, idx = lax.top_k(s, MOBA_TOPK)
    valid = jnp.arange(MOBA_TOPK)[None, :] < n_full[:, None]
    idx = jnp.where(valid, idx, 0)
    return idx, valid


def moba_attend(q, t_pos, k_sel, v_sel, pos_sel, valid_sel, k_own, v_own, pos_own, slopes):
    B, H, Q, J, N, _ = k_sel.shape
    qf = q.astype(jnp.float32) * (HEAD_DIM ** -0.5)
    m = slopes[None, :, None, None]
    dist_sel = (t_pos[:, None, None] - pos_sel).astype(jnp.float32)
    s_sel = jnp.einsum('bhqd,bhqjnd->bhqjn', qf, k_sel.astype(jnp.float32)) - m[..., None] * dist_sel
    s_sel = jnp.where(valid_sel[..., None], s_sel, NEG).reshape(B, H, Q, J * N)
    dist_own = t_pos[:, None] - pos_own[None, :]
    s_own = jnp.einsum('bhqd,bhnd->bhqn', qf, k_own.astype(jnp.float32)) - m * dist_own.astype(jnp.float32)
    s_own = jnp.where(dist_own >= 0, s_own, NEG)
    p = jax.nn.softmax(jnp.concatenate([s_sel, s_own], axis=-1), axis=-1)
    p_sel = p[..., : J * N].reshape(B, H, Q, J, N)
    out = (jnp.einsum('bhqjn,bhqjnd->bhqd', p_sel, v_sel.astype(jnp.float32))
           + jnp.einsum('bhqn,bhnd->bhqd', p[..., J * N:], v_own.astype(jnp.float32)))
    return out.astype(q.dtype)


def moba_prompt(q, k, v, slopes):
    B, H, S, _ = q.shape
    nb = -(-S // MOBA_BLOCK)
    pad = nb * MOBA_BLOCK - S
    kb = jnp.pad(k, ((0, 0), (0, 0), (0, pad), (0, 0))).reshape(B, H, nb, MOBA_BLOCK, HEAD_DIM)
    vb = jnp.pad(v, ((0, 0), (0, 0), (0, pad), (0, 0))).reshape(B, H, nb, MOBA_BLOCK, HEAD_DIM)
    kmean = kb.astype(jnp.float32).mean(axis=3)
    bi = jnp.arange(B)[:, None, None, None]
    hi = jnp.arange(H)[None, :, None, None]

    def chunk(ci):
        q0 = ci * Q_CHUNK
        qc = lax.dynamic_slice_in_dim(q, q0, Q_CHUNK, axis=2)
        t_pos = q0 + jnp.arange(Q_CHUNK)
        idx, valid = select_blocks(qc, kmean, t_pos)
        k_sel = kb[bi, hi, idx]
        v_sel = vb[bi, hi, idx]
        pos_sel = idx[..., None] * MOBA_BLOCK + jnp.arange(MOBA_BLOCK)
        own = q0 // MOBA_BLOCK
        k_own = lax.dynamic_index_in_dim(kb, own, axis=2, keepdims=False)
        v_own = lax.dynamic_index_in_dim(vb, own, axis=2, keepdims=False)
        pos_own = own * MOBA_BLOCK + jnp.arange(MOBA_BLOCK)
        return moba_attend(qc, t_pos, k_sel, v_sel, pos_sel, valid, k_own, v_own, pos_own, slopes)

    out = lax.map(chunk, jnp.arange(S // Q_CHUNK))
    return out.transpose(1, 2, 0, 3, 4).reshape(B, H, S, HEAD_DIM)


def moba_sample(q, k_new, v_new, pool_k, pool_v, page_table, slopes):
    DB, H, DS, _ = q.shape
    ppb = MOBA_BLOCK // PAGE_SIZE
    past = page_table.shape[1] * PAGE_SIZE
    n_full = past // MOBA_BLOCK
    own0 = n_full * MOBA_BLOCK
    n_tail = past - own0
    t_pos = past + jnp.arange(DS)
    kp = pool_k[page_table[:, : n_full * ppb]]
    kmean = kp.astype(jnp.float32).reshape(DB, n_full, ppb, H, PAGE_SIZE, HEAD_DIM).mean(axis=(2, 4))
    kmean = kmean.transpose(0, 2, 1, 3)
    idx, valid = select_blocks(q, kmean, t_pos)
    bi = jnp.arange(DB)[:, None, None, None, None]
    hi = jnp.arange(H)[None, :, None, None, None]
    phys = page_table[bi, idx[..., None] * ppb + jnp.arange(ppb)]
    k_sel = pool_k[phys, hi].reshape(DB, H, DS, MOBA_TOPK, MOBA_BLOCK, HEAD_DIM)
    v_sel = pool_v[phys, hi].reshape(DB, H, DS, MOBA_TOPK, MOBA_BLOCK, HEAD_DIM)
    pos_sel = idx[..., None] * MOBA_BLOCK + jnp.arange(MOBA_BLOCK)
    tail_pages = page_table[:, own0 // PAGE_SIZE: past // PAGE_SIZE]
    k_tail = pool_k[tail_pages].transpose(0, 2, 1, 3, 4).reshape(DB, H, n_tail, HEAD_DIM)
    v_tail = pool_v[tail_pages].transpose(0, 2, 1, 3, 4).reshape(DB, H, n_tail, HEAD_DIM)
    k_own = jnp.concatenate([k_tail.astype(k_new.dtype), k_new], axis=2)
    v_own = jnp.concatenate([v_tail.astype(v_new.dtype), v_new], axis=2)
    pos_own = own0 + jnp.arange(n_tail + DS)
    return moba_attend(q, t_pos, k_sel, v_sel, pos_sel, valid, k_own, v_own, pos_own, slopes)


def conv_module(a, gate, buf, w_dw, b_dw, g_ln, b_ln):
    u = a * jax.nn.sigmoid(gate)
    ext = jnp.concatenate([buf.astype(u.dtype), u], axis=1)
    y = lax.conv_general_dilated(ext, w_dw.astype(ext.dtype), window_strides=(1,), padding='VALID',
                                 dimension_numbers=('NWC', 'WIO', 'NWC'),
                                 feature_group_count=CONV_WIDTH) + b_dw
    y = jax.nn.silu(layernorm(y, g_ln, b_ln))
    return y, ext[:, -CONV_BUF:]


def layer(x, c, conv_buf, attend, w_ada, b_ada, g_pre_mix, w_in, w_dw, b_dw, g_conv_ln, b_conv_ln,
          g_attn_out, g_conv_out, w_out, g_post_mix, g_pre_ffn, w_gate_up, w_down, g_post_ffn):
    B, L, _ = x.shape
    mod = jax.nn.silu(c) @ w_ada + b_ada
    sh_m, sc_m, gt_m, sh_f, sc_f, gt_f = jnp.split(mod[:, None, :], 6, axis=-1)
    h = rmsnorm(x, g_pre_mix) * (1.0 + sc_m) + sh_m
    q, k, v, a, g = jnp.split(h @ w_in, [ATTN_WIDTH, 2 * ATTN_WIDTH, 3 * ATTN_WIDTH,
                                        3 * ATTN_WIDTH + CONV_WIDTH], axis=-1)
    to_heads = lambda t: t.reshape(B, L, N_HEADS, HEAD_DIM).transpose(0, 2, 1, 3)
    qh, kh, vh = to_heads(q), to_heads(k), to_heads(v)
    o_attn = attend(qh, kh, vh).transpose(0, 2, 1, 3).reshape(B, L, ATTN_WIDTH)
    o_conv, new_buf = conv_module(a, g, conv_buf, w_dw, b_dw, g_conv_ln, b_conv_ln)
    merged = jnp.concatenate([rmsnorm(o_attn, g_attn_out), rmsnorm(o_conv, g_conv_out)], axis=-1) @ w_out
    x = x + gt_m * rmsnorm(merged, g_post_mix)
    h = rmsnorm(x, g_pre_ffn) * (1.0 + sc_f) + sh_f
    gate, up = jnp.split(h @ w_gate_up, 2, axis=-1)
    x = x + gt_f * rmsnorm((jax.nn.silu(gate) * up) @ w_down, g_post_ffn)
    return x, kh, vh, new_buf


def setup_inputs(seed: int = 0) -> dict:
    key = jax.random.key(seed)
    ks = jax.random.split(key, 32)
    f32 = jnp.float32
    n_pages = PAST_LEN // PAGE_SIZE
    n_used = DEC_BATCH * n_pages
    n_pool = n_used + max(1, n_used // 4)
    nrm = lambda k, shape, s: s * jax.random.normal(k, shape, f32)
    gain = lambda k, n: 1.0 + 0.02 * jax.random.normal(k, (DEPTH, n), f32)
    page_table = jax.random.permutation(ks[0], n_pool)[:n_used].reshape(DEC_BATCH, n_pages).astype(jnp.int32)
    return {
        'x_prompt': nrm(ks[1], (BATCH, SEQ, D_MODEL), 1.0),
        'x_sample': nrm(ks[2], (DEC_BATCH, DEC_SEQ, D_MODEL), 1.0),
        'cache_k': nrm(ks[3], (DEPTH, n_pool, N_HEADS, PAGE_SIZE, HEAD_DIM), 1.0),
        'cache_v': nrm(ks[4], (DEPTH, n_pool, N_HEADS, PAGE_SIZE, HEAD_DIM), 1.0),
        'state_conv': nrm(ks[5], (DEPTH, DEC_BATCH, CONV_BUF, CONV_WIDTH), 0.5),
        'page_table': page_table,
        'c_prompt': nrm(ks[6], (BATCH, D_MODEL), 1.0),
        'c_sample': nrm(ks[7], (DEC_BATCH, D_MODEL), 1.0),
        'w_ada': nrm(ks[8], (DEPTH, D_MODEL, 6 * D_MODEL), 0.5 * D_MODEL ** -0.5),
        'b_ada': nrm(ks[9], (DEPTH, 6 * D_MODEL), 0.02),
        'g_pre_mix': gain(ks[10], D_MODEL),
        'w_in': nrm(ks[11], (DEPTH, D_MODEL, IN_COLS), D_MODEL ** -0.5),
        'w_dw': nrm(ks[12], (DEPTH, CONV_K, 1, CONV_WIDTH), CONV_K ** -0.5),
        'b_dw': nrm(ks[13], (DEPTH, CONV_WIDTH), 0.02),
        'g_conv_ln': gain(ks[14], CONV_WIDTH),
        'b_conv_ln': nrm(ks[15], (DEPTH, CONV_WIDTH), 0.02),
        'g_attn_out': gain(ks[16], ATTN_WIDTH),
        'g_conv_out': gain(ks[17], CONV_WIDTH),
        'w_out': nrm(ks[18], (DEPTH, MIX_WIDTH, D_MODEL), MIX_WIDTH ** -0.5),
        'g_post_mix': gain(ks[19], D_MODEL),
        'g_pre_ffn': gain(ks[20], D_MODEL),
        'w_gate_up': nrm(ks[21], (DEPTH, D_MODEL, 2 * D_FF), D_MODEL ** -0.5),
        'w_down': nrm(ks[22], (DEPTH, D_FF, D_MODEL), D_FF ** -0.5),
        'g_post_ffn': gain(ks[23], D_MODEL),
    }


def reference(x_prompt, x_sample, cache_k, cache_v, state_conv, page_table, c_prompt, c_sample,
              w_ada, b_ada, g_pre_mix, w_in, w_dw, b_dw, g_conv_ln, b_conv_ln, g_attn_out, g_conv_out,
              w_out, g_post_mix, g_pre_ffn, w_gate_up, w_down, g_post_ffn):
    slopes = alibi_slopes()
    B, S, _ = x_prompt.shape
    y_p, y_s = x_prompt, x_sample
    k_p, v_p, cv_p, k_s, v_s, cv_s = [], [], [], [], [], []
    zero_buf = jnp.zeros((B, CONV_BUF, CONV_WIDTH), x_prompt.dtype)
    to_pages = lambda t: t.reshape(B, N_HEADS, S // PAGE_SIZE, PAGE_SIZE, HEAD_DIM).transpose(0, 2, 1, 3, 4)
    for l in range(DEPTH):
        lw = (w_ada[l], b_ada[l], g_pre_mix[l], w_in[l], w_dw[l], b_dw[l], g_conv_ln[l], b_conv_ln[l],
              g_attn_out[l], g_conv_out[l], w_out[l], g_post_mix[l], g_pre_ffn[l], w_gate_up[l],
              w_down[l], g_post_ffn[l])
        y_p, kh, vh, buf = layer(y_p, c_prompt, zero_buf, functools.partial(moba_prompt, slopes=slopes), *lw)
        k_p.append(to_pages(kh))
        v_p.append(to_pages(vh))
        cv_p.append(buf)
        attend_s = functools.partial(moba_sample, pool_k=cache_k[l], pool_v=cache_v[l],
                                     page_table=page_table, slopes=slopes)
        y_s, kh_s, vh_s, buf_s = layer(y_s, c_sample, state_conv[l], attend_s, *lw)
        k_s.append(kh_s)
        v_s.append(vh_s)
        cv_s.append(buf_s)
    return (y_p, y_s, jnp.stack(k_p), jnp.stack(v_p), jnp.stack(cv_p),
            jnp.stack(k_s), jnp.stack(v_s), jnp.stack(cv_s))
```

```python
import functools

import jax
import jax.numpy as jnp
from jax import lax
from jax.experimental import pallas as pl
from jax.experimental.pallas import tpu as pltpu

F32 = jnp.float32
BF16 = jnp.bfloat16

N_HEADS = 8
HEAD_DIM = 128
ATTN_WIDTH = N_HEADS * HEAD_DIM
CONV_K = 31
CONV_BUF = CONV_K - 1
CONV_HALO = 32
PAGE = 128
MOBA_BLOCK = 256
MOBA_TOPK = 3
PAGES_PER_BLOCK = MOBA_BLOCK // PAGE
EPS = 1e-6
NEG = -1e30
MIB = 1024 * 1024

_NT = (((1,), (1,)), ((), ()))


def _params(sem, vmem_mib):
    return pltpu.CompilerParams(dimension_semantics=sem, vmem_limit_bytes=vmem_mib * MIB)


def _rms(x, g):
    return x * lax.rsqrt(jnp.mean(x * x, axis=-1, keepdims=True) + EPS) * g


def _silu(x):
    return x * jax.nn.sigmoid(x)


def _mod2d(ref):
    v = ref[...]
    return v.reshape(v.shape[-2], v.shape[-1])


def _ada_kernel(c_ref, w_ref, b_ref, o_ref):
    s = _silu(c_ref[...]).astype(BF16)
    o_ref[...] = jnp.dot(s, w_ref[...], preferred_element_type=F32) + b_ref[...]


def _ada(c_all, w_bf, b):
    rows, d = c_all.shape
    n = w_bf.shape[1]
    tn = 1024
    return pl.pallas_call(
        _ada_kernel,
        out_shape=jax.ShapeDtypeStruct((rows, n), F32),
        grid=(n // tn,),
        in_specs=[pl.BlockSpec((rows, d), lambda j: (0, 0)),
                  pl.BlockSpec((d, tn), lambda j: (0, j)),
                  pl.BlockSpec((1, tn), lambda j: (0, j))],
        out_specs=pl.BlockSpec((rows, tn), lambda j: (0, j)),
        compiler_params=_params(("arbitrary",), 32),
        name="ada_mod",
    )(c_all, w_bf, b)


def _modnorm_kernel(x_ref, g_ref, sc_ref, sh_ref, o_ref):
    h = _rms(x_ref[...], g_ref[...]) * (1.0 + _mod2d(sc_ref)) + _mod2d(sh_ref)
    o_ref[...] = h.astype(BF16)


def _mod_spec(mod, which, tm, rows_per_group):
    if rows_per_group is None:
        return mod[which], pl.BlockSpec((tm, mod[which].shape[1]), lambda i, *_: (i, 0))
    tiles = rows_per_group // tm
    return mod, pl.BlockSpec((1, 1, mod.shape[2]), lambda i, *_: ((i // tiles) * 6 + which, 0, 0))


def _modnorm(x, g, mod, which_sc, which_sh, tm, rows_per_group):
    m, d = x.shape
    sc, sc_spec = _mod_spec(mod, which_sc, tm, rows_per_group)
    sh, sh_spec = _mod_spec(mod, which_sh, tm, rows_per_group)
    return pl.pallas_call(
        _modnorm_kernel,
        out_shape=jax.ShapeDtypeStruct((m, d), BF16),
        grid=(m // tm,),
        in_specs=[pl.BlockSpec((tm, d), lambda i: (i, 0)),
                  pl.BlockSpec((1, d), lambda i: (0, 0)), sc_spec, sh_spec],
        out_specs=pl.BlockSpec((tm, d), lambda i: (i, 0)),
        compiler_params=_params(("parallel",), 32),
        name="modnorm",
    )(x, g, sc, sh)


def _kv_proj_kernel(h_ref, w_ref, pages_ref, heads_ref, *rest, tm, with_mean):
    acc = jnp.dot(h_ref[...], w_ref[...], preferred_element_type=F32)
    for hh in range(N_HEADS):
        cs = slice(hh * HEAD_DIM, (hh + 1) * HEAD_DIM)
        heads_ref[0, hh] = acc[:, cs].astype(BF16)
        for p in range(tm // PAGE):
            pages_ref[0, p, hh] = acc[p * PAGE:(p + 1) * PAGE, cs]
    if with_mean:
        mean_ref = rest[0]
        for mb in range(tm // MOBA_BLOCK):
            mean_ref[mb] = jnp.mean(acc[mb * MOBA_BLOCK:(mb + 1) * MOBA_BLOCK, :], axis=0, keepdims=True)


def _kv_proj(h, w_bf, col_block, batch, seq, with_mean):
    tm = 512
    m, d = h.shape
    tiles = seq // tm
    out_shape = [jax.ShapeDtypeStruct((batch, seq // PAGE, N_HEADS, PAGE, HEAD_DIM), F32),
                 jax.ShapeDtypeStruct((batch, N_HEADS, seq, HEAD_DIM), BF16)]
    out_specs = [pl.BlockSpec((1, tm // PAGE, N_HEADS, PAGE, HEAD_DIM), lambda i: (i // tiles, i % tiles, 0, 0, 0)),
                 pl.BlockSpec((1, N_HEADS, tm, HEAD_DIM), lambda i: (i // tiles, 0, i % tiles, 0))]
    if with_mean:
        out_shape.append(jax.ShapeDtypeStruct((m // MOBA_BLOCK, 1, ATTN_WIDTH), F32))
        out_specs.append(pl.BlockSpec((tm // MOBA_BLOCK, 1, ATTN_WIDTH), lambda i: (i, 0, 0)))
    return pl.pallas_call(
        functools.partial(_kv_proj_kernel, tm=tm, with_mean=with_mean),
        out_shape=out_shape,
        grid=(m // tm,),
        in_specs=[pl.BlockSpec((tm, d), lambda i: (i, 0)),
                  pl.BlockSpec((d, ATTN_WIDTH), lambda i: (0, col_block))],
        out_specs=out_specs,
        compiler_params=_params(("parallel",), 48),
        name="kv_proj",
    )(h, w_bf)


def _q_proj_kernel(h_ref, w_ref, km_ref, q_ref, pen_ref, *, tm, tiles):
    i = pl.program_id(0)
    acc = jnp.dot(h_ref[...], w_ref[...], preferred_element_type=F32)
    km = km_ref[0]
    nb = km.shape[0]
    jrow = lax.broadcasted_iota(jnp.int32, (nb, tm), 0)
    tcol = lax.broadcasted_iota(jnp.int32, (nb, tm), 1)
    n_full = (i % tiles) * (tm // MOBA_BLOCK) + lax.shift_right_logical(tcol, MOBA_BLOCK.bit_length() - 1)
    valid = jrow < n_full
    pens = []
    for hh in range(N_HEADS):
        cs = slice(hh * HEAD_DIM, (hh + 1) * HEAD_DIM)
        qh = acc[:, cs]
        q_ref[0, hh] = (qh * (HEAD_DIM ** -0.5)).astype(BF16)
        s = lax.dot_general(km[:, cs], qh, _NT, precision=lax.Precision.HIGHEST, preferred_element_type=F32)
        rank = jnp.zeros((nb, tm), jnp.int32)
        for jp in range(nb):
            row = s[jp:jp + 1, :]
            beats = valid[jp:jp + 1, :] & ((row > s) | ((row == s) & (jrow > jp)))
            rank = rank + jnp.where(beats, 1, 0)
        sel = valid & (rank < MOBA_TOPK)
        pens.append(jnp.where(sel, 0.0, NEG))
    pen = jnp.concatenate(pens, axis=0)
    pen_ref[0] = pen.T.astype(BF16)


def _q_proj(h, w_bf, kmean, batch, seq):
    tm = 512
    m, d = h.shape
    tiles = seq // tm
    nb = kmean.shape[1]
    return pl.pallas_call(
        functools.partial(_q_proj_kernel, tm=tm, tiles=tiles),
        out_shape=[jax.ShapeDtypeStruct((batch, N_HEADS, seq, HEAD_DIM), BF16),
                   jax.ShapeDtypeStruct((batch, seq, N_HEADS * nb), BF16)],
        grid=(m // tm,),
        in_specs=[pl.BlockSpec((tm, d), lambda i: (i, 0)),
                  pl.BlockSpec((d, ATTN_WIDTH), lambda i: (0, 0)),
                  pl.BlockSpec((1, nb, ATTN_WIDTH), lambda i: (i // tiles, 0, 0))],
        out_specs=[pl.BlockSpec((1, N_HEADS, tm, HEAD_DIM), lambda i: (i // tiles, 0, i % tiles, 0)),
                   pl.BlockSpec((1, tm, N_HEADS * nb), lambda i: (i // tiles, i % tiles, 0))],
        compiler_params=_params(("parallel",), 48),
        name="q_proj",
    )(h, w_bf, kmean)


def _moba_prompt_kernel(slopes_ref, q_ref, pen_ref, k_ref, v_ref, o_ref, *, nb):
    h = pl.program_id(1)
    i = pl.program_id(2)
    slope = slopes_ref[h]
    q = q_ref[0, 0]
    qe = jnp.concatenate([q, pen_ref[0]], axis=1)
    row = lax.broadcasted_iota(jnp.int32, (MOBA_BLOCK, MOBA_BLOCK), 0)
    col = lax.broadcasted_iota(jnp.int32, (MOBA_BLOCK, MOBA_BLOCK), 1)
    rc = (row - col).astype(F32)
    bias = slope * rc
    lane = lax.broadcasted_iota(jnp.int32, (MOBA_BLOCK, HEAD_DIM), 1)

    def kv(j):
        sl = pl.ds(pl.multiple_of(j * MOBA_BLOCK, MOBA_BLOCK), MOBA_BLOCK)
        return k_ref[0, 0, sl, :], v_ref[0, 0, sl, :]

    kd, vd = kv(i)
    s = lax.dot_general(q, kd, _NT, preferred_element_type=F32) - bias
    s = jnp.where(rc >= 0.0, s, NEG)
    m = jnp.max(s, axis=1, keepdims=True)
    p = jnp.exp(s - m)
    l = jnp.sum(p, axis=1, keepdims=True)
    acc = jnp.dot(p.astype(BF16), vd, preferred_element_type=F32)

    def body(j, carry):
        m, l, acc = carry
        kj, vj = kv(j)
        onehot = jnp.where(lane == h * nb + j, 1.0, 0.0).astype(BF16)
        ke = jnp.concatenate([kj, onehot], axis=1)
        off = slope * ((i - j) * MOBA_BLOCK).astype(F32)
        s = lax.dot_general(qe, ke, _NT, preferred_element_type=F32) - bias - off
        m_new = jnp.maximum(m, jnp.max(s, axis=1, keepdims=True))
        alpha = jnp.exp(m - m_new)
        p = jnp.exp(s - m_new)
        l = alpha * l + jnp.sum(p, axis=1, keepdims=True)
        acc = alpha * acc + jnp.dot(p.astype(BF16), vj, preferred_element_type=F32)
        return m_new, l, acc

    m, l, acc = lax.fori_loop(0, i, body, (m, l, acc))
    o_ref[0] = acc / l


def _moba_prompt(slopes, q, pen, k, v):
    batch, _, seq, _ = q.shape
    nb = seq // MOBA_BLOCK
    grid_spec = pltpu.PrefetchScalarGridSpec(
        num_scalar_prefetch=1,
        grid=(batch, N_HEADS, nb),
        in_specs=[pl.BlockSpec((1, 1, MOBA_BLOCK, HEAD_DIM), lambda b, h, i, s: (b, h, i, 0)),
                  pl.BlockSpec((1, MOBA_BLOCK, N_HEADS * nb), lambda b, h, i, s: (b, i, 0)),
                  pl.BlockSpec((1, 1, seq, HEAD_DIM), lambda b, h, i, s: (b, h, 0, 0)),
                  pl.BlockSpec((1, 1, seq, HEAD_DIM), lambda b, h, i, s: (b, h, 0, 0))],
        out_specs=pl.BlockSpec((1, MOBA_BLOCK, HEAD_DIM), lambda b, h, i, s: (b, i, h)),
    )
    return pl.pallas_call(
        functools.partial(_moba_prompt_kernel, nb=nb),
        out_shape=jax.ShapeDtypeStruct((batch, seq, ATTN_WIDTH), F32),
        grid_spec=grid_spec,
        compiler_params=_params(("parallel", "parallel", "arbitrary"), 32),
        name="moba_prompt",
    )(slopes, q, pen, k, v)


def _glu_proj_kernel(h_ref, wa_ref, wg_ref, u_ref):
    a = jnp.dot(h_ref[...], wa_ref[...], preferred_element_type=F32)
    g = jnp.dot(h_ref[...], wg_ref[...], preferred_element_type=F32)
    u_ref[...] = a * jax.nn.sigmoid(g)


def _glu_proj(h, w_bf, a_col0, g_col0, width):
    tm, tn = 512, 512
    m, d = h.shape
    return pl.pallas_call(
        _glu_proj_kernel,
        out_shape=jax.ShapeDtypeStruct((m, width), F32),
        grid=(width // tn, m // tm),
        in_specs=[pl.BlockSpec((tm, d), lambda n, i: (i, 0)),
                  pl.BlockSpec((d, tn), lambda n, i: (0, a_col0 // tn + n)),
                  pl.BlockSpec((d, tn), lambda n, i: (0, g_col0 // tn + n))],
        out_specs=pl.BlockSpec((tm, tn), lambda n, i: (i, n)),
        compiler_params=_params(("parallel", "parallel"), 32),
        name="glu_proj",
    )(h, w_bf, w_bf)


def _conv_tail(y, gln, bln, gco):
    mu = jnp.mean(y, axis=-1, keepdims=True)
    var = jnp.mean(jnp.square(y - mu), axis=-1, keepdims=True)
    z = _silu((y - mu) * lax.rsqrt(var + EPS) * gln + bln)
    return _rms(z, gco)


def _conv_prompt_kernel(u_ref, halo_ref, w_ref, bdw_ref, gln_ref, bln_ref, gco_ref, cn_ref, buf_ref,
                        ext_ref, y_ref, *, tt):
    t = pl.program_id(1)
    ext_ref[0:CONV_HALO, :] = jnp.where(t == 0, 0.0, halo_ref[...])
    ext_ref[CONV_HALO:CONV_HALO + tt, :] = u_ref[...]
    lead = CONV_HALO - CONV_BUF
    width = u_ref.shape[1]
    for cb in range(width // 128):
        cs = slice(cb * 128, (cb + 1) * 128)
        acc = jnp.zeros((tt, 128), F32)
        for k in range(CONV_K):
            acc = acc + w_ref[k:k + 1, cs] * ext_ref[lead + k:lead + k + tt, cs]
        y_ref[:, cs] = acc + bdw_ref[:, cs]
    cn_ref[...] = _conv_tail(y_ref[...], gln_ref[...], bln_ref[...], gco_ref[...]).astype(BF16)

    @pl.when(t == pl.num_programs(1) - 1)
    def _():
        buf_ref[0] = ext_ref[CONV_HALO + tt - CONV_BUF:CONV_HALO + tt, :]


def _conv_prompt(u, w_dw, b_dw, g_ln, b_ln, g_co, batch, seq):
    tt = 256
    m, width = u.shape
    tiles = seq // tt
    halo_per_tile = tt // CONV_HALO
    vec = pl.BlockSpec((1, width), lambda b, t: (0, 0))
    return pl.pallas_call(
        functools.partial(_conv_prompt_kernel, tt=tt),
        out_shape=[jax.ShapeDtypeStruct((m, width), BF16),
                   jax.ShapeDtypeStruct((batch, CONV_BUF, width), F32)],
        grid=(batch, tiles),
        in_specs=[pl.BlockSpec((tt, width), lambda b, t: (b * tiles + t, 0)),
                  pl.BlockSpec((CONV_HALO, width),
                               lambda b, t: (jnp.maximum((b * tiles + t) * halo_per_tile - 1, 0), 0)),
                  pl.BlockSpec((CONV_K, width), lambda b, t: (0, 0)), vec, vec, vec, vec],
        out_specs=[pl.BlockSpec((tt, width), lambda b, t: (b * tiles + t, 0)),
                   pl.BlockSpec((1, CONV_BUF, width), lambda b, t: (b, 0, 0))],
        scratch_shapes=[pltpu.VMEM((CONV_HALO + tt, width), F32), pltpu.VMEM((tt, width), F32)],
        compiler_params=_params(("parallel", "arbitrary"), 32),
        name="conv_prompt",
    )(u, u, w_dw, b_dw, g_ln, b_ln, g_co)


def _conv_sample_kernel(st_ref, a_ref, g_ref, w_ref, bdw_ref, gln_ref, bln_ref, gco_ref, cn_ref, buf_ref,
                        ext_ref, *, rows):
    ext_ref[0:CONV_BUF, :] = st_ref[0]
    ext_ref[CONV_BUF:CONV_BUF + rows, :] = a_ref[0] * jax.nn.sigmoid(g_ref[0])
    acc = jnp.zeros((rows, a_ref.shape[2]), F32)
    for k in range(CONV_K):
        acc = acc + w_ref[k:k + 1, :] * ext_ref[k:k + rows, :]
    y = acc + bdw_ref[...]
    cn_ref[0] = _conv_tail(y, gln_ref[...], bln_ref[...], gco_ref[...]).astype(BF16)
    buf_ref[0] = ext_ref[rows:rows + CONV_BUF, :]


def _conv_sample(state, a, g, w_dw, b_dw, g_ln, b_ln, g_co):
    batch, rows, width = a.shape
    vec = pl.BlockSpec((1, width), lambda b: (0, 0))
    tok = pl.BlockSpec((1, rows, width), lambda b: (b, 0, 0))
    buf = pl.BlockSpec((1, CONV_BUF, width), lambda b: (b, 0, 0))
    return pl.pallas_call(
        functools.partial(_conv_sample_kernel, rows=rows),
        out_shape=[jax.ShapeDtypeStruct((batch, rows, width), BF16),
                   jax.ShapeDtypeStruct((batch, CONV_BUF, width), F32)],
        grid=(batch,),
        in_specs=[buf, tok, tok, pl.BlockSpec((CONV_K, width), lambda b: (0, 0)), vec, vec, vec, vec],
        out_specs=[tok, buf],
        scratch_shapes=[pltpu.VMEM((CONV_BUF + rows + 6, width), F32)],
        compiler_params=_params(("parallel",), 32),
        name="conv_sample",
    )(state, a, g, w_dw, b_dw, g_ln, b_ln, g_co)


def _out_proj_kernel(o_ref, cn_ref, w_ref, x_ref, gao_ref, gpm_ref, gpf_ref, gt_ref, sc_ref, sh_ref,
                     x1_ref, h2_ref):
    an = _rms(o_ref[...], gao_ref[...]).astype(BF16)
    mix = jnp.concatenate([an, cn_ref[...]], axis=1)
    merged = jnp.dot(mix, w_ref[...], preferred_element_type=F32)
    x1 = x_ref[...] + _mod2d(gt_ref) * _rms(merged, gpm_ref[...])
    x1_ref[...] = x1
    h2_ref[...] = (_rms(x1, gpf_ref[...]) * (1.0 + _mod2d(sc_ref)) + _mod2d(sh_ref)).astype(BF16)


def _out_proj(o_attn, conv_n, w_bf, x, g_ao, g_pm, g_pf, mod, tm, rows_per_group):
    m, d = x.shape
    aw = o_attn.shape[1]
    cw = conv_n.shape[1]
    gt, gt_spec = _mod_spec(mod, 2, tm, rows_per_group)
    sc, sc_spec = _mod_spec(mod, 4, tm, rows_per_group)
    sh, sh_spec = _mod_spec(mod, 3, tm, rows_per_group)
    row = lambda w: pl.BlockSpec((tm, w), lambda i: (i, 0))
    vec = lambda w: pl.BlockSpec((1, w), lambda i: (0, 0))
    return pl.pallas_call(
        _out_proj_kernel,
        out_shape=[jax.ShapeDtypeStruct((m, d), F32), jax.ShapeDtypeStruct((m, d), BF16)],
        grid=(m // tm,),
        in_specs=[row(aw), row(cw), pl.BlockSpec((aw + cw, d), lambda i: (0, 0)), row(d),
                  vec(aw), vec(d), vec(d), gt_spec, sc_spec, sh_spec],
        out_specs=[row(d), row(d)],
        compiler_params=_params(("parallel",), 48),
        name="out_proj",
    )(o_attn, conv_n, w_bf, x, g_ao, g_pm, g_pf, gt, sc, sh)


def _ffn_up_kernel(h_ref, wg_ref, wu_ref, a_ref):
    g = jnp.dot(h_ref[...], wg_ref[...], preferred_element_type=F32)
    u = jnp.dot(h_ref[...], wu_ref[...], preferred_element_type=F32)
    a_ref[...] = (_silu(g) * u).astype(BF16)


def _ffn_up(h, w_bf, tm):
    m, d = h.shape
    d_ff = w_bf.shape[1] // 2
    tn = 512
    return pl.pallas_call(
        _ffn_up_kernel,
        out_shape=jax.ShapeDtypeStruct((m, d_ff), BF16),
        grid=(d_ff // tn, m // tm),
        in_specs=[pl.BlockSpec((tm, d), lambda n, i: (i, 0)),
                  pl.BlockSpec((d, tn), lambda n, i: (0, n)),
                  pl.BlockSpec((d, tn), lambda n, i: (0, d_ff // tn + n))],
        out_specs=pl.BlockSpec((tm, tn), lambda n, i: (i, n)),
        compiler_params=_params(("parallel", "parallel"), 32),
        name="ffn_up",
    )(h, w_bf, w_bf)


def _ffn_down_kernel(a_ref, w_ref, x1_ref, g_ref, gt_ref, y_ref, acc_ref):
    k = pl.program_id(1)

    @pl.when(k == 0)
    def _():
        acc_ref[...] = jnp.zeros_like(acc_ref)

    acc_ref[...] += jnp.dot(a_ref[...], w_ref[...], preferred_element_type=F32)

    @pl.when(k == pl.num_programs(1) - 1)
    def _():
        y_ref[...] = x1_ref[...] + _mod2d(gt_ref) * _rms(acc_ref[...], g_ref[...])


def _ffn_down(act, w_bf, x1, g_post, mod, tm, rows_per_group):
    m, d_ff = act.shape
    d = w_bf.shape[1]
    tk = 512
    gt, gt_spec = _mod_spec(mod, 5, tm, rows_per_group)
    return pl.pallas_call(
        _ffn_down_kernel,
        out_shape=jax.ShapeDtypeStruct((m, d), F32),
        grid=(m // tm, d_ff // tk),
        in_specs=[pl.BlockSpec((tm, tk), lambda i, k: (i, k)),
                  pl.BlockSpec((tk, d), lambda i, k: (k, 0)),
                  pl.BlockSpec((tm, d), lambda i, k: (i, 0)),
                  pl.BlockSpec((1, d), lambda i, k: (0, 0)), gt_spec],
        out_specs=pl.BlockSpec((tm, d), lambda i, k: (i, 0)),
        scratch_shapes=[pltpu.VMEM((tm, d), F32)],
        compiler_params=_params(("parallel", "arbitrary"), 48),
        name="ffn_down",
    )(act, w_bf, x1, g_post, gt)


def _matmul_kernel(h_ref, w_ref, o_ref):
    o_ref[...] = jnp.dot(h_ref[...], w_ref[...], preferred_element_type=F32)


def _matmul(h, w_bf):
    m, d = h.shape
    n = w_bf.shape[1]
    tn = 1024
    return pl.pallas_call(
        _matmul_kernel,
        out_shape=jax.ShapeDtypeStruct((m, n), F32),
        grid=(n // tn,),
        in_specs=[pl.BlockSpec((m, d), lambda j: (0, 0)), pl.BlockSpec((d, tn), lambda j: (0, j))],
        out_specs=pl.BlockSpec((m, tn), lambda j: (0, j)),
        compiler_params=_params(("parallel",), 32),
        name="matmul",
    )(h, w_bf)


def _page_mean_kernel(pt_ref, *refs, pages_per_step):
    page_refs, o_ref = refs[:pages_per_step], refs[pages_per_step]
    for blk in range(pages_per_step // PAGES_PER_BLOCK):
        tot = jnp.zeros((N_HEADS, HEAD_DIM), F32)
        for pp in range(PAGES_PER_BLOCK):
            tot = tot + jnp.sum(page_refs[blk * PAGES_PER_BLOCK + pp][0], axis=1)
        o_ref[0, blk] = tot * (1.0 / MOBA_BLOCK)


def _page_mean(page_table_flat, pool_k, batch, n_pages):
    pages_per_step = 16
    steps = n_pages // pages_per_step
    blocks_per_step = pages_per_step // PAGES_PER_BLOCK

    def page_spec(p):
        return pl.BlockSpec((1, N_HEADS, PAGE, HEAD_DIM),
                            lambda b, s, pt: (pt[b * n_pages + s * pages_per_step + p], 0, 0, 0))

    grid_spec = pltpu.PrefetchScalarGridSpec(
        num_scalar_prefetch=1,
        grid=(batch, steps),
        in_specs=[page_spec(p) for p in range(pages_per_step)],
        out_specs=pl.BlockSpec((1, blocks_per_step, N_HEADS, HEAD_DIM), lambda b, s, pt: (b, s, 0, 0)),
    )
    return pl.pallas_call(
        functools.partial(_page_mean_kernel, pages_per_step=pages_per_step),
        out_shape=jax.ShapeDtypeStruct((batch, n_pages // PAGES_PER_BLOCK, N_HEADS, HEAD_DIM), F32),
        grid_spec=grid_spec,
        compiler_params=_params(("parallel", "parallel"), 48),
        name="page_mean",
    )(page_table_flat, *([pool_k] * pages_per_step))


def _select_sample_kernel(q_ref, km_ref, idx_ref, *, n_q):
    nb = km_ref.shape[2]
    lane = lax.broadcasted_iota(jnp.int32, (nb, 128), 1)
    rowi = lax.broadcasted_iota(jnp.int32, (nb, 128), 0)
    s = jnp.full((nb, 128), NEG, F32)
    for hh in range(N_HEADS):
        km = km_ref[0, hh]
        for t in range(n_q):
            col = jnp.sum(km * q_ref[0, hh, t:t + 1, :], axis=1, keepdims=True)
            s = jnp.where(lane == hh * n_q + t, col, s)
    out_row = lax.broadcasted_iota(jnp.int32, (8, 128), 0)
    out = jnp.zeros((8, 128), jnp.int32)
    for r in range(MOBA_TOPK):
        top = jnp.max(s, axis=0, keepdims=True)
        arg = jnp.min(jnp.where(s == top, rowi, nb), axis=0, keepdims=True)
        out = jnp.where(out_row == r, arg, out)
        s = jnp.where(rowi == arg, -jnp.inf, s)
    idx_ref[0] = out


def _select_sample(q, kmean):
    batch, _, n_q, _ = q.shape
    nb = kmean.shape[2]
    return pl.pallas_call(
        functools.partial(_select_sample_kernel, n_q=n_q),
        out_shape=jax.ShapeDtypeStruct((batch, 8, 128), jnp.int32),
        grid=(batch,),
        in_specs=[pl.BlockSpec((1, N_HEADS, n_q, HEAD_DIM), lambda b: (b, 0, 0, 0)),
                  pl.BlockSpec((1, N_HEADS, nb, HEAD_DIM), lambda b: (b, 0, 0, 0))],
        out_specs=pl.BlockSpec((1, 8, 128), lambda b: (b, 0, 0)),
        compiler_params=_params(("parallel",), 32),
        name="select_sample",
    )(q, kmean)


def _moba_sample_kernel(pt_ref, idx_ref, slopes_ref, q_ref, kn_ref, vn_ref, *refs, n_q, past):
    n_sel = n_q * MOBA_TOPK * PAGES_PER_BLOCK
    k_refs, v_refs, o_ref = refs[:n_sel], refs[n_sel:2 * n_sel], refs[2 * n_sel]
    b = pl.program_id(0)
    h = pl.program_id(1)
    slope = slopes_ref[h]
    key = lax.broadcasted_iota(jnp.int32, (PAGE, 1), 0)
    for t in range(n_q):
        q = q_ref[0, 0, t:t + 1, :] * (HEAD_DIM ** -0.5)
        scores = []
        for s in range(MOBA_TOPK):
            blk = idx_ref[((b * N_HEADS + h) * n_q + t) * MOBA_TOPK + s]
            for pp in range(PAGES_PER_BLOCK):
                kp = k_refs[(t * MOBA_TOPK + s) * PAGES_PER_BLOCK + pp][0, 0]
                dist = (past + t - blk * MOBA_BLOCK - pp * PAGE - key).astype(F32)
                scores.append(jnp.sum(kp * q, axis=1, keepdims=True) - slope * dist)
        own = []
        for t2 in range(t + 1):
            own.append(jnp.sum(kn_ref[0, 0, t2:t2 + 1, :] * q, axis=1, keepdims=True) - slope * float(t - t2))
        m = own[0]
        for sc in scores:
            m = jnp.maximum(m, jnp.max(sc, axis=0, keepdims=True))
        for sc in own[1:]:
            m = jnp.maximum(m, sc)
        l = jnp.zeros((1, 1), F32)
        acc = jnp.zeros((1, HEAD_DIM), F32)
        for n, sc in enumerate(scores):
            p = jnp.exp(sc - m)
            l = l + jnp.sum(p, axis=0, keepdims=True)
            acc = acc + jnp.sum(p * v_refs[n + t * MOBA_TOPK * PAGES_PER_BLOCK][0, 0], axis=0, keepdims=True)
        for t2, sc in enumerate(own):
            p = jnp.exp(sc - m)
            l = l + p
            acc = acc + p * vn_ref[0, 0, t2:t2 + 1, :]
        o_ref[0, 0, t:t + 1, :] = acc / l


def _moba_sample(page_table_flat, idx_flat, slopes, q, k_new, v_new, pool_k, pool_v, n_pages):
    batch, _, n_q, _ = q.shape
    past = n_pages * PAGE

    def sel_spec(t, s, pp):
        def index(b, h, pt, idx, sl):
            blk = idx[((b * N_HEADS + h) * n_q + t) * MOBA_TOPK + s]
            return (pt[b * n_pages + blk * PAGES_PER_BLOCK + pp], h, 0, 0)
        return pl.BlockSpec((1, 1, PAGE, HEAD_DIM), index)

    sel_specs = [sel_spec(t, s, pp) for t in range(n_q) for s in range(MOBA_TOPK) for pp in range(PAGES_PER_BLOCK)]
    tok = pl.BlockSpec((1, 1, n_q, HEAD_DIM), lambda b, h, pt, idx, sl: (b, h, 0, 0))
    grid_spec = pltpu.PrefetchScalarGridSpec(
        num_scalar_prefetch=3,
        grid=(batch, N_HEADS),
        in_specs=[tok, tok, tok] + sel_specs + sel_specs,
        out_specs=tok,
    )
    n_sel = len(sel_specs)
    return pl.pallas_call(
        functools.partial(_moba_sample_kernel, n_q=n_q, past=past),
        out_shape=jax.ShapeDtypeStruct((batch, N_HEADS, n_q, HEAD_DIM), F32),
        grid_spec=grid_spec,
        compiler_params=_params(("parallel", "parallel"), 32),
        name="moba_sample",
    )(page_table_flat, idx_flat, slopes, q, k_new, v_new, *([pool_k] * n_sel), *([pool_v] * n_sel))


def kernel(x_prompt, x_sample, cache_k, cache_v, state_conv, page_table, c_prompt, c_sample, w_ada, b_ada, g_pre_mix, w_in, w_dw, b_dw, g_conv_ln, b_conv_ln, g_attn_out, g_conv_out, w_out, g_post_mix, g_pre_ffn, w_gate_up, w_down, g_post_ffn):
    depth = w_ada.shape[0]
    assert depth == 1, "single layer: the prompt and sample residual streams are not chained across layers here"
    batch, seq, d = x_prompt.shape
    dec_batch, dec_seq, _ = x_sample.shape
    n_pages = page_table.shape[1]
    past = n_pages * PAGE
    conv_w = w_dw.shape[-1]
    assert past % MOBA_BLOCK == 0 and dec_seq <= MOBA_BLOCK and past // MOBA_BLOCK >= MOBA_TOPK
    assert seq % 512 == 0 and d == ATTN_WIDTH + conv_w

    slopes = 2.0 ** (-8.0 * (jnp.arange(N_HEADS, dtype=F32) + 1.0) / N_HEADS)
    l = 0
    w_ada_bf, w_in_bf = w_ada[l].astype(BF16), w_in[l].astype(BF16)
    w_out_bf, w_gu_bf, w_down_bf = w_out[l].astype(BF16), w_gate_up[l].astype(BF16), w_down[l].astype(BF16)
    vec = lambda a: a[l].reshape(1, -1)
    g_pm, g_ao, g_co = vec(g_pre_mix), vec(g_attn_out), vec(g_conv_out)
    g_post, g_pf, g_pffn = vec(g_post_mix), vec(g_pre_ffn), vec(g_post_ffn)
    bdw, gln, bln = vec(b_dw), vec(g_conv_ln), vec(b_conv_ln)
    wdw = w_dw[l].reshape(CONV_K, conv_w)

    n_c = batch + dec_batch
    c_rows = -(-n_c // 8) * 8
    c_all = jnp.concatenate([c_prompt, c_sample, jnp.zeros((c_rows - n_c, d), F32)], axis=0)
    mod = _ada(c_all, w_ada_bf, vec(b_ada))
    mod_p = mod.reshape(c_rows * 6, 1, d)
    mod_rows = jnp.repeat(mod[batch:n_c].reshape(dec_batch, 6, d), dec_seq, axis=0)
    mod_s = [mod_rows[:, w] for w in range(6)]

    xp = x_prompt.reshape(batch * seq, d)
    h = _modnorm(xp, g_pm, mod_p, 1, 0, 512, seq)
    k_pages, k_heads, kmean = _kv_proj(h, w_in_bf, 1, batch, seq, True)
    v_pages, v_heads = _kv_proj(h, w_in_bf, 2, batch, seq, False)
    q_heads, pen = _q_proj(h, w_in_bf, kmean.reshape(batch, seq // MOBA_BLOCK, ATTN_WIDTH), batch, seq)
    o_attn = _moba_prompt(slopes, q_heads, pen, k_heads, v_heads).reshape(batch * seq, ATTN_WIDTH)
    u = _glu_proj(h, w_in_bf, 3 * ATTN_WIDTH, 3 * ATTN_WIDTH + conv_w, conv_w)
    conv_n, conv_buf_p = _conv_prompt(u, wdw, bdw, gln, bln, g_co, batch, seq)
    x1, h2 = _out_proj(o_attn, conv_n, w_out_bf, xp, g_ao, g_post, g_pf, mod_p, 256, seq)
    act = _ffn_up(h2, w_gu_bf, 512)
    y_p = _ffn_down(act, w_down_bf, x1, g_pffn, mod_p, 512, seq).reshape(batch, seq, d)

    m_s = dec_batch * dec_seq
    xs = x_sample.reshape(m_s, d)
    hs = _modnorm(xs, g_pm, mod_s, 1, 0, m_s, None)
    proj = _matmul(hs, w_in_bf)
    to_heads = lambda t: t.reshape(dec_batch, dec_seq, N_HEADS, HEAD_DIM).transpose(0, 2, 1, 3)
    q_s = to_heads(proj[:, :ATTN_WIDTH])
    k_s = to_heads(proj[:, ATTN_WIDTH:2 * ATTN_WIDTH])
    v_s = to_heads(proj[:, 2 * ATTN_WIDTH:3 * ATTN_WIDTH])
    a_s = proj[:, 3 * ATTN_WIDTH:3 * ATTN_WIDTH + conv_w].reshape(dec_batch, dec_seq, conv_w)
    g_s = proj[:, 3 * ATTN_WIDTH + conv_w:].reshape(dec_batch, dec_seq, conv_w)

    pt_flat = page_table.reshape(-1)
    kmean_s = _page_mean(pt_flat, cache_k[l], dec_batch, n_pages).transpose(0, 2, 1, 3)
    idx_raw = _select_sample(q_s, kmean_s)
    idx = idx_raw[:, :MOBA_TOPK, :N_HEADS * dec_seq].reshape(dec_batch, MOBA_TOPK, N_HEADS, dec_seq)
    idx_flat = idx.transpose(0, 2, 3, 1).reshape(-1)
    o_s = _moba_sample(pt_flat, idx_flat, slopes, q_s, k_s, v_s, cache_k[l], cache_v[l], n_pages)
    o_attn_s = o_s.transpose(0, 2, 1, 3).reshape(m_s, ATTN_WIDTH)
    conv_n_s, conv_buf_s = _conv_sample(state_conv[l], a_s, g_s, wdw, bdw, gln, bln, g_co)
    x1_s, h2_s = _out_proj(o_attn_s, conv_n_s.reshape(m_s, conv_w), w_out_bf, xs, g_ao, g_post, g_pf,
                           mod_s, m_s, None)
    act_s = _ffn_up(h2_s, w_gu_bf, m_s)
    y_s = _ffn_down(act_s, w_down_bf, x1_s, g_pffn, mod_s, m_s, None).reshape(dec_batch, dec_seq, d)

    return (y_p, y_s, k_pages[None], v_pages[None], conv_buf_p[None],
            k_s[None], v_s[None], conv_buf_s[None])
```

```python
import functools

import jax
import jax.numpy as jnp
from jax import lax
from jax.experimental import pallas as pl
from jax.experimental.pallas import tpu as pltpu

F32 = jnp.float32
BF16 = jnp.bfloat16

N_HEADS = 8
HEAD_DIM = 128
ATTN_WIDTH = N_HEADS * HEAD_DIM
CONV_K = 31
CONV_BUF = CONV_K - 1
CONV_HALO = 32
SUBLANES = 8
PAGE = 128
MOBA_BLOCK = 256
MOBA_TOPK = 3
PAGES_PER_BLOCK = MOBA_BLOCK // PAGE
ATTN_TILE = 2 * MOBA_BLOCK
ATTN_KEYS = 4 * MOBA_BLOCK
LOG2E = 1.4426950408889634
EPS = 1e-6
NEG = -1e30
MIB = 1024 * 1024

_NT = (((1,), (1,)), ((), ()))


def _params(sem, vmem_mib):
    return pltpu.CompilerParams(dimension_semantics=sem, vmem_limit_bytes=vmem_mib * MIB)


def _resident(shape, index_map):
    return pl.BlockSpec(shape, index_map, pipeline_mode=pl.Buffered(1))


def _rms(x, g):
    return x * lax.rsqrt(jnp.mean(x * x, axis=-1, keepdims=True) + EPS) * g


def _silu(x):
    return x * jax.nn.sigmoid(x)


def _mod2d(ref):
    v = ref[...]
    return v.reshape(v.shape[-2], v.shape[-1])


def _mod_spec(mod, which, tm, rows_per_group):
    if rows_per_group is None:
        return mod[which], pl.BlockSpec((tm, mod[which].shape[1]), lambda i, *_: (i, 0))
    tiles = rows_per_group // tm
    return mod, pl.BlockSpec((1, 1, mod.shape[2]), lambda i, *_: ((i // tiles) * 6 + which, 0, 0))


def _ada_kernel(c_ref, w_ref, b_ref, o_ref):
    s = _silu(c_ref[...]).astype(BF16)
    o_ref[...] = jnp.dot(s, w_ref[...].astype(BF16), preferred_element_type=F32) + b_ref[...]


def _ada(c_all, w, b):
    rows, d = c_all.shape
    n = w.shape[1]
    tn = 1024
    return pl.pallas_call(
        _ada_kernel,
        out_shape=jax.ShapeDtypeStruct((rows, n), F32),
        grid=(n // tn,),
        in_specs=[pl.BlockSpec((rows, d), lambda j: (0, 0)),
                  pl.BlockSpec((d, tn), lambda j: (0, j)),
                  pl.BlockSpec((1, tn), lambda j: (0, j))],
        out_specs=pl.BlockSpec((rows, tn), lambda j: (0, j)),
        compiler_params=_params(("arbitrary",), 40),
        name="ada_mod",
    )(c_all, w, b)


def _modnorm_kernel(x_ref, g_ref, sc_ref, sh_ref, o_ref):
    h = _rms(x_ref[...], g_ref[...]) * (1.0 + _mod2d(sc_ref)) + _mod2d(sh_ref)
    o_ref[...] = h.astype(BF16)


def _modnorm(x, g, mod, which_sc, which_sh, tm, rows_per_group):
    m, d = x.shape
    sc, sc_spec = _mod_spec(mod, which_sc, tm, rows_per_group)
    sh, sh_spec = _mod_spec(mod, which_sh, tm, rows_per_group)
    return pl.pallas_call(
        _modnorm_kernel,
        out_shape=jax.ShapeDtypeStruct((m, d), BF16),
        grid=(m // tm,),
        in_specs=[pl.BlockSpec((tm, d), lambda i: (i, 0)),
                  pl.BlockSpec((1, d), lambda i: (0, 0)), sc_spec, sh_spec],
        out_specs=pl.BlockSpec((tm, d), lambda i: (i, 0)),
        compiler_params=_params(("parallel",), 32),
        name="modnorm",
    )(x, g, sc, sh)


def _kv_proj_kernel(*refs, tm, with_mean, with_cast):
    h_ref, w_ref, hs_ref = refs[:3]
    n_in = 4 if with_cast else 3
    outs = refs[n_in:]
    pages_ref, heads_ref, s_ref = outs[:3]
    wbf_ref = refs[-1]

    @pl.when(pl.program_id(0) == 0)
    def _():
        wbf_ref[...] = w_ref[...].astype(BF16)
        s_ref[...] = jnp.dot(hs_ref[...], wbf_ref[...], preferred_element_type=F32)

    acc = jnp.dot(h_ref[...], wbf_ref[...], preferred_element_type=F32)
    for hh in range(N_HEADS):
        cs = slice(hh * HEAD_DIM, (hh + 1) * HEAD_DIM)
        heads_ref[0, hh] = acc[:, cs].astype(BF16)
        for p in range(tm // PAGE):
            pages_ref[0, p, hh] = acc[p * PAGE:(p + 1) * PAGE, cs]
    nxt = 3
    if with_mean:
        mean_ref = outs[nxt]
        nxt += 1
        for mb in range(tm // MOBA_BLOCK):
            mean_ref[mb] = jnp.mean(acc[mb * MOBA_BLOCK:(mb + 1) * MOBA_BLOCK, :], axis=0, keepdims=True)
    if with_cast:
        outs[nxt][...] = refs[3][...].astype(BF16)


def _kv_proj(h, w_in, col_block, hs, batch, seq, with_mean, cast_src=None):
    tm = 512
    m, d = h.shape
    ms = hs.shape[0]
    tiles = seq // tm
    steps = m // tm
    in_specs = [pl.BlockSpec((tm, d), lambda i: (i, 0)),
                _resident((d, ATTN_WIDTH), lambda i: (0, col_block)),
                _resident((ms, d), lambda i: (0, 0))]
    args = [h, w_in, hs]
    out_shape = [jax.ShapeDtypeStruct((batch, seq // PAGE, N_HEADS, PAGE, HEAD_DIM), F32),
                 jax.ShapeDtypeStruct((batch, N_HEADS, seq, HEAD_DIM), BF16),
                 jax.ShapeDtypeStruct((ms, ATTN_WIDTH), F32)]
    out_specs = [pl.BlockSpec((1, tm // PAGE, N_HEADS, PAGE, HEAD_DIM), lambda i: (i // tiles, i % tiles, 0, 0, 0)),
                 pl.BlockSpec((1, N_HEADS, tm, HEAD_DIM), lambda i: (i // tiles, 0, i % tiles, 0)),
                 pl.BlockSpec((ms, ATTN_WIDTH), lambda i: (0, 0))]
    if with_mean:
        out_shape.append(jax.ShapeDtypeStruct((m // MOBA_BLOCK, 1, ATTN_WIDTH), F32))
        out_specs.append(pl.BlockSpec((tm // MOBA_BLOCK, 1, ATTN_WIDTH), lambda i: (i, 0, 0)))
    if cast_src is not None:
        rows, cols = cast_src.shape
        slab = rows // steps
        in_specs.append(pl.BlockSpec((slab, cols), lambda i: (i, 0)))
        args.append(cast_src)
        out_shape.append(jax.ShapeDtypeStruct((rows, cols), BF16))
        out_specs.append(pl.BlockSpec((slab, cols), lambda i: (i, 0)))
    return pl.pallas_call(
        functools.partial(_kv_proj_kernel, tm=tm, with_mean=with_mean, with_cast=cast_src is not None),
        out_shape=out_shape,
        grid=(steps,),
        in_specs=in_specs,
        out_specs=out_specs,
        scratch_shapes=[pltpu.VMEM((d, ATTN_WIDTH), BF16)],
        compiler_params=_params(("arbitrary",), 48),
        name="kv_proj",
    )(*args)


def _q_proj_kernel(h_ref, w_ref, hs_ref, km_ref, q_ref, pen_ref, qs_ref, wbf_ref, sm_ref, *, tm, tiles):
    i = pl.program_id(0)

    @pl.when(i == 0)
    def _():
        wbf_ref[...] = w_ref[...].astype(BF16)
        qs_ref[...] = jnp.dot(hs_ref[...], wbf_ref[...], preferred_element_type=F32)

    acc = jnp.dot(h_ref[...], wbf_ref[...], preferred_element_type=F32)
    km = km_ref[0]
    nb = km.shape[0]
    jrow = lax.broadcasted_iota(jnp.int32, (nb, tm), 0)
    tcol = lax.broadcasted_iota(jnp.int32, (nb, tm), 1)
    n_full = (i % tiles) * (tm // MOBA_BLOCK) + lax.shift_right_logical(tcol, MOBA_BLOCK.bit_length() - 1)
    valid = jrow < n_full
    own = jrow == n_full
    pens = []
    for hh in range(N_HEADS):
        cs = slice(hh * HEAD_DIM, (hh + 1) * HEAD_DIM)
        qh = acc[:, cs]
        q_ref[0, hh] = (qh * (HEAD_DIM ** -0.5 * LOG2E)).astype(BF16)
        s = lax.dot_general(km[:, cs], qh, _NT, precision=lax.Precision.HIGHEST, preferred_element_type=F32)
        sm = jnp.where(valid, s, NEG)
        sm_ref[...] = sm
        rank = jnp.zeros((nb, tm), jnp.int32)
        for jp in range(nb):
            row = sm_ref[jp:jp + 1, :]
            gt = jnp.where(row > sm, 1, 0)
            ge = jnp.where(row >= sm, 1, 0)
            rank = rank + jnp.where(jrow > jp, ge, gt)
        keep = (valid & (rank < MOBA_TOPK)) | own
        pens.append(jnp.where(keep, 0.0, NEG))
    pen = jnp.concatenate(pens, axis=0)
    pen_ref[0] = pen.T.astype(BF16)


def _q_proj(h, w_in, hs, kmean, batch, seq):
    tm = ATTN_TILE
    m, d = h.shape
    ms = hs.shape[0]
    tiles = seq // tm
    nb = kmean.shape[1]
    return pl.pallas_call(
        functools.partial(_q_proj_kernel, tm=tm, tiles=tiles),
        out_shape=[jax.ShapeDtypeStruct((batch, N_HEADS, seq, HEAD_DIM), BF16),
                   jax.ShapeDtypeStruct((batch, seq, N_HEADS * nb), BF16),
                   jax.ShapeDtypeStruct((ms, ATTN_WIDTH), F32)],
        grid=(m // tm,),
        in_specs=[pl.BlockSpec((tm, d), lambda i: (i, 0)),
                  _resident((d, ATTN_WIDTH), lambda i: (0, 0)),
                  _resident((ms, d), lambda i: (0, 0)),
                  pl.BlockSpec((1, nb, ATTN_WIDTH), lambda i: (i // tiles, 0, 0))],
        out_specs=[pl.BlockSpec((1, N_HEADS, tm, HEAD_DIM), lambda i: (i // tiles, 0, i % tiles, 0)),
                   pl.BlockSpec((1, tm, N_HEADS * nb), lambda i: (i // tiles, i % tiles, 0)),
                   pl.BlockSpec((ms, ATTN_WIDTH), lambda i: (0, 0))],
        scratch_shapes=[pltpu.VMEM((d, ATTN_WIDTH), BF16), pltpu.VMEM((nb, tm), F32)],
        compiler_params=_params(("arbitrary",), 48),
        name="q_proj",
    )(h, w_in, hs, kmean)


def _fold_lanes(x, op):
    out = x[:, 0:128]
    for c in range(1, x.shape[1] // 128):
        out = op(out, x[:, c * 128:(c + 1) * 128])
    return out


def _moba_prompt_kernel(slopes_ref, q_ref, pen_ref, k_ref, v_ref, o_ref, ke_ref, s_ref, *, nb):
    tq, tk = ATTN_TILE, ATTN_KEYS
    h = pl.program_id(1)
    g = pl.program_id(2)
    slope = slopes_ref[h] * LOG2E
    seq = k_ref.shape[2]

    @pl.when(g == 0)
    def _():
        ke_ref[:, 0:HEAD_DIM] = k_ref[0, 0]
        blk = lax.shift_right_logical(lax.broadcasted_iota(jnp.int32, (seq, HEAD_DIM), 0),
                                      MOBA_BLOCK.bit_length() - 1)
        lane = lax.broadcasted_iota(jnp.int32, (seq, HEAD_DIM), 1)
        ke_ref[:, HEAD_DIM:2 * HEAD_DIM] = jnp.where(lane == h * nb + blk, 1.0, 0.0).astype(BF16)

    qe = jnp.concatenate([q_ref[0, 0], pen_ref[0]], axis=1)
    key_lane = lax.broadcasted_iota(jnp.int32, (1, tk), 1)

    def scores(c):
        ke = ke_ref[pl.ds(pl.multiple_of(c * tk, tk), tk), :]
        s = lax.dot_general(qe, ke, _NT, preferred_element_type=F32)
        return s + slope * (c * tk + key_lane).astype(F32)

    last = lax.shift_right_logical(g * tq, tk.bit_length() - 1)
    qpos = g * tq + lax.broadcasted_iota(jnp.int32, (tq, tk), 0)
    kpos = last * tk + lax.broadcasted_iota(jnp.int32, (tq, tk), 1)
    s = jnp.where(kpos <= qpos, scores(last), NEG)
    s_ref[last] = s
    mx = _fold_lanes(s, jnp.maximum)

    def pass1(c, mx):
        s = scores(c)
        s_ref[c] = s
        return jnp.maximum(mx, _fold_lanes(s, jnp.maximum))

    mx = lax.fori_loop(0, last, pass1, mx)
    m = jnp.max(mx, axis=1, keepdims=True)

    def pass2(c, carry):
        ls, acc = carry
        p = jnp.exp2(s_ref[c] - m)
        vc = v_ref[0, 0, pl.ds(pl.multiple_of(c * tk, tk), tk), :]
        return ls + _fold_lanes(p, jnp.add), acc + jnp.dot(p.astype(BF16), vc, preferred_element_type=F32)

    zeros = jnp.zeros((tq, HEAD_DIM), F32)
    ls, acc = lax.fori_loop(0, last + 1, pass2, (zeros, zeros))
    o_ref[0] = acc / jnp.sum(ls, axis=1, keepdims=True)


def _moba_prompt(slopes, q, pen, k, v):
    batch, _, seq, _ = q.shape
    nb = seq // MOBA_BLOCK
    nt = seq // ATTN_TILE
    grid_spec = pltpu.PrefetchScalarGridSpec(
        num_scalar_prefetch=1,
        grid=(batch, N_HEADS, nt),
        in_specs=[pl.BlockSpec((1, 1, ATTN_TILE, HEAD_DIM), lambda b, h, g, s: (b, h, g, 0)),
                  pl.BlockSpec((1, ATTN_TILE, N_HEADS * nb), lambda b, h, g, s: (b, g, 0)),
                  pl.BlockSpec((1, 1, seq, HEAD_DIM), lambda b, h, g, s: (b, h, 0, 0)),
                  pl.BlockSpec((1, 1, seq, HEAD_DIM), lambda b, h, g, s: (b, h, 0, 0))],
        out_specs=pl.BlockSpec((1, ATTN_TILE, HEAD_DIM), lambda b, h, g, s: (b, g, h)),
        scratch_shapes=[pltpu.VMEM((seq, 2 * HEAD_DIM), BF16),
                        pltpu.VMEM((seq // ATTN_KEYS, ATTN_TILE, ATTN_KEYS), F32)],
    )
    return pl.pallas_call(
        functools.partial(_moba_prompt_kernel, nb=nb),
        out_shape=jax.ShapeDtypeStruct((batch, seq, ATTN_WIDTH), F32),
        grid_spec=grid_spec,
        compiler_params=_params(("arbitrary", "arbitrary", "arbitrary"), 40),
        name="moba_prompt",
    )(slopes, q, pen, k, v)


def _glu_proj_kernel(h_ref, wa_ref, wg_ref, hs_ref, u_ref, us_ref, wabf_ref, wgbf_ref):
    @pl.when(pl.program_id(1) == 0)
    def _():
        wabf_ref[...] = wa_ref[...].astype(BF16)
        wgbf_ref[...] = wg_ref[...].astype(BF16)
        a = jnp.dot(hs_ref[...], wabf_ref[...], preferred_element_type=F32)
        g = jnp.dot(hs_ref[...], wgbf_ref[...], preferred_element_type=F32)
        us_ref[...] = a * jax.nn.sigmoid(g)

    a = jnp.dot(h_ref[...], wabf_ref[...], preferred_element_type=F32)
    g = jnp.dot(h_ref[...], wgbf_ref[...], preferred_element_type=F32)
    u_ref[...] = a * jax.nn.sigmoid(g)


def _glu_proj(h, w_in, hs, a_col0, g_col0, width):
    tm, tn = 512, 512
    m, d = h.shape
    ms = hs.shape[0]
    return pl.pallas_call(
        _glu_proj_kernel,
        out_shape=[jax.ShapeDtypeStruct((m, width), F32), jax.ShapeDtypeStruct((ms, width), F32)],
        grid=(width // tn, m // tm),
        in_specs=[pl.BlockSpec((tm, d), lambda n, i: (i, 0)),
                  pl.BlockSpec((d, tn), lambda n, i: (0, a_col0 // tn + n)),
                  pl.BlockSpec((d, tn), lambda n, i: (0, g_col0 // tn + n)),
                  _resident((ms, d), lambda n, i: (0, 0))],
        out_specs=[pl.BlockSpec((tm, tn), lambda n, i: (i, n)),
                   pl.BlockSpec((ms, tn), lambda n, i: (0, n))],
        scratch_shapes=[pltpu.VMEM((d, tn), BF16), pltpu.VMEM((d, tn), BF16)],
        compiler_params=_params(("arbitrary", "arbitrary"), 48),
        name="glu_proj",
    )(h, w_in, w_in, hs)


def _conv_tail(y, gln, bln, gco):
    mu = jnp.mean(y, axis=-1, keepdims=True)
    var = jnp.mean(jnp.square(y - mu), axis=-1, keepdims=True)
    z = _silu((y - mu) * lax.rsqrt(var + EPS) * gln + bln)
    return _rms(z, gco)


def _conv_prompt_kernel(u_ref, halo_ref, w_ref, bdw_ref, gln_ref, bln_ref, gco_ref, cn_ref, buf_ref,
                        ext_ref, z_ref, y_ref, *, tt):
    t = pl.program_id(1)
    ext_ref[0:CONV_HALO, :] = jnp.where(t == 0, 0.0, halo_ref[...])
    ext_ref[CONV_HALO:CONV_HALO + tt, :] = u_ref[...]
    width = u_ref.shape[1]
    ext_ref[CONV_HALO + tt:CONV_HALO + tt + SUBLANES, :] = jnp.zeros((SUBLANES, width), F32)
    lead = CONV_HALO - CONV_BUF
    zr = tt + SUBLANES
    for cb in range(width // 128):
        cs = slice(cb * 128, (cb + 1) * 128)
        y = None
        for b in range(SUBLANES):
            taps = [k for k in range(CONV_K) if (lead + k) % SUBLANES == b]
            z = None
            for k in taps:
                r0 = lead + k - b
                term = w_ref[k:k + 1, cs] * ext_ref[r0:r0 + zr, cs]
                z = term if z is None else z + term
            if b == 0:
                y = z[0:tt, :]
            else:
                z_ref[...] = z
                y = y + z_ref[b:b + tt, :]
        y_ref[:, cs] = y + bdw_ref[:, cs]
    cn_ref[...] = _conv_tail(y_ref[...], gln_ref[...], bln_ref[...], gco_ref[...]).astype(BF16)

    @pl.when(t == pl.num_programs(1) - 1)
    def _():
        buf_ref[0] = ext_ref[CONV_HALO + tt - CONV_BUF:CONV_HALO + tt, :]


def _conv_prompt(u, w_dw, b_dw, g_ln, b_ln, g_co, batch, seq):
    tt = 256
    m, width = u.shape
    tiles = seq // tt
    halo_per_tile = tt // CONV_HALO
    vec = pl.BlockSpec((1, width), lambda b, t: (0, 0))
    return pl.pallas_call(
        functools.partial(_conv_prompt_kernel, tt=tt),
        out_shape=[jax.ShapeDtypeStruct((m, width), BF16),
                   jax.ShapeDtypeStruct((batch, CONV_BUF, width), F32)],
        grid=(batch, tiles),
        in_specs=[pl.BlockSpec((tt, width), lambda b, t: (b * tiles + t, 0)),
                  pl.BlockSpec((CONV_HALO, width),
                               lambda b, t: (jnp.maximum((b * tiles + t) * halo_per_tile - 1, 0), 0)),
                  pl.BlockSpec((CONV_K, width), lambda b, t: (0, 0)), vec, vec, vec, vec],
        out_specs=[pl.BlockSpec((tt, width), lambda b, t: (b * tiles + t, 0)),
                   pl.BlockSpec((1, CONV_BUF, width), lambda b, t: (b, 0, 0))],
        scratch_shapes=[pltpu.VMEM((CONV_HALO + tt + SUBLANES, width), F32),
                        pltpu.VMEM((tt + SUBLANES, 128), F32),
                        pltpu.VMEM((tt, width), F32)],
        compiler_params=_params(("parallel", "arbitrary"), 32),
        name="conv_prompt",
    )(u, u, w_dw, b_dw, g_ln, b_ln, g_co)


def _conv_sample_kernel(st_ref, u_ref, w_ref, bdw_ref, gln_ref, bln_ref, gco_ref, cn_ref, buf_ref,
                        ext_ref, *, rows):
    ext_ref[0:CONV_BUF, :] = st_ref[0]
    ext_ref[CONV_BUF:CONV_BUF + rows, :] = u_ref[0]
    acc = jnp.zeros((rows, u_ref.shape[2]), F32)
    for k in range(CONV_K):
        acc = acc + w_ref[k:k + 1, :] * ext_ref[k:k + rows, :]
    y = acc + bdw_ref[...]
    cn_ref[0] = _conv_tail(y, gln_ref[...], bln_ref[...], gco_ref[...]).astype(BF16)
    buf_ref[0] = ext_ref[rows:rows + CONV_BUF, :]


def _conv_sample(state, u, w_dw, b_dw, g_ln, b_ln, g_co):
    batch, rows, width = u.shape
    vec = pl.BlockSpec((1, width), lambda b: (0, 0))
    tok = pl.BlockSpec((1, rows, width), lambda b: (b, 0, 0))
    buf = pl.BlockSpec((1, CONV_BUF, width), lambda b: (b, 0, 0))
    return pl.pallas_call(
        functools.partial(_conv_sample_kernel, rows=rows),
        out_shape=[jax.ShapeDtypeStruct((batch, rows, width), BF16),
                   jax.ShapeDtypeStruct((batch, CONV_BUF, width), F32)],
        grid=(batch,),
        in_specs=[buf, tok, pl.BlockSpec((CONV_K, width), lambda b: (0, 0)), vec, vec, vec, vec],
        out_specs=[tok, buf],
        scratch_shapes=[pltpu.VMEM((CONV_BUF + rows + 6, width), F32)],
        compiler_params=_params(("parallel",), 32),
        name="conv_sample",
    )(state, u, w_dw, b_dw, g_ln, b_ln, g_co)


def _out_proj_kernel(o_ref, cn_ref, w_ref, x_ref, gao_ref, gpm_ref, gpf_ref, gt_ref, sc_ref, sh_ref,
                     x1_ref, h2_ref):
    an = _rms(o_ref[...], gao_ref[...]).astype(BF16)
    mix = jnp.concatenate([an, cn_ref[...]], axis=1)
    merged = jnp.dot(mix, w_ref[...], preferred_element_type=F32)
    x1 = x_ref[...] + _mod2d(gt_ref) * _rms(merged, gpm_ref[...])
    x1_ref[...] = x1
    h2_ref[...] = (_rms(x1, gpf_ref[...]) * (1.0 + _mod2d(sc_ref)) + _mod2d(sh_ref)).astype(BF16)


def _out_proj(o_attn, conv_n, w_bf, x, g_ao, g_pm, g_pf, mod, tm, rows_per_group):
    m, d = x.shape
    aw = o_attn.shape[1]
    cw = conv_n.shape[1]
    gt, gt_spec = _mod_spec(mod, 2, tm, rows_per_group)
    sc, sc_spec = _mod_spec(mod, 4, tm, rows_per_group)
    sh, sh_spec = _mod_spec(mod, 3, tm, rows_per_group)
    row = lambda w: pl.BlockSpec((tm, w), lambda i: (i, 0))
    vec = lambda w: pl.BlockSpec((1, w), lambda i: (0, 0))
    return pl.pallas_call(
        _out_proj_kernel,
        out_shape=[jax.ShapeDtypeStruct((m, d), F32), jax.ShapeDtypeStruct((m, d), BF16)],
        grid=(m // tm,),
        in_specs=[row(aw), row(cw), _resident((aw + cw, d), lambda i: (0, 0)), row(d),
                  vec(aw), vec(d), vec(d), gt_spec, sc_spec, sh_spec],
        out_specs=[row(d), row(d)],
        compiler_params=_params(("parallel",), 48),
        name="out_proj",
    )(o_attn, conv_n, w_bf, x, g_ao, g_pm, g_pf, gt, sc, sh)


def _ffn_up_kernel(h_ref, wg_ref, wu_ref, hs_ref, wd_ref, a_ref, as_ref, wdbf_ref, wgbf_ref, wubf_ref):
    @pl.when(pl.program_id(1) == 0)
    def _():
        wgbf_ref[...] = wg_ref[...].astype(BF16)
        wubf_ref[...] = wu_ref[...].astype(BF16)
        g = jnp.dot(hs_ref[...], wgbf_ref[...], preferred_element_type=F32)
        u = jnp.dot(hs_ref[...], wubf_ref[...], preferred_element_type=F32)
        as_ref[...] = (_silu(g) * u).astype(BF16)

    g = jnp.dot(h_ref[...], wgbf_ref[...], preferred_element_type=F32)
    u = jnp.dot(h_ref[...], wubf_ref[...], preferred_element_type=F32)
    a_ref[...] = (_silu(g) * u).astype(BF16)
    wdbf_ref[...] = wd_ref[...].astype(BF16)


def _ffn_up(h, w_gate_up, hs, w_down):
    tm, tn = 512, 512
    m, d = h.shape
    ms = hs.shape[0]
    d_ff = w_gate_up.shape[1] // 2
    n_blocks, m_tiles = d_ff // tn, m // tm
    slab = w_down.shape[0] // (n_blocks * m_tiles)
    return pl.pallas_call(
        _ffn_up_kernel,
        out_shape=[jax.ShapeDtypeStruct((m, d_ff), BF16), jax.ShapeDtypeStruct((ms, d_ff), BF16),
                   jax.ShapeDtypeStruct(w_down.shape, BF16)],
        grid=(n_blocks, m_tiles),
        in_specs=[pl.BlockSpec((tm, d), lambda n, i: (i, 0)),
                  pl.BlockSpec((d, tn), lambda n, i: (0, n)),
                  pl.BlockSpec((d, tn), lambda n, i: (0, n_blocks + n)),
                  _resident((ms, d), lambda n, i: (0, 0)),
                  pl.BlockSpec((slab, w_down.shape[1]), lambda n, i: (n * m_tiles + i, 0))],
        out_specs=[pl.BlockSpec((tm, tn), lambda n, i: (i, n)),
                   pl.BlockSpec((ms, tn), lambda n, i: (0, n)),
                   pl.BlockSpec((slab, w_down.shape[1]), lambda n, i: (n * m_tiles + i, 0))],
        scratch_shapes=[pltpu.VMEM((d, tn), BF16), pltpu.VMEM((d, tn), BF16)],
        compiler_params=_params(("arbitrary", "arbitrary"), 48),
        name="ffn_up",
    )(h, w_gate_up, w_gate_up, hs, w_down)


def _ffn_down_kernel(a_ref, w_ref, x1_ref, g_ref, gt_ref, y_ref):
    z = jnp.dot(a_ref[...], w_ref[...], preferred_element_type=F32)
    y_ref[...] = x1_ref[...] + _mod2d(gt_ref) * _rms(z, g_ref[...])


def _ffn_down(act, w_bf, x1, g_post, mod, tm, rows_per_group):
    m, d_ff = act.shape
    d = w_bf.shape[1]
    gt, gt_spec = _mod_spec(mod, 5, tm, rows_per_group)
    return pl.pallas_call(
        _ffn_down_kernel,
        out_shape=jax.ShapeDtypeStruct((m, d), F32),
        grid=(m // tm,),
        in_specs=[pl.BlockSpec((tm, d_ff), lambda i: (i, 0)),
                  _resident((d_ff, d), lambda i: (0, 0)),
                  pl.BlockSpec((tm, d), lambda i: (i, 0)),
                  pl.BlockSpec((1, d), lambda i: (0, 0)), gt_spec],
        out_specs=pl.BlockSpec((tm, d), lambda i: (i, 0)),
        compiler_params=_params(("parallel",), 48),
        name="ffn_down",
    )(act, w_bf, x1, g_post, gt)


def _page_mean_kernel(pt_ref, *refs, pages_per_step):
    page_refs, o_ref = refs[:pages_per_step], refs[pages_per_step]
    for blk in range(pages_per_step // PAGES_PER_BLOCK):
        tot = jnp.zeros((N_HEADS, HEAD_DIM), F32)
        for pp in range(PAGES_PER_BLOCK):
            tot = tot + jnp.sum(page_refs[blk * PAGES_PER_BLOCK + pp][0], axis=1)
        o_ref[0, blk] = tot * (1.0 / MOBA_BLOCK)


def _page_mean(page_table_flat, pool_k, batch, n_pages):
    pages_per_step = 16
    steps = n_pages // pages_per_step
    blocks_per_step = pages_per_step // PAGES_PER_BLOCK

    def page_spec(p):
        return pl.BlockSpec((1, N_HEADS, PAGE, HEAD_DIM),
                            lambda b, s, pt: (pt[b * n_pages + s * pages_per_step + p], 0, 0, 0))

    grid_spec = pltpu.PrefetchScalarGridSpec(
        num_scalar_prefetch=1,
        grid=(batch, steps),
        in_specs=[page_spec(p) for p in range(pages_per_step)],
        out_specs=pl.BlockSpec((1, blocks_per_step, N_HEADS, HEAD_DIM), lambda b, s, pt: (b, s, 0, 0)),
    )
    return pl.pallas_call(
        functools.partial(_page_mean_kernel, pages_per_step=pages_per_step),
        out_shape=jax.ShapeDtypeStruct((batch, n_pages // PAGES_PER_BLOCK, N_HEADS, HEAD_DIM), F32),
        grid_spec=grid_spec,
        compiler_params=_params(("parallel", "parallel"), 48),
        name="page_mean",
    )(page_table_flat, *([pool_k] * pages_per_step))


def _select_sample_kernel(q_ref, km_ref, idx_ref, *, n_q):
    nb = km_ref.shape[2]
    lane = lax.broadcasted_iota(jnp.int32, (nb, 128), 1)
    rowi = lax.broadcasted_iota(jnp.int32, (nb, 128), 0)
    s = jnp.full((nb, 128), NEG, F32)
    for hh in range(N_HEADS):
        km = km_ref[0, hh]
        for t in range(n_q):
            col = jnp.sum(km * q_ref[0, hh, t:t + 1, :], axis=1, keepdims=True)
            s = jnp.where(lane == hh * n_q + t, col, s)
    out_row = lax.broadcasted_iota(jnp.int32, (8, 128), 0)
    out = jnp.zeros((8, 128), jnp.int32)
    for r in range(MOBA_TOPK):
        top = jnp.max(s, axis=0, keepdims=True)
        arg = jnp.min(jnp.where(s == top, rowi, nb), axis=0, keepdims=True)
        out = jnp.where(out_row == r, arg, out)
        s = jnp.where(rowi == arg, -jnp.inf, s)
    idx_ref[0] = out


def _select_sample(q, kmean):
    batch, _, n_q, _ = q.shape
    nb = kmean.shape[2]
    return pl.pallas_call(
        functools.partial(_select_sample_kernel, n_q=n_q),
        out_shape=jax.ShapeDtypeStruct((batch, 8, 128), jnp.int32),
        grid=(batch,),
        in_specs=[pl.BlockSpec((1, N_HEADS, n_q, HEAD_DIM), lambda b: (b, 0, 0, 0)),
                  pl.BlockSpec((1, N_HEADS, nb, HEAD_DIM), lambda b: (b, 0, 0, 0))],
        out_specs=pl.BlockSpec((1, 8, 128), lambda b: (b, 0, 0)),
        compiler_params=_params(("parallel",), 32),
        name="select_sample",
    )(q, kmean)


def _moba_sample_kernel(pt_ref, idx_ref, slopes_ref, q_ref, kn_ref, vn_ref, *refs, n_q, past):
    n_sel = n_q * MOBA_TOPK * PAGES_PER_BLOCK
    k_refs, v_refs, o_ref = refs[:n_sel], refs[n_sel:2 * n_sel], refs[2 * n_sel]
    b = pl.program_id(0)
    h = pl.program_id(1)
    slope = slopes_ref[h]
    key = lax.broadcasted_iota(jnp.int32, (PAGE, 1), 0)
    for t in range(n_q):
        q = q_ref[0, 0, t:t + 1, :] * (HEAD_DIM ** -0.5)
        scores = []
        for s in range(MOBA_TOPK):
            blk = idx_ref[((b * N_HEADS + h) * n_q + t) * MOBA_TOPK + s]
            for pp in range(PAGES_PER_BLOCK):
                kp = k_refs[(t * MOBA_TOPK + s) * PAGES_PER_BLOCK + pp][0, 0]
                dist = (past + t - blk * MOBA_BLOCK - pp * PAGE - key).astype(F32)
                scores.append(jnp.sum(kp * q, axis=1, keepdims=True) - slope * dist)
        own = []
        for t2 in range(t + 1):
            own.append(jnp.sum(kn_ref[0, 0, t2:t2 + 1, :] * q, axis=1, keepdims=True) - slope * float(t - t2))
        m = own[0]
        for sc in scores:
            m = jnp.maximum(m, jnp.max(sc, axis=0, keepdims=True))
        for sc in own[1:]:
            m = jnp.maximum(m, sc)
        l = jnp.zeros((1, 1), F32)
        acc = jnp.zeros((1, HEAD_DIM), F32)
        for n, sc in enumerate(scores):
            p = jnp.exp(sc - m)
            l = l + jnp.sum(p, axis=0, keepdims=True)
            acc = acc + jnp.sum(p * v_refs[n + t * MOBA_TOPK * PAGES_PER_BLOCK][0, 0], axis=0, keepdims=True)
        for t2, sc in enumerate(own):
            p = jnp.exp(sc - m)
            l = l + p
            acc = acc + p * vn_ref[0, 0, t2:t2 + 1, :]
        o_ref[0, 0, t:t + 1, :] = acc / l


def _moba_sample(page_table_flat, idx_flat, slopes, q, k_new, v_new, pool_k, pool_v, n_pages):
    batch, _, n_q, _ = q.shape
    past = n_pages * PAGE

    def sel_spec(t, s, pp):
        def index(b, h, pt, idx, sl):
            blk = idx[((b * N_HEADS + h) * n_q + t) * MOBA_TOPK + s]
            return (pt[b * n_pages + blk * PAGES_PER_BLOCK + pp], h, 0, 0)
        return pl.BlockSpec((1, 1, PAGE, HEAD_DIM), index)

    sel_specs = [sel_spec(t, s, pp) for t in range(n_q) for s in range(MOBA_TOPK) for pp in range(PAGES_PER_BLOCK)]
    tok = pl.BlockSpec((1, 1, n_q, HEAD_DIM), lambda b, h, pt, idx, sl: (b, h, 0, 0))
    grid_spec = pltpu.PrefetchScalarGridSpec(
        num_scalar_prefetch=3,
        grid=(batch, N_HEADS),
        in_specs=[tok, tok, tok] + sel_specs + sel_specs,
        out_specs=tok,
    )
    n_sel = len(sel_specs)
    return pl.pallas_call(
        functools.partial(_moba_sample_kernel, n_q=n_q, past=past),
        out_shape=jax.ShapeDtypeStruct((batch, N_HEADS, n_q, HEAD_DIM), F32),
        grid_spec=grid_spec,
        compiler_params=_params(("parallel", "parallel"), 32),
        name="moba_sample",
    )(page_table_flat, idx_flat, slopes, q, k_new, v_new, *([pool_k] * n_sel), *([pool_v] * n_sel))


def kernel(x_prompt, x_sample, cache_k, cache_v, state_conv, page_table, c_prompt, c_sample, w_ada, b_ada, g_pre_mix, w_in, w_dw, b_dw, g_conv_ln, b_conv_ln, g_attn_out, g_conv_out, w_out, g_post_mix, g_pre_ffn, w_gate_up, w_down, g_post_ffn):
    depth = w_ada.shape[0]
    assert depth == 1, "single layer: the prompt and sample residual streams are not chained across layers here"
    batch, seq, d = x_prompt.shape
    dec_batch, dec_seq, _ = x_sample.shape
    n_pages = page_table.shape[1]
    past = n_pages * PAGE
    conv_w = w_dw.shape[-1]
    assert past % MOBA_BLOCK == 0 and dec_seq <= MOBA_BLOCK and past // MOBA_BLOCK >= MOBA_TOPK
    assert seq % ATTN_TILE == 0 and d == ATTN_WIDTH + conv_w

    slopes = 2.0 ** (-8.0 * (jnp.arange(N_HEADS, dtype=F32) + 1.0) / N_HEADS)
    l = 0
    vec = lambda a: a[l].reshape(1, -1)
    g_pm, g_ao, g_co = vec(g_pre_mix), vec(g_attn_out), vec(g_conv_out)
    g_post, g_pf, g_pffn = vec(g_post_mix), vec(g_pre_ffn), vec(g_post_ffn)
    bdw, gln, bln = vec(b_dw), vec(g_conv_ln), vec(b_conv_ln)
    wdw = w_dw[l].reshape(CONV_K, conv_w)

    n_c = batch + dec_batch
    c_rows = -(-n_c // 8) * 8
    c_all = jnp.concatenate([c_prompt, c_sample, jnp.zeros((c_rows - n_c, d), F32)], axis=0)
    mod = _ada(c_all, w_ada[l], vec(b_ada))
    mod_p = mod.reshape(c_rows * 6, 1, d)
    mod_rows = jnp.repeat(mod[batch:n_c].reshape(dec_batch, 6, d), dec_seq, axis=0)
    mod_s = [mod_rows[:, w] for w in range(6)]

    m_s = dec_batch * dec_seq
    xp = x_prompt.reshape(batch * seq, d)
    xs = x_sample.reshape(m_s, d)
    h = _modnorm(xp, g_pm, mod_p, 1, 0, 512, seq)
    hs = _modnorm(xs, g_pm, mod_s, 1, 0, m_s, None)

    k_pages, k_heads, k_rows_s, kmean, w_out_bf = _kv_proj(h, w_in[l], 1, hs, batch, seq, True, w_out[l])
    v_pages, v_heads, v_rows_s = _kv_proj(h, w_in[l], 2, hs, batch, seq, False)
    q_heads, pen, q_rows_s = _q_proj(h, w_in[l], hs, kmean.reshape(batch, seq // MOBA_BLOCK, ATTN_WIDTH),
                                     batch, seq)
    u, u_s = _glu_proj(h, w_in[l], hs, 3 * ATTN_WIDTH, 3 * ATTN_WIDTH + conv_w, conv_w)

    o_attn = _moba_prompt(slopes, q_heads, pen, k_heads, v_heads).reshape(batch * seq, ATTN_WIDTH)
    conv_n, conv_buf_p = _conv_prompt(u, wdw, bdw, gln, bln, g_co, batch, seq)
    x1, h2 = _out_proj(o_attn, conv_n, w_out_bf, xp, g_ao, g_post, g_pf, mod_p, 256, seq)

    to_heads = lambda t: t.reshape(dec_batch, dec_seq, N_HEADS, HEAD_DIM).transpose(0, 2, 1, 3)
    q_s, k_s, v_s = to_heads(q_rows_s), to_heads(k_rows_s), to_heads(v_rows_s)
    pt_flat = page_table.reshape(-1)
    kmean_s = _page_mean(pt_flat, cache_k[l], dec_batch, n_pages).transpose(0, 2, 1, 3)
    idx_raw = _select_sample(q_s, kmean_s)
    idx = idx_raw[:, :MOBA_TOPK, :N_HEADS * dec_seq].reshape(dec_batch, MOBA_TOPK, N_HEADS, dec_seq)
    idx_flat = idx.transpose(0, 2, 3, 1).reshape(-1)
    o_s = _moba_sample(pt_flat, idx_flat, slopes, q_s, k_s, v_s, cache_k[l], cache_v[l], n_pages)
    o_attn_s = o_s.transpose(0, 2, 1, 3).reshape(m_s, ATTN_WIDTH)
    conv_n_s, conv_buf_s = _conv_sample(state_conv[l], u_s.reshape(dec_batch, dec_seq, conv_w),
                                        wdw, bdw, gln, bln, g_co)
    x1_s, h2_s = _out_proj(o_attn_s, conv_n_s.reshape(m_s, conv_w), w_out_bf, xs, g_ao, g_post, g_pf,
                           mod_s, m_s, None)

    act, act_s, w_down_bf = _ffn_up(h2, w_gate_up[l], h2_s, w_down[l])
    y_p = _ffn_down(act, w_down_bf, x1, g_pffn, mod_p, 256, seq).reshape(batch, seq, d)
    y_s = _ffn_down(act_s, w_down_bf, x1_s, g_pffn, mod_s, m_s, None).reshape(dec_batch, dec_seq, d)

    return (y_p, y_s, k_pages[None], v_pages[None], conv_buf_p[None],
            k_s[None], v_s[None], conv_buf_s[None])
```

```python
import functools

import jax
import jax.numpy as jnp
from jax import lax
from jax.experimental import pallas as pl
from jax.experimental.pallas import tpu as pltpu

F32 = jnp.float32
BF16 = jnp.bfloat16

N_HEADS = 8
HEAD_DIM = 128
ATTN_WIDTH = N_HEADS * HEAD_DIM
CONV_K = 31
CONV_BUF = CONV_K - 1
CONV_HALO = 32
SUBLANES = 8
PAGE = 128
MOBA_BLOCK = 256
MOBA_TOPK = 3
PAGES_PER_BLOCK = MOBA_BLOCK // PAGE
ATTN_TILE = 2 * MOBA_BLOCK
ATTN_KEYS = 4 * MOBA_BLOCK
LOG2E = 1.4426950408889634
EPS = 1e-6
NEG = -1e30
MIB = 1024 * 1024

_NT = (((1,), (1,)), ((), ()))


def _params(sem, vmem_mib):
    return pltpu.CompilerParams(dimension_semantics=sem, vmem_limit_bytes=vmem_mib * MIB)


def _resident(shape, index_map):
    return pl.BlockSpec(shape, index_map, pipeline_mode=pl.Buffered(1))


def _rms(x, g):
    return x * lax.rsqrt(jnp.mean(x * x, axis=-1, keepdims=True) + EPS) * g


def _silu(x):
    return x * jax.nn.sigmoid(x)


def _mod2d(ref):
    v = ref[...]
    return v.reshape(v.shape[-2], v.shape[-1])


def _mod_spec(mod, which, tm, rows_per_group):
    if rows_per_group is None:
        return mod[which], pl.BlockSpec((tm, mod[which].shape[1]), lambda i, *_: (i, 0))
    tiles = rows_per_group // tm
    return mod, pl.BlockSpec((1, 1, mod.shape[2]), lambda i, *_: ((i // tiles) * 6 + which, 0, 0))


def _ada_kernel(c_ref, w_ref, b_ref, o_ref):
    s = _silu(c_ref[...]).astype(BF16)
    o_ref[...] = jnp.dot(s, w_ref[...].astype(BF16), preferred_element_type=F32) + b_ref[...]


def _ada(c_all, w, b):
    rows, d = c_all.shape
    n = w.shape[1]
    tn = 1024
    return pl.pallas_call(
        _ada_kernel,
        out_shape=jax.ShapeDtypeStruct((rows, n), F32),
        grid=(n // tn,),
        in_specs=[pl.BlockSpec((rows, d), lambda j: (0, 0)),
                  pl.BlockSpec((d, tn), lambda j: (0, j)),
                  pl.BlockSpec((1, tn), lambda j: (0, j))],
        out_specs=pl.BlockSpec((rows, tn), lambda j: (0, j)),
        compiler_params=_params(("arbitrary",), 40),
        name="ada_mod",
    )(c_all, w, b)


def _modnorm_kernel(x_ref, g_ref, sc_ref, sh_ref, o_ref):
    h = _rms(x_ref[...], g_ref[...]) * (1.0 + _mod2d(sc_ref)) + _mod2d(sh_ref)
    o_ref[...] = h.astype(BF16)


def _modnorm(x, g, mod, which_sc, which_sh, tm, rows_per_group):
    m, d = x.shape
    sc, sc_spec = _mod_spec(mod, which_sc, tm, rows_per_group)
    sh, sh_spec = _mod_spec(mod, which_sh, tm, rows_per_group)
    return pl.pallas_call(
        _modnorm_kernel,
        out_shape=jax.ShapeDtypeStruct((m, d), BF16),
        grid=(m // tm,),
        in_specs=[pl.BlockSpec((tm, d), lambda i: (i, 0)),
                  pl.BlockSpec((1, d), lambda i: (0, 0)), sc_spec, sh_spec],
        out_specs=pl.BlockSpec((tm, d), lambda i: (i, 0)),
        compiler_params=_params(("parallel",), 32),
        name="modnorm",
    )(x, g, sc, sh)


def _kv_proj_kernel(*refs, tm, with_mean, with_cast):
    h_ref, w_ref, hs_ref = refs[:3]
    n_in = 4 if with_cast else 3
    outs = refs[n_in:]
    pages_ref, heads_ref, s_ref = outs[:3]
    wbf_ref = refs[-1]

    @pl.when(pl.program_id(0) == 0)
    def _():
        wbf_ref[...] = w_ref[...].astype(BF16)
        s_ref[...] = jnp.dot(hs_ref[...], wbf_ref[...], preferred_element_type=F32)

    acc = jnp.dot(h_ref[...], wbf_ref[...], preferred_element_type=F32)
    for hh in range(N_HEADS):
        cs = slice(hh * HEAD_DIM, (hh + 1) * HEAD_DIM)
        heads_ref[0, hh] = acc[:, cs].astype(BF16)
        for p in range(tm // PAGE):
            pages_ref[0, p, hh] = acc[p * PAGE:(p + 1) * PAGE, cs]
    nxt = 3
    if with_mean:
        mean_ref = outs[nxt]
        nxt += 1
        for mb in range(tm // MOBA_BLOCK):
            mean_ref[mb] = jnp.mean(acc[mb * MOBA_BLOCK:(mb + 1) * MOBA_BLOCK, :], axis=0, keepdims=True)
    if with_cast:
        outs[nxt][...] = refs[3][...].astype(BF16)


def _kv_proj(h, w_in, col_block, hs, batch, seq, with_mean, cast_src=None):
    tm = 512
    m, d = h.shape
    ms = hs.shape[0]
    tiles = seq // tm
    steps = m // tm
    in_specs = [pl.BlockSpec((tm, d), lambda i: (i, 0)),
                _resident((d, ATTN_WIDTH), lambda i: (0, col_block)),
                _resident((ms, d), lambda i: (0, 0))]
    args = [h, w_in, hs]
    out_shape = [jax.ShapeDtypeStruct((batch, seq // PAGE, N_HEADS, PAGE, HEAD_DIM), F32),
                 jax.ShapeDtypeStruct((batch, N_HEADS, seq, HEAD_DIM), BF16),
                 jax.ShapeDtypeStruct((ms, ATTN_WIDTH), F32)]
    out_specs = [pl.BlockSpec((1, tm // PAGE, N_HEADS, PAGE, HEAD_DIM), lambda i: (i // tiles, i % tiles, 0, 0, 0)),
                 pl.BlockSpec((1, N_HEADS, tm, HEAD_DIM), lambda i: (i // tiles, 0, i % tiles, 0)),
                 pl.BlockSpec((ms, ATTN_WIDTH), lambda i: (0, 0))]
    if with_mean:
        out_shape.append(jax.ShapeDtypeStruct((m // MOBA_BLOCK, 1, ATTN_WIDTH), F32))
        out_specs.append(pl.BlockSpec((tm // MOBA_BLOCK, 1, ATTN_WIDTH), lambda i: (i, 0, 0)))
    if cast_src is not None:
        rows, cols = cast_src.shape
        slab = rows // steps
        in_specs.append(pl.BlockSpec((slab, cols), lambda i: (i, 0)))
        args.append(cast_src)
        out_shape.append(jax.ShapeDtypeStruct((rows, cols), BF16))
        out_specs.append(pl.BlockSpec((slab, cols), lambda i: (i, 0)))
    return pl.pallas_call(
        functools.partial(_kv_proj_kernel, tm=tm, with_mean=with_mean, with_cast=cast_src is not None),
        out_shape=out_shape,
        grid=(steps,),
        in_specs=in_specs,
        out_specs=out_specs,
        scratch_shapes=[pltpu.VMEM((d, ATTN_WIDTH), BF16)],
        compiler_params=_params(("arbitrary",), 48),
        name="kv_proj",
    )(*args)


def _q_proj_kernel(h_ref, w_ref, hs_ref, km_ref, q_ref, pen_ref, qs_ref, wbf_ref, sm_ref, *, tm, tiles):
    i = pl.program_id(0)

    @pl.when(i == 0)
    def _():
        wbf_ref[...] = w_ref[...].astype(BF16)
        qs_ref[...] = jnp.dot(hs_ref[...], wbf_ref[...], preferred_element_type=F32)

    acc = jnp.dot(h_ref[...], wbf_ref[...], preferred_element_type=F32)
    km = km_ref[0]
    nb = km.shape[0]
    jrow = lax.broadcasted_iota(jnp.int32, (nb, tm), 0)
    tcol = lax.broadcasted_iota(jnp.int32, (nb, tm), 1)
    n_full = (i % tiles) * (tm // MOBA_BLOCK) + lax.shift_right_logical(tcol, MOBA_BLOCK.bit_length() - 1)
    valid = jrow < n_full
    own = jrow == n_full
    pens = []
    for hh in range(N_HEADS):
        cs = slice(hh * HEAD_DIM, (hh + 1) * HEAD_DIM)
        qh = acc[:, cs]
        q_ref[0, hh] = (qh * (HEAD_DIM ** -0.5 * LOG2E)).astype(BF16)
        s = lax.dot_general(km[:, cs], qh, _NT, precision=lax.Precision.HIGHEST, preferred_element_type=F32)
        sm = jnp.where(valid, s, NEG)
        sm_ref[...] = sm
        rank = jnp.zeros((nb, tm), jnp.int32)
        for jp in range(nb):
            row = sm_ref[jp:jp + 1, :]
            gt = jnp.where(row > sm, 1, 0)
            ge = jnp.where(row >= sm, 1, 0)
            rank = rank + jnp.where(jrow > jp, ge, gt)
        keep = (valid & (rank < MOBA_TOPK)) | own
        pens.append(jnp.where(keep, 0.0, NEG))
    pen = jnp.concatenate(pens, axis=0)
    pen_ref[0] = pen.T.astype(BF16)


def _q_proj(h, w_in, hs, kmean, batch, seq):
    tm = ATTN_TILE
    m, d = h.shape
    ms = hs.shape[0]
    tiles = seq // tm
    nb = kmean.shape[1]
    return pl.pallas_call(
        functools.partial(_q_proj_kernel, tm=tm, tiles=tiles),
        out_shape=[jax.ShapeDtypeStruct((batch, N_HEADS, seq, HEAD_DIM), BF16),
                   jax.ShapeDtypeStruct((batch, seq, N_HEADS * nb), BF16),
                   jax.ShapeDtypeStruct((ms, ATTN_WIDTH), F32)],
        grid=(m // tm,),
        in_specs=[pl.BlockSpec((tm, d), lambda i: (i, 0)),
                  _resident((d, ATTN_WIDTH), lambda i: (0, 0)),
                  _resident((ms, d), lambda i: (0, 0)),
                  pl.BlockSpec((1, nb, ATTN_WIDTH), lambda i: (i // tiles, 0, 0))],
        out_specs=[pl.BlockSpec((1, N_HEADS, tm, HEAD_DIM), lambda i: (i // tiles, 0, i % tiles, 0)),
                   pl.BlockSpec((1, tm, N_HEADS * nb), lambda i: (i // tiles, i % tiles, 0)),
                   pl.BlockSpec((ms, ATTN_WIDTH), lambda i: (0, 0))],
        scratch_shapes=[pltpu.VMEM((d, ATTN_WIDTH), BF16), pltpu.VMEM((nb, tm), F32)],
        compiler_params=_params(("arbitrary",), 48),
        name="q_proj",
    )(h, w_in, hs, kmean)


def _fold_lanes(x, op):
    out = x[:, 0:128]
    for c in range(1, x.shape[1] // 128):
        out = op(out, x[:, c * 128:(c + 1) * 128])
    return out


def _block_key_means(page_refs, o_ref):
    for blk in range(len(page_refs) // PAGES_PER_BLOCK):
        tot = jnp.zeros((N_HEADS, HEAD_DIM), F32)
        for pp in range(PAGES_PER_BLOCK):
            tot = tot + jnp.sum(page_refs[blk * PAGES_PER_BLOCK + pp][0], axis=1)
        o_ref[0, blk] = tot * (1.0 / MOBA_BLOCK)


def _moba_prompt_kernel(slopes_ref, pt_ref, q_ref, pen_ref, k_ref, v_ref, *refs, nb, n_side):
    page_refs = refs[:n_side]
    o_ref, km_ref, ke_ref, ve_ref, s_ref = refs[n_side:]
    _block_key_means(page_refs, km_ref)

    tq, tk = ATTN_TILE, ATTN_KEYS
    h = pl.program_id(1)
    g = pl.program_id(2)
    slope = slopes_ref[h] * LOG2E
    seq = k_ref.shape[2]

    @pl.when(g == 0)
    def _():
        blk = lax.shift_right_logical(lax.broadcasted_iota(jnp.int32, (seq, HEAD_DIM), 0),
                                      MOBA_BLOCK.bit_length() - 1)
        lane = lax.broadcasted_iota(jnp.int32, (seq, HEAD_DIM), 1)
        ke_ref[:, 0:HEAD_DIM] = k_ref[0, 0]
        ke_ref[:, HEAD_DIM:2 * HEAD_DIM] = jnp.where(lane == h * nb + blk, 1.0, 0.0).astype(BF16)
        ve_ref[:, 0:HEAD_DIM] = v_ref[0, 0]
        ve_ref[:, HEAD_DIM:2 * HEAD_DIM] = jnp.where(lane == 0, 1.0, 0.0).astype(BF16)

    qe = jnp.concatenate([q_ref[0, 0], pen_ref[0]], axis=1)
    key_lane = lax.broadcasted_iota(jnp.int32, (1, tk), 1)

    def scores(c):
        ke = ke_ref[pl.ds(pl.multiple_of(c * tk, tk), tk), :]
        s = lax.dot_general(qe, ke, _NT, preferred_element_type=F32)
        return s + slope * (c * tk + key_lane).astype(F32)

    last = lax.shift_right_logical(g * tq, tk.bit_length() - 1)
    qpos = g * tq + lax.broadcasted_iota(jnp.int32, (tq, tk), 0)
    kpos = last * tk + lax.broadcasted_iota(jnp.int32, (tq, tk), 1)
    s = jnp.where(kpos <= qpos, scores(last), NEG)
    s_ref[last] = s
    mx = _fold_lanes(s, jnp.maximum)

    def pass1(c, mx):
        s = scores(c)
        s_ref[c] = s
        return jnp.maximum(mx, _fold_lanes(s, jnp.maximum))

    mx = lax.fori_loop(0, last, pass1, mx)
    m = jnp.max(mx, axis=1, keepdims=True)

    def pass2(c, acc):
        p = jnp.exp2(s_ref[c] - m)
        ve = ve_ref[pl.ds(pl.multiple_of(c * tk, tk), tk), :]
        return acc + jnp.dot(p.astype(BF16), ve, preferred_element_type=F32)

    acc = lax.fori_loop(0, last + 1, pass2, jnp.zeros((tq, 2 * HEAD_DIM), F32))
    o_ref[0] = acc[:, 0:HEAD_DIM] / acc[:, HEAD_DIM:HEAD_DIM + 1]


def _moba_prompt(slopes, page_table_flat, q, pen, k, v, pool_k, side_batch, n_pages):
    batch, _, seq, _ = q.shape
    nb = seq // MOBA_BLOCK
    nt = seq // ATTN_TILE
    steps = batch * N_HEADS * nt
    n_side = side_batch * n_pages // steps
    assert n_side * steps == side_batch * n_pages and n_side % PAGES_PER_BLOCK == 0 and n_pages % n_side == 0
    steps_per_seq = n_pages // n_side

    def step(b, h, g):
        return (b * N_HEADS + h) * nt + g

    def page_spec(p):
        return pl.BlockSpec((1, N_HEADS, PAGE, HEAD_DIM),
                            lambda b, h, g, sl, pt: (pt[step(b, h, g) * n_side + p], 0, 0, 0))

    grid_spec = pltpu.PrefetchScalarGridSpec(
        num_scalar_prefetch=2,
        grid=(batch, N_HEADS, nt),
        in_specs=[pl.BlockSpec((1, 1, ATTN_TILE, HEAD_DIM), lambda b, h, g, sl, pt: (b, h, g, 0)),
                  pl.BlockSpec((1, ATTN_TILE, N_HEADS * nb), lambda b, h, g, sl, pt: (b, g, 0)),
                  pl.BlockSpec((1, 1, seq, HEAD_DIM), lambda b, h, g, sl, pt: (b, h, 0, 0)),
                  pl.BlockSpec((1, 1, seq, HEAD_DIM), lambda b, h, g, sl, pt: (b, h, 0, 0))]
                 + [page_spec(p) for p in range(n_side)],
        out_specs=[pl.BlockSpec((1, ATTN_TILE, HEAD_DIM), lambda b, h, g, sl, pt: (b, g, h)),
                   pl.BlockSpec((1, n_side // PAGES_PER_BLOCK, N_HEADS, HEAD_DIM),
                                lambda b, h, g, sl, pt: (step(b, h, g) // steps_per_seq,
                                                         step(b, h, g) % steps_per_seq, 0, 0))],
        scratch_shapes=[pltpu.VMEM((seq, 2 * HEAD_DIM), BF16),
                        pltpu.VMEM((seq, 2 * HEAD_DIM), BF16),
                        pltpu.VMEM((seq // ATTN_KEYS, ATTN_TILE, ATTN_KEYS), F32)],
    )
    return pl.pallas_call(
        functools.partial(_moba_prompt_kernel, nb=nb, n_side=n_side),
        out_shape=[jax.ShapeDtypeStruct((batch, seq, ATTN_WIDTH), F32),
                   jax.ShapeDtypeStruct((side_batch, n_pages // PAGES_PER_BLOCK, N_HEADS, HEAD_DIM), F32)],
        grid_spec=grid_spec,
        compiler_params=_params(("arbitrary", "arbitrary", "arbitrary"), 48),
        name="moba_prompt",
    )(slopes, page_table_flat, q, pen, k, v, *([pool_k] * n_side))


def _glu_proj_kernel(h_ref, wa_ref, wg_ref, hs_ref, u_ref, us_ref, wabf_ref, wgbf_ref):
    @pl.when(pl.program_id(1) == 0)
    def _():
        wabf_ref[...] = wa_ref[...].astype(BF16)
        wgbf_ref[...] = wg_ref[...].astype(BF16)
        a = jnp.dot(hs_ref[...], wabf_ref[...], preferred_element_type=F32)
        g = jnp.dot(hs_ref[...], wgbf_ref[...], preferred_element_type=F32)
        us_ref[...] = a * jax.nn.sigmoid(g)

    a = jnp.dot(h_ref[...], wabf_ref[...], preferred_element_type=F32)
    g = jnp.dot(h_ref[...], wgbf_ref[...], preferred_element_type=F32)
    u_ref[...] = a * jax.nn.sigmoid(g)


def _glu_proj(h, w_in, hs, a_col0, g_col0, width):
    tm, tn = 512, 512
    m, d = h.shape
    ms = hs.shape[0]
    return pl.pallas_call(
        _glu_proj_kernel,
        out_shape=[jax.ShapeDtypeStruct((m, width), F32), jax.ShapeDtypeStruct((ms, width), F32)],
        grid=(width // tn, m // tm),
        in_specs=[pl.BlockSpec((tm, d), lambda n, i: (i, 0)),
                  pl.BlockSpec((d, tn), lambda n, i: (0, a_col0 // tn + n)),
                  pl.BlockSpec((d, tn), lambda n, i: (0, g_col0 // tn + n)),
                  _resident((ms, d), lambda n, i: (0, 0))],
        out_specs=[pl.BlockSpec((tm, tn), lambda n, i: (i, n)),
                   pl.BlockSpec((ms, tn), lambda n, i: (0, n))],
        scratch_shapes=[pltpu.VMEM((d, tn), BF16), pltpu.VMEM((d, tn), BF16)],
        compiler_params=_params(("arbitrary", "arbitrary"), 48),
        name="glu_proj",
    )(h, w_in, w_in, hs)


def _conv_tail(y, gln, bln, gco):
    mu = jnp.mean(y, axis=-1, keepdims=True)
    var = jnp.mean(jnp.square(y - mu), axis=-1, keepdims=True)
    z = _silu((y - mu) * lax.rsqrt(var + EPS) * gln + bln)
    return _rms(z, gco)


def _conv_prompt_kernel(u_ref, halo_ref, w_ref, bdw_ref, gln_ref, bln_ref, gco_ref, cn_ref, buf_ref,
                        ext_ref, z_ref, y_ref, *, tt):
    t = pl.program_id(1)
    ext_ref[0:CONV_HALO, :] = jnp.where(t == 0, 0.0, halo_ref[...])
    ext_ref[CONV_HALO:CONV_HALO + tt, :] = u_ref[...]
    width = u_ref.shape[1]
    ext_ref[CONV_HALO + tt:CONV_HALO + tt + SUBLANES, :] = jnp.zeros((SUBLANES, width), F32)
    lead = CONV_HALO - CONV_BUF
    zr = tt + SUBLANES
    for cb in range(width // 128):
        cs = slice(cb * 128, (cb + 1) * 128)
        y = None
        for b in range(SUBLANES):
            taps = [k for k in range(CONV_K) if (lead + k) % SUBLANES == b]
            z = None
            for k in taps:
                r0 = lead + k - b
                term = w_ref[k:k + 1, cs] * ext_ref[r0:r0 + zr, cs]
                z = term if z is None else z + term
            if b == 0:
                y = z[0:tt, :]
            else:
                z_ref[...] = z
                y = y + z_ref[b:b + tt, :]
        y_ref[:, cs] = y + bdw_ref[:, cs]
    cn_ref[...] = _conv_tail(y_ref[...], gln_ref[...], bln_ref[...], gco_ref[...]).astype(BF16)

    @pl.when(t == pl.num_programs(1) - 1)
    def _():
        buf_ref[0] = ext_ref[CONV_HALO + tt - CONV_BUF:CONV_HALO + tt, :]


def _conv_prompt(u, w_dw, b_dw, g_ln, b_ln, g_co, batch, seq):
    tt = 256
    m, width = u.shape
    tiles = seq // tt
    halo_per_tile = tt // CONV_HALO
    vec = pl.BlockSpec((1, width), lambda b, t: (0, 0))
    return pl.pallas_call(
        functools.partial(_conv_prompt_kernel, tt=tt),
        out_shape=[jax.ShapeDtypeStruct((m, width), BF16),
                   jax.ShapeDtypeStruct((batch, CONV_BUF, width), F32)],
        grid=(batch, tiles),
        in_specs=[pl.BlockSpec((tt, width), lambda b, t: (b * tiles + t, 0)),
                  pl.BlockSpec((CONV_HALO, width),
                               lambda b, t: (jnp.maximum((b * tiles + t) * halo_per_tile - 1, 0), 0)),
                  pl.BlockSpec((CONV_K, width), lambda b, t: (0, 0)), vec, vec, vec, vec],
        out_specs=[pl.BlockSpec((tt, width), lambda b, t: (b * tiles + t, 0)),
                   pl.BlockSpec((1, CONV_BUF, width), lambda b, t: (b, 0, 0))],
        scratch_shapes=[pltpu.VMEM((CONV_HALO + tt + SUBLANES, width), F32),
                        pltpu.VMEM((tt + SUBLANES, 128), F32),
                        pltpu.VMEM((tt, width), F32)],
        compiler_params=_params(("parallel", "arbitrary"), 32),
        name="conv_prompt",
    )(u, u, w_dw, b_dw, g_ln, b_ln, g_co)


def _conv_sample_kernel(st_ref, u_ref, w_ref, bdw_ref, gln_ref, bln_ref, gco_ref, cn_ref, buf_ref,
                        ext_ref, *, rows):
    ext_ref[0:CONV_BUF, :] = st_ref[0]
    ext_ref[CONV_BUF:CONV_BUF + rows, :] = u_ref[0]
    acc = jnp.zeros((rows, u_ref.shape[2]), F32)
    for k in range(CONV_K):
        acc = acc + w_ref[k:k + 1, :] * ext_ref[k:k + rows, :]
    y = acc + bdw_ref[...]
    cn_ref[0] = _conv_tail(y, gln_ref[...], bln_ref[...], gco_ref[...]).astype(BF16)
    buf_ref[0] = ext_ref[rows:rows + CONV_BUF, :]


def _conv_sample(state, u, w_dw, b_dw, g_ln, b_ln, g_co):
    batch, rows, width = u.shape
    vec = pl.BlockSpec((1, width), lambda b: (0, 0))
    tok = pl.BlockSpec((1, rows, width), lambda b: (b, 0, 0))
    buf = pl.BlockSpec((1, CONV_BUF, width), lambda b: (b, 0, 0))
    return pl.pallas_call(
        functools.partial(_conv_sample_kernel, rows=rows),
        out_shape=[jax.ShapeDtypeStruct((batch, rows, width), BF16),
                   jax.ShapeDtypeStruct((batch, CONV_BUF, width), F32)],
        grid=(batch,),
        in_specs=[buf, tok, pl.BlockSpec((CONV_K, width), lambda b: (0, 0)), vec, vec, vec, vec],
        out_specs=[tok, buf],
        scratch_shapes=[pltpu.VMEM((CONV_BUF + rows + 6, width), F32)],
        compiler_params=_params(("parallel",), 32),
        name="conv_sample",
    )(state, u, w_dw, b_dw, g_ln, b_ln, g_co)


def _out_proj_kernel(o_ref, cn_ref, w_ref, x_ref, gao_ref, gpm_ref, gpf_ref, gt_ref, sc_ref, sh_ref,
                     x1_ref, h2_ref):
    an = _rms(o_ref[...], gao_ref[...]).astype(BF16)
    mix = jnp.concatenate([an, cn_ref[...]], axis=1)
    merged = jnp.dot(mix, w_ref[...], preferred_element_type=F32)
    x1 = x_ref[...] + _mod2d(gt_ref) * _rms(merged, gpm_ref[...])
    x1_ref[...] = x1
    h2_ref[...] = (_rms(x1, gpf_ref[...]) * (1.0 + _mod2d(sc_ref)) + _mod2d(sh_ref)).astype(BF16)


def _out_proj(o_attn, conv_n, w_bf, x, g_ao, g_pm, g_pf, mod, tm, rows_per_group):
    m, d = x.shape
    aw = o_attn.shape[1]
    cw = conv_n.shape[1]
    gt, gt_spec = _mod_spec(mod, 2, tm, rows_per_group)
    sc, sc_spec = _mod_spec(mod, 4, tm, rows_per_group)
    sh, sh_spec = _mod_spec(mod, 3, tm, rows_per_group)
    row = lambda w: pl.BlockSpec((tm, w), lambda i: (i, 0))
    vec = lambda w: pl.BlockSpec((1, w), lambda i: (0, 0))
    return pl.pallas_call(
        _out_proj_kernel,
        out_shape=[jax.ShapeDtypeStruct((m, d), F32), jax.ShapeDtypeStruct((m, d), BF16)],
        grid=(m // tm,),
        in_specs=[row(aw), row(cw), _resident((aw + cw, d), lambda i: (0, 0)), row(d),
                  vec(aw), vec(d), vec(d), gt_spec, sc_spec, sh_spec],
        out_specs=[row(d), row(d)],
        compiler_params=_params(("parallel",), 56),
        name="out_proj",
    )(o_attn, conv_n, w_bf, x, g_ao, g_pm, g_pf, gt, sc, sh)


def _ffn_up_kernel(h_ref, wg_ref, wu_ref, hs_ref, wd_ref, a_ref, as_ref, wdbf_ref, wgbf_ref, wubf_ref):
    @pl.when(pl.program_id(1) == 0)
    def _():
        wgbf_ref[...] = wg_ref[...].astype(BF16)
        wubf_ref[...] = wu_ref[...].astype(BF16)
        g = jnp.dot(hs_ref[...], wgbf_ref[...], preferred_element_type=F32)
        u = jnp.dot(hs_ref[...], wubf_ref[...], preferred_element_type=F32)
        as_ref[...] = (_silu(g) * u).astype(BF16)

    g = jnp.dot(h_ref[...], wgbf_ref[...], preferred_element_type=F32)
    u = jnp.dot(h_ref[...], wubf_ref[...], preferred_element_type=F32)
    a_ref[...] = (_silu(g) * u).astype(BF16)
    wdbf_ref[...] = wd_ref[...].astype(BF16)


def _ffn_up(h, w_gate_up, hs, w_down):
    tm, tn = 1024, 512
    m, d = h.shape
    ms = hs.shape[0]
    d_ff = w_gate_up.shape[1] // 2
    n_blocks, m_tiles = d_ff // tn, m // tm
    slab = w_down.shape[0] // (n_blocks * m_tiles)
    return pl.pallas_call(
        _ffn_up_kernel,
        out_shape=[jax.ShapeDtypeStruct((m, d_ff), BF16), jax.ShapeDtypeStruct((ms, d_ff), BF16),
                   jax.ShapeDtypeStruct(w_down.shape, BF16)],
        grid=(n_blocks, m_tiles),
        in_specs=[pl.BlockSpec((tm, d), lambda n, i: (i, 0)),
                  pl.BlockSpec((d, tn), lambda n, i: (0, n)),
                  pl.BlockSpec((d, tn), lambda n, i: (0, n_blocks + n)),
                  _resident((ms, d), lambda n, i: (0, 0)),
                  pl.BlockSpec((slab, w_down.shape[1]), lambda n, i: (n * m_tiles + i, 0))],
        out_specs=[pl.BlockSpec((tm, tn), lambda n, i: (i, n)),
                   pl.BlockSpec((ms, tn), lambda n, i: (0, n)),
                   pl.BlockSpec((slab, w_down.shape[1]), lambda n, i: (n * m_tiles + i, 0))],
        scratch_shapes=[pltpu.VMEM((d, tn), BF16), pltpu.VMEM((d, tn), BF16)],
        compiler_params=_params(("arbitrary", "arbitrary"), 56),
        name="ffn_up",
    )(h, w_gate_up, w_gate_up, hs, w_down)


def _ffn_down_kernel(a_ref, w_ref, x1_ref, g_ref, gt_ref, y_ref):
    z = jnp.dot(a_ref[...], w_ref[...], preferred_element_type=F32)
    y_ref[...] = x1_ref[...] + _mod2d(gt_ref) * _rms(z, g_ref[...])


def _ffn_down(act, w_bf, x1, g_post, mod, tm, rows_per_group):
    m, d_ff = act.shape
    d = w_bf.shape[1]
    gt, gt_spec = _mod_spec(mod, 5, tm, rows_per_group)
    return pl.pallas_call(
        _ffn_down_kernel,
        out_shape=jax.ShapeDtypeStruct((m, d), F32),
        grid=(m // tm,),
        in_specs=[pl.BlockSpec((tm, d_ff), lambda i: (i, 0)),
                  _resident((d_ff, d), lambda i: (0, 0)),
                  pl.BlockSpec((tm, d), lambda i: (i, 0)),
                  pl.BlockSpec((1, d), lambda i: (0, 0)), gt_spec],
        out_specs=pl.BlockSpec((tm, d), lambda i: (i, 0)),
        compiler_params=_params(("parallel",), 48),
        name="ffn_down",
    )(act, w_bf, x1, g_post, gt)


def _select_sample_kernel(q_ref, km_ref, idx_ref, *, n_q):
    nb = km_ref.shape[2]
    lane = lax.broadcasted_iota(jnp.int32, (nb, 128), 1)
    rowi = lax.broadcasted_iota(jnp.int32, (nb, 128), 0)
    s = jnp.full((nb, 128), NEG, F32)
    for hh in range(N_HEADS):
        km = km_ref[0, hh]
        for t in range(n_q):
            col = jnp.sum(km * q_ref[0, hh, t:t + 1, :], axis=1, keepdims=True)
            s = jnp.where(lane == hh * n_q + t, col, s)
    out_row = lax.broadcasted_iota(jnp.int32, (8, 128), 0)
    out = jnp.zeros((8, 128), jnp.int32)
    for r in range(MOBA_TOPK):
        top = jnp.max(s, axis=0, keepdims=True)
        arg = jnp.min(jnp.where(s == top, rowi, nb), axis=0, keepdims=True)
        out = jnp.where(out_row == r, arg, out)
        s = jnp.where(rowi == arg, -jnp.inf, s)
    idx_ref[0] = out


def _select_sample(q, kmean):
    batch, _, n_q, _ = q.shape
    nb = kmean.shape[2]
    return pl.pallas_call(
        functools.partial(_select_sample_kernel, n_q=n_q),
        out_shape=jax.ShapeDtypeStruct((batch, 8, 128), jnp.int32),
        grid=(batch,),
        in_specs=[pl.BlockSpec((1, N_HEADS, n_q, HEAD_DIM), lambda b: (b, 0, 0, 0)),
                  pl.BlockSpec((1, N_HEADS, nb, HEAD_DIM), lambda b: (b, 0, 0, 0))],
        out_specs=pl.BlockSpec((1, 8, 128), lambda b: (b, 0, 0)),
        compiler_params=_params(("parallel",), 32),
        name="select_sample",
    )(q, kmean)


def _moba_sample_kernel(pt_ref, idx_ref, slopes_ref, q_ref, kn_ref, vn_ref, *refs, n_q, past):
    n_sel = n_q * MOBA_TOPK * PAGES_PER_BLOCK
    k_refs, v_refs, o_ref = refs[:n_sel], refs[n_sel:2 * n_sel], refs[2 * n_sel]
    b = pl.program_id(0)
    h = pl.program_id(1)
    slope = slopes_ref[h]
    key = lax.broadcasted_iota(jnp.int32, (PAGE, 1), 0)
    for t in range(n_q):
        q = q_ref[0, 0, t:t + 1, :] * (HEAD_DIM ** -0.5)
        scores = []
        for s in range(MOBA_TOPK):
            blk = idx_ref[((b * N_HEADS + h) * n_q + t) * MOBA_TOPK + s]
            for pp in range(PAGES_PER_BLOCK):
                kp = k_refs[(t * MOBA_TOPK + s) * PAGES_PER_BLOCK + pp][0, 0]
                dist = (past + t - blk * MOBA_BLOCK - pp * PAGE - key).astype(F32)
                scores.append(jnp.sum(kp * q, axis=1, keepdims=True) - slope * dist)
        own = []
        for t2 in range(t + 1):
            own.append(jnp.sum(kn_ref[0, 0, t2:t2 + 1, :] * q, axis=1, keepdims=True) - slope * float(t - t2))
        m = own[0]
        for sc in scores:
            m = jnp.maximum(m, jnp.max(sc, axis=0, keepdims=True))
        for sc in own[1:]:
            m = jnp.maximum(m, sc)
        l = jnp.zeros((1, 1), F32)
        acc = jnp.zeros((1, HEAD_DIM), F32)
        for n, sc in enumerate(scores):
            p = jnp.exp(sc - m)
            l = l + jnp.sum(p, axis=0, keepdims=True)
            acc = acc + jnp.sum(p * v_refs[n + t * MOBA_TOPK * PAGES_PER_BLOCK][0, 0], axis=0, keepdims=True)
        for t2, sc in enumerate(own):
            p = jnp.exp(sc - m)
            l = l + p
            acc = acc + p * vn_ref[0, 0, t2:t2 + 1, :]
        o_ref[0, 0, t:t + 1, :] = acc / l


def _moba_sample(page_table_flat, idx_flat, slopes, q, k_new, v_new, pool_k, pool_v, n_pages):
    batch, _, n_q, _ = q.shape
    past = n_pages * PAGE

    def sel_spec(t, s, pp):
        def index(b, h, pt, idx, sl):
            blk = idx[((b * N_HEADS + h) * n_q + t) * MOBA_TOPK + s]
            return (pt[b * n_pages + blk * PAGES_PER_BLOCK + pp], h, 0, 0)
        return pl.BlockSpec((1, 1, PAGE, HEAD_DIM), index)

    sel_specs = [sel_spec(t, s, pp) for t in range(n_q) for s in range(MOBA_TOPK) for pp in range(PAGES_PER_BLOCK)]
    tok = pl.BlockSpec((1, 1, n_q, HEAD_DIM), lambda b, h, pt, idx, sl: (b, h, 0, 0))
    grid_spec = pltpu.PrefetchScalarGridSpec(
        num_scalar_prefetch=3,
        grid=(batch, N_HEADS),
        in_specs=[tok, tok, tok] + sel_specs + sel_specs,
        out_specs=tok,
    )
    n_sel = len(sel_specs)
    return pl.pallas_call(
        functools.partial(_moba_sample_kernel, n_q=n_q, past=past),
        out_shape=jax.ShapeDtypeStruct((batch, N_HEADS, n_q, HEAD_DIM), F32),
        grid_spec=grid_spec,
        compiler_params=_params(("parallel", "parallel"), 32),
        name="moba_sample",
    )(page_table_flat, idx_flat, slopes, q, k_new, v_new, *([pool_k] * n_sel), *([pool_v] * n_sel))


def kernel(x_prompt, x_sample, cache_k, cache_v, state_conv, page_table, c_prompt, c_sample, w_ada, b_ada, g_pre_mix, w_in, w_dw, b_dw, g_conv_ln, b_conv_ln, g_attn_out, g_conv_out, w_out, g_post_mix, g_pre_ffn, w_gate_up, w_down, g_post_ffn):
    depth = w_ada.shape[0]
    assert depth == 1, "single layer: the prompt and sample residual streams are not chained across layers here"
    batch, seq, d = x_prompt.shape
    dec_batch, dec_seq, _ = x_sample.shape
    n_pages = page_table.shape[1]
    past = n_pages * PAGE
    conv_w = w_dw.shape[-1]
    assert past % MOBA_BLOCK == 0 and dec_seq <= MOBA_BLOCK and past // MOBA_BLOCK >= MOBA_TOPK
    assert seq % ATTN_TILE == 0 and d == ATTN_WIDTH + conv_w

    slopes = 2.0 ** (-8.0 * (jnp.arange(N_HEADS, dtype=F32) + 1.0) / N_HEADS)
    l = 0
    vec = lambda a: a[l].reshape(1, -1)
    g_pm, g_ao, g_co = vec(g_pre_mix), vec(g_attn_out), vec(g_conv_out)
    g_post, g_pf, g_pffn = vec(g_post_mix), vec(g_pre_ffn), vec(g_post_ffn)
    bdw, gln, bln = vec(b_dw), vec(g_conv_ln), vec(b_conv_ln)
    wdw = w_dw[l].reshape(CONV_K, conv_w)

    n_c = batch + dec_batch
    c_rows = -(-n_c // 8) * 8
    c_all = jnp.concatenate([c_prompt, c_sample, jnp.zeros((c_rows - n_c, d), F32)], axis=0)
    mod = _ada(c_all, w_ada[l], vec(b_ada))
    mod_p = mod.reshape(c_rows * 6, 1, d)
    mod_rows = jnp.repeat(mod[batch:n_c].reshape(dec_batch, 6, d), dec_seq, axis=0)
    mod_s = [mod_rows[:, w] for w in range(6)]

    m_s = dec_batch * dec_seq
    xp = x_prompt.reshape(batch * seq, d)
    xs = x_sample.reshape(m_s, d)
    h = _modnorm(xp, g_pm, mod_p, 1, 0, 512, seq)
    hs = _modnorm(xs, g_pm, mod_s, 1, 0, m_s, None)

    k_pages, k_heads, k_rows_s, kmean, w_out_bf = _kv_proj(h, w_in[l], 1, hs, batch, seq, True, w_out[l])
    v_pages, v_heads, v_rows_s = _kv_proj(h, w_in[l], 2, hs, batch, seq, False)
    q_heads, pen, q_rows_s = _q_proj(h, w_in[l], hs, kmean.reshape(batch, seq // MOBA_BLOCK, ATTN_WIDTH),
                                     batch, seq)
    u, u_s = _glu_proj(h, w_in[l], hs, 3 * ATTN_WIDTH, 3 * ATTN_WIDTH + conv_w, conv_w)

    pt_flat = page_table.reshape(-1)
    o_attn, kmean_s = _moba_prompt(slopes, pt_flat, q_heads, pen, k_heads, v_heads, cache_k[l], dec_batch, n_pages)
    o_attn = o_attn.reshape(batch * seq, ATTN_WIDTH)
    conv_n, conv_buf_p = _conv_prompt(u, wdw, bdw, gln, bln, g_co, batch, seq)
    x1, h2 = _out_proj(o_attn, conv_n, w_out_bf, xp, g_ao, g_post, g_pf, mod_p, 512, seq)

    to_heads = lambda t: t.reshape(dec_batch, dec_seq, N_HEADS, HEAD_DIM).transpose(0, 2, 1, 3)
    q_s, k_s, v_s = to_heads(q_rows_s), to_heads(k_rows_s), to_heads(v_rows_s)
    kmean_s = kmean_s.transpose(0, 2, 1, 3)
    idx_raw = _select_sample(q_s, kmean_s)
    idx = idx_raw[:, :MOBA_TOPK, :N_HEADS * dec_seq].reshape(dec_batch, MOBA_TOPK, N_HEADS, dec_seq)
    idx_flat = idx.transpose(0, 2, 3, 1).reshape(-1)
    o_s = _moba_sample(pt_flat, idx_flat, slopes, q_s, k_s, v_s, cache_k[l], cache_v[l], n_pages)
    o_attn_s = o_s.transpose(0, 2, 1, 3).reshape(m_s, ATTN_WIDTH)
    conv_n_s, conv_buf_s = _conv_sample(state_conv[l], u_s.reshape(dec_batch, dec_seq, conv_w),
                                        wdw, bdw, gln, bln, g_co)
    x1_s, h2_s = _out_proj(o_attn_s, conv_n_s.reshape(m_s, conv_w), w_out_bf, xs, g_ao, g_post, g_pf,
                           mod_s, m_s, None)

    act, act_s, w_down_bf = _ffn_up(h2, w_gate_up[l], h2_s, w_down[l])
    y_p = _ffn_down(act, w_down_bf, x1, g_pffn, mod_p, 256, seq).reshape(batch, seq, d)
    y_s = _ffn_down(act_s, w_down_bf, x1_s, g_pffn, mod_s, m_s, None).reshape(dec_batch, dec_seq, d)

    return (y_p, y_s, k_pages[None], v_pages[None], conv_buf_p[None],
            k_s[None], v_s[None], conv_buf_s[None])
```

```python
import functools

import jax
import jax.numpy as jnp
from jax import lax
from jax.experimental import pallas as pl
from jax.experimental.pallas import tpu as pltpu

F32 = jnp.float32
BF16 = jnp.bfloat16

N_HEADS = 8
HEAD_DIM = 128
ATTN_WIDTH = N_HEADS * HEAD_DIM
CONV_K = 31
CONV_BUF = CONV_K - 1
CONV_HALO = 32
SUBLANES = 8
PAGE = 128
MOBA_BLOCK = 256
MOBA_TOPK = 3
PAGES_PER_BLOCK = MOBA_BLOCK // PAGE
ATTN_TILE = 4 * MOBA_BLOCK
PROJ_TILE = 512
ATTN_KEYS = 4 * MOBA_BLOCK
LOG2E = 1.4426950408889634
EPS = 1e-6
NEG = -1e30
MIB = 1024 * 1024

_NT = (((1,), (1,)), ((), ()))


def _params(sem, vmem_mib):
    return pltpu.CompilerParams(dimension_semantics=sem, vmem_limit_bytes=vmem_mib * MIB)


def _resident(shape, index_map):
    return pl.BlockSpec(shape, index_map, pipeline_mode=pl.Buffered(1))


def _rms(x, g):
    return x * lax.rsqrt(jnp.mean(x * x, axis=-1, keepdims=True) + EPS) * g


def _silu(x):
    return x * jax.nn.sigmoid(x)


def _mod2d(ref):
    v = ref[...]
    return v.reshape(v.shape[-2], v.shape[-1])


def _mod_spec(mod, which, tm, rows_per_group):
    if rows_per_group is None:
        return mod[which], pl.BlockSpec((tm, mod[which].shape[1]), lambda i, *_: (i, 0))
    tiles = rows_per_group // tm
    return mod, pl.BlockSpec((1, 1, mod.shape[2]), lambda i, *_: ((i // tiles) * 6 + which, 0, 0))


def _ada_kernel(c_ref, w_ref, b_ref, o_ref):
    s = _silu(c_ref[...]).astype(BF16)
    o_ref[...] = jnp.dot(s, w_ref[...].astype(BF16), preferred_element_type=F32) + b_ref[...]


def _ada(c_all, w, b):
    rows, d = c_all.shape
    n = w.shape[1]
    tn = 1024
    return pl.pallas_call(
        _ada_kernel,
        out_shape=jax.ShapeDtypeStruct((rows, n), F32),
        grid=(n // tn,),
        in_specs=[pl.BlockSpec((rows, d), lambda j: (0, 0)),
                  pl.BlockSpec((d, tn), lambda j: (0, j)),
                  pl.BlockSpec((1, tn), lambda j: (0, j))],
        out_specs=pl.BlockSpec((rows, tn), lambda j: (0, j)),
        compiler_params=_params(("arbitrary",), 40),
        name="ada_mod",
    )(c_all, w, b)


def _modnorm_kernel(x_ref, g_ref, sc_ref, sh_ref, o_ref):
    h = _rms(x_ref[...], g_ref[...]) * (1.0 + _mod2d(sc_ref)) + _mod2d(sh_ref)
    o_ref[...] = h.astype(BF16)


def _modnorm(x, g, mod, which_sc, which_sh, tm, rows_per_group):
    m, d = x.shape
    sc, sc_spec = _mod_spec(mod, which_sc, tm, rows_per_group)
    sh, sh_spec = _mod_spec(mod, which_sh, tm, rows_per_group)
    return pl.pallas_call(
        _modnorm_kernel,
        out_shape=jax.ShapeDtypeStruct((m, d), BF16),
        grid=(m // tm,),
        in_specs=[pl.BlockSpec((tm, d), lambda i: (i, 0)),
                  pl.BlockSpec((1, d), lambda i: (0, 0)), sc_spec, sh_spec],
        out_specs=pl.BlockSpec((tm, d), lambda i: (i, 0)),
        compiler_params=_params(("parallel",), 32),
        name="modnorm",
    )(x, g, sc, sh)


def _kv_proj_kernel(*refs, tm, with_mean, with_cast):
    h_ref, w_ref, hs_ref = refs[:3]
    n_in = 4 if with_cast else 3
    outs = refs[n_in:]
    pages_ref, heads_ref, s_ref = outs[:3]
    wbf_ref = refs[-1]

    @pl.when(pl.program_id(0) == 0)
    def _():
        wbf_ref[...] = w_ref[...].astype(BF16)
        s_ref[...] = jnp.dot(hs_ref[...], wbf_ref[...], preferred_element_type=F32)

    acc = jnp.dot(h_ref[...], wbf_ref[...], preferred_element_type=F32)
    for hh in range(N_HEADS):
        cs = slice(hh * HEAD_DIM, (hh + 1) * HEAD_DIM)
        heads_ref[0, hh] = acc[:, cs].astype(BF16)
        for p in range(tm // PAGE):
            pages_ref[0, p, hh] = acc[p * PAGE:(p + 1) * PAGE, cs]
    nxt = 3
    if with_mean:
        mean_ref = outs[nxt]
        nxt += 1
        for mb in range(tm // MOBA_BLOCK):
            mean_ref[mb] = jnp.mean(acc[mb * MOBA_BLOCK:(mb + 1) * MOBA_BLOCK, :], axis=0, keepdims=True)
    if with_cast:
        outs[nxt][...] = refs[3][...].astype(BF16)


def _kv_proj(h, w_in, col_block, hs, batch, seq, with_mean, cast_src=None):
    tm = 512
    m, d = h.shape
    ms = hs.shape[0]
    tiles = seq // tm
    steps = m // tm
    in_specs = [pl.BlockSpec((tm, d), lambda i: (i, 0)),
                _resident((d, ATTN_WIDTH), lambda i: (0, col_block)),
                _resident((ms, d), lambda i: (0, 0))]
    args = [h, w_in, hs]
    out_shape = [jax.ShapeDtypeStruct((batch, seq // PAGE, N_HEADS, PAGE, HEAD_DIM), F32),
                 jax.ShapeDtypeStruct((batch, N_HEADS, seq, HEAD_DIM), BF16),
                 jax.ShapeDtypeStruct((ms, ATTN_WIDTH), F32)]
    out_specs = [pl.BlockSpec((1, tm // PAGE, N_HEADS, PAGE, HEAD_DIM), lambda i: (i // tiles, i % tiles, 0, 0, 0)),
                 pl.BlockSpec((1, N_HEADS, tm, HEAD_DIM), lambda i: (i // tiles, 0, i % tiles, 0)),
                 pl.BlockSpec((ms, ATTN_WIDTH), lambda i: (0, 0))]
    if with_mean:
        out_shape.append(jax.ShapeDtypeStruct((m // MOBA_BLOCK, 1, ATTN_WIDTH), F32))
        out_specs.append(pl.BlockSpec((tm // MOBA_BLOCK, 1, ATTN_WIDTH), lambda i: (i, 0, 0)))
    if cast_src is not None:
        rows, cols = cast_src.shape
        slab = rows // steps
        in_specs.append(pl.BlockSpec((slab, cols), lambda i: (i, 0)))
        args.append(cast_src)
        out_shape.append(jax.ShapeDtypeStruct((rows, cols), BF16))
        out_specs.append(pl.BlockSpec((slab, cols), lambda i: (i, 0)))
    return pl.pallas_call(
        functools.partial(_kv_proj_kernel, tm=tm, with_mean=with_mean, with_cast=cast_src is not None),
        out_shape=out_shape,
        grid=(steps,),
        in_specs=in_specs,
        out_specs=out_specs,
        scratch_shapes=[pltpu.VMEM((d, ATTN_WIDTH), BF16)],
        compiler_params=_params(("arbitrary",), 48),
        name="kv_proj",
    )(*args)


def _q_proj_kernel(h_ref, w_ref, hs_ref, km_ref, q_ref, pen_ref, qs_ref, wbf_ref, sm_ref, *, tm, tiles):
    i = pl.program_id(0)

    @pl.when(i == 0)
    def _():
        wbf_ref[...] = w_ref[...].astype(BF16)
        qs_ref[...] = jnp.dot(hs_ref[...], wbf_ref[...], preferred_element_type=F32)

    acc = jnp.dot(h_ref[...], wbf_ref[...], preferred_element_type=F32)
    km = km_ref[0]
    nb = km.shape[0]
    jrow = lax.broadcasted_iota(jnp.int32, (nb, tm), 0)
    tcol = lax.broadcasted_iota(jnp.int32, (nb, tm), 1)
    n_full = (i % tiles) * (tm // MOBA_BLOCK) + lax.shift_right_logical(tcol, MOBA_BLOCK.bit_length() - 1)
    valid = jrow < n_full
    own = jrow == n_full
    pens = []
    for hh in range(N_HEADS):
        cs = slice(hh * HEAD_DIM, (hh + 1) * HEAD_DIM)
        qh = acc[:, cs]
        q_ref[0, hh] = (qh * (HEAD_DIM ** -0.5 * LOG2E)).astype(BF16)
        s = lax.dot_general(km[:, cs], qh, _NT, precision=lax.Precision.HIGHEST, preferred_element_type=F32)
        sm = jnp.where(valid, s, NEG)
        sm_ref[...] = sm
        rank = jnp.zeros((nb, tm), jnp.int32)
        for jp in range(nb):
            row = sm_ref[jp:jp + 1, :]
            gt = jnp.where(row > sm, 1, 0)
            ge = jnp.where(row >= sm, 1, 0)
            rank = rank + jnp.where(jrow > jp, ge, gt)
        keep = (valid & (rank < MOBA_TOPK)) | own
        pens.append(jnp.where(keep, 0.0, NEG))
    pen = jnp.concatenate(pens, axis=0)
    pen_ref[0] = pen.T.astype(BF16)


def _q_proj(h, w_in, hs, kmean, batch, seq):
    tm = PROJ_TILE
    m, d = h.shape
    ms = hs.shape[0]
    tiles = seq // tm
    nb = kmean.shape[1]
    return pl.pallas_call(
        functools.partial(_q_proj_kernel, tm=tm, tiles=tiles),
        out_shape=[jax.ShapeDtypeStruct((batch, N_HEADS, seq, HEAD_DIM), BF16),
                   jax.ShapeDtypeStruct((batch, seq, N_HEADS * nb), BF16),
                   jax.ShapeDtypeStruct((ms, ATTN_WIDTH), F32)],
        grid=(m // tm,),
        in_specs=[pl.BlockSpec((tm, d), lambda i: (i, 0)),
                  _resident((d, ATTN_WIDTH), lambda i: (0, 0)),
                  _resident((ms, d), lambda i: (0, 0)),
                  pl.BlockSpec((1, nb, ATTN_WIDTH), lambda i: (i // tiles, 0, 0))],
        out_specs=[pl.BlockSpec((1, N_HEADS, tm, HEAD_DIM), lambda i: (i // tiles, 0, i % tiles, 0)),
                   pl.BlockSpec((1, tm, N_HEADS * nb), lambda i: (i // tiles, i % tiles, 0)),
                   pl.BlockSpec((ms, ATTN_WIDTH), lambda i: (0, 0))],
        scratch_shapes=[pltpu.VMEM((d, ATTN_WIDTH), BF16), pltpu.VMEM((nb, tm), F32)],
        compiler_params=_params(("arbitrary",), 48),
        name="q_proj",
    )(h, w_in, hs, kmean)


def _fold_lanes(x, op):
    out = x[:, 0:128]
    for c in range(1, x.shape[1] // 128):
        out = op(out, x[:, c * 128:(c + 1) * 128])
    return out


def _block_key_means(page_refs, o_ref):
    for blk in range(len(page_refs) // PAGES_PER_BLOCK):
        tot = jnp.zeros((N_HEADS, HEAD_DIM), F32)
        for pp in range(PAGES_PER_BLOCK):
            tot = tot + jnp.sum(page_refs[blk * PAGES_PER_BLOCK + pp][0], axis=1)
        o_ref[0, blk] = tot * (1.0 / MOBA_BLOCK)


def _moba_prompt_kernel(slopes_ref, pt_ref, q_ref, pen_ref, k_ref, v_ref, *refs, nb, n_side):
    page_refs = refs[:n_side]
    o_ref, km_ref, ke_ref, ve_ref, s_ref = refs[n_side:]
    _block_key_means(page_refs, km_ref)

    tq, tk = ATTN_TILE, ATTN_KEYS
    h = pl.program_id(1)
    g = pl.program_id(2)
    slope = slopes_ref[h] * LOG2E
    seq = k_ref.shape[2]

    @pl.when(g == 0)
    def _():
        blk = lax.shift_right_logical(lax.broadcasted_iota(jnp.int32, (seq, HEAD_DIM), 0),
                                      MOBA_BLOCK.bit_length() - 1)
        lane = lax.broadcasted_iota(jnp.int32, (seq, HEAD_DIM), 1)
        ke_ref[:, 0:HEAD_DIM] = k_ref[0, 0]
        ke_ref[:, HEAD_DIM:2 * HEAD_DIM] = jnp.where(lane == h * nb + blk, 1.0, 0.0).astype(BF16)
        ve_ref[:, 0:HEAD_DIM] = v_ref[0, 0]
        ve_ref[:, HEAD_DIM:2 * HEAD_DIM] = jnp.where(lane == 0, 1.0, 0.0).astype(BF16)

    qe = jnp.concatenate([q_ref[0, 0], pen_ref[0]], axis=1)
    key_lane = lax.broadcasted_iota(jnp.int32, (1, tk), 1)

    def scores(c):
        ke = ke_ref[pl.ds(pl.multiple_of(c * tk, tk), tk), :]
        s = lax.dot_general(qe, ke, _NT, preferred_element_type=F32)
        return s + slope * (c * tk + key_lane).astype(F32)

    last = lax.shift_right_logical(g * tq, tk.bit_length() - 1)
    qpos = g * tq + lax.broadcasted_iota(jnp.int32, (tq, tk), 0)
    kpos = last * tk + lax.broadcasted_iota(jnp.int32, (tq, tk), 1)
    s = jnp.where(kpos <= qpos, scores(last), NEG)
    s_ref[last] = s
    mx = _fold_lanes(s, jnp.maximum)

    def pass1(c, mx):
        s = scores(c)
        s_ref[c] = s
        return jnp.maximum(mx, _fold_lanes(s, jnp.maximum))

    mx = lax.fori_loop(0, last, pass1, mx)
    m = jnp.max(mx, axis=1, keepdims=True)

    def pass2(c, acc):
        p = jnp.exp2(s_ref[c] - m)
        ve = ve_ref[pl.ds(pl.multiple_of(c * tk, tk), tk), :]
        return acc + jnp.dot(p.astype(BF16), ve, preferred_element_type=F32)

    acc = lax.fori_loop(0, last + 1, pass2, jnp.zeros((tq, 2 * HEAD_DIM), F32))
    o_ref[0] = acc[:, 0:HEAD_DIM] / acc[:, HEAD_DIM:HEAD_DIM + 1]


def _moba_prompt(slopes, page_table_flat, q, pen, k, v, pool_k, side_batch, n_pages):
    batch, _, seq, _ = q.shape
    nb = seq // MOBA_BLOCK
    nt = seq // ATTN_TILE
    steps = batch * N_HEADS * nt
    n_side = side_batch * n_pages // steps
    assert n_side * steps == side_batch * n_pages and n_side % PAGES_PER_BLOCK == 0 and n_pages % n_side == 0
    steps_per_seq = n_pages // n_side

    def step(b, h, g):
        return (b * N_HEADS + h) * nt + g

    def page_spec(p):
        return pl.BlockSpec((1, N_HEADS, PAGE, HEAD_DIM),
                            lambda b, h, g, sl, pt: (pt[step(b, h, g) * n_side + p], 0, 0, 0))

    grid_spec = pltpu.PrefetchScalarGridSpec(
        num_scalar_prefetch=2,
        grid=(batch, N_HEADS, nt),
        in_specs=[pl.BlockSpec((1, 1, ATTN_TILE, HEAD_DIM), lambda b, h, g, sl, pt: (b, h, g, 0)),
                  pl.BlockSpec((1, ATTN_TILE, N_HEADS * nb), lambda b, h, g, sl, pt: (b, g, 0)),
                  pl.BlockSpec((1, 1, seq, HEAD_DIM), lambda b, h, g, sl, pt: (b, h, 0, 0)),
                  pl.BlockSpec((1, 1, seq, HEAD_DIM), lambda b, h, g, sl, pt: (b, h, 0, 0))]
                 + [page_spec(p) for p in range(n_side)],
        out_specs=[pl.BlockSpec((1, ATTN_TILE, HEAD_DIM), lambda b, h, g, sl, pt: (b, g, h)),
                   pl.BlockSpec((1, n_side // PAGES_PER_BLOCK, N_HEADS, HEAD_DIM),
                                lambda b, h, g, sl, pt: (step(b, h, g) // steps_per_seq,
                                                         step(b, h, g) % steps_per_seq, 0, 0))],
        scratch_shapes=[pltpu.VMEM((seq, 2 * HEAD_DIM), BF16),
                        pltpu.VMEM((seq, 2 * HEAD_DIM), BF16),
                        pltpu.VMEM((seq // ATTN_KEYS, ATTN_TILE, ATTN_KEYS), F32)],
    )
    return pl.pallas_call(
        functools.partial(_moba_prompt_kernel, nb=nb, n_side=n_side),
        out_shape=[jax.ShapeDtypeStruct((batch, seq, ATTN_WIDTH), F32),
                   jax.ShapeDtypeStruct((side_batch, n_pages // PAGES_PER_BLOCK, N_HEADS, HEAD_DIM), F32)],
        grid_spec=grid_spec,
        compiler_params=_params(("arbitrary", "arbitrary", "arbitrary"), 56),
        name="moba_prompt",
    )(slopes, page_table_flat, q, pen, k, v, *([pool_k] * n_side))


def _glu_proj_kernel(h_ref, wa_ref, wg_ref, hs_ref, u_ref, us_ref, wbf_ref):
    tn = wa_ref.shape[1]

    def glu(rows):
        ag = jnp.dot(rows, wbf_ref[...], preferred_element_type=F32)
        return ag[:, 0:tn] * jax.nn.sigmoid(ag[:, tn:2 * tn])

    @pl.when(pl.program_id(1) == 0)
    def _():
        wbf_ref[:, 0:tn] = wa_ref[...].astype(BF16)
        wbf_ref[:, tn:2 * tn] = wg_ref[...].astype(BF16)
        us_ref[...] = glu(hs_ref[...])

    u_ref[...] = glu(h_ref[...])


def _glu_proj(h, w_in, hs, a_col0, g_col0, width):
    tm, tn = 512, 512
    m, d = h.shape
    ms = hs.shape[0]
    return pl.pallas_call(
        _glu_proj_kernel,
        out_shape=[jax.ShapeDtypeStruct((m, width), F32), jax.ShapeDtypeStruct((ms, width), F32)],
        grid=(width // tn, m // tm),
        in_specs=[pl.BlockSpec((tm, d), lambda n, i: (i, 0)),
                  pl.BlockSpec((d, tn), lambda n, i: (0, a_col0 // tn + n)),
                  pl.BlockSpec((d, tn), lambda n, i: (0, g_col0 // tn + n)),
                  _resident((ms, d), lambda n, i: (0, 0))],
        out_specs=[pl.BlockSpec((tm, tn), lambda n, i: (i, n)),
                   pl.BlockSpec((ms, tn), lambda n, i: (0, n))],
        scratch_shapes=[pltpu.VMEM((d, 2 * tn), BF16)],
        compiler_params=_params(("arbitrary", "arbitrary"), 48),
        name="glu_proj",
    )(h, w_in, w_in, hs)


def _conv_tail(y, gln, bln, gco):
    mu = jnp.mean(y, axis=-1, keepdims=True)
    var = jnp.mean(jnp.square(y - mu), axis=-1, keepdims=True)
    z = _silu((y - mu) * lax.rsqrt(var + EPS) * gln + bln)
    return _rms(z, gco)


def _conv_prompt_kernel(u_ref, halo_ref, w_ref, bdw_ref, gln_ref, bln_ref, gco_ref, cn_ref, buf_ref,
                        ext_ref, z_ref, y_ref, *, tt):
    t = pl.program_id(1)
    ext_ref[0:CONV_HALO, :] = jnp.where(t == 0, 0.0, halo_ref[...])
    ext_ref[CONV_HALO:CONV_HALO + tt, :] = u_ref[...]
    width = u_ref.shape[1]
    ext_ref[CONV_HALO + tt:CONV_HALO + tt + SUBLANES, :] = jnp.zeros((SUBLANES, width), F32)
    lead = CONV_HALO - CONV_BUF
    zr = tt + SUBLANES
    for cb in range(width // 128):
        cs = slice(cb * 128, (cb + 1) * 128)
        y = None
        for b in range(SUBLANES):
            taps = [k for k in range(CONV_K) if (lead + k) % SUBLANES == b]
            z = None
            for k in taps:
                r0 = lead + k - b
                term = w_ref[k:k + 1, cs] * ext_ref[r0:r0 + zr, cs]
                z = term if z is None else z + term
            if b == 0:
                y = z[0:tt, :]
            else:
                z_ref[...] = z
                y = y + z_ref[b:b + tt, :]
        y_ref[:, cs] = y + bdw_ref[:, cs]
    cn_ref[...] = _conv_tail(y_ref[...], gln_ref[...], bln_ref[...], gco_ref[...]).astype(BF16)

    @pl.when(t == pl.num_programs(1) - 1)
    def _():
        buf_ref[0] = ext_ref[CONV_HALO + tt - CONV_BUF:CONV_HALO + tt, :]


def _conv_prompt(u, w_dw, b_dw, g_ln, b_ln, g_co, batch, seq):
    tt = 256
    m, width = u.shape
    tiles = seq // tt
    halo_per_tile = tt // CONV_HALO
    vec = pl.BlockSpec((1, width), lambda b, t: (0, 0))
    return pl.pallas_call(
        functools.partial(_conv_prompt_kernel, tt=tt),
        out_shape=[jax.ShapeDtypeStruct((m, width), BF16),
                   jax.ShapeDtypeStruct((batch, CONV_BUF, width), F32)],
        grid=(batch, tiles),
        in_specs=[pl.BlockSpec((tt, width), lambda b, t: (b * tiles + t, 0)),
                  pl.BlockSpec((CONV_HALO, width),
                               lambda b, t: (jnp.maximum((b * tiles + t) * halo_per_tile - 1, 0), 0)),
                  pl.BlockSpec((CONV_K, width), lambda b, t: (0, 0)), vec, vec, vec, vec],
        out_specs=[pl.BlockSpec((tt, width), lambda b, t: (b * tiles + t, 0)),
                   pl.BlockSpec((1, CONV_BUF, width), lambda b, t: (b, 0, 0))],
        scratch_shapes=[pltpu.VMEM((CONV_HALO + tt + SUBLANES, width), F32),
                        pltpu.VMEM((tt + SUBLANES, 128), F32),
                        pltpu.VMEM((tt, width), F32)],
        compiler_params=_params(("parallel", "arbitrary"), 32),
        name="conv_prompt",
    )(u, u, w_dw, b_dw, g_ln, b_ln, g_co)


def _conv_sample_kernel(st_ref, u_ref, w_ref, bdw_ref, gln_ref, bln_ref, gco_ref, cn_ref, buf_ref,
                        ext_ref, *, rows):
    ext_ref[0:CONV_BUF, :] = st_ref[0]
    ext_ref[CONV_BUF:CONV_BUF + rows, :] = u_ref[0]
    acc = jnp.zeros((rows, u_ref.shape[2]), F32)
    for k in range(CONV_K):
        acc = acc + w_ref[k:k + 1, :] * ext_ref[k:k + rows, :]
    y = acc + bdw_ref[...]
    cn_ref[0] = _conv_tail(y, gln_ref[...], bln_ref[...], gco_ref[...]).astype(BF16)
    buf_ref[0] = ext_ref[rows:rows + CONV_BUF, :]


def _conv_sample(state, u, w_dw, b_dw, g_ln, b_ln, g_co):
    batch, rows, width = u.shape
    vec = pl.BlockSpec((1, width), lambda b: (0, 0))
    tok = pl.BlockSpec((1, rows, width), lambda b: (b, 0, 0))
    buf = pl.BlockSpec((1, CONV_BUF, width), lambda b: (b, 0, 0))
    return pl.pallas_call(
        functools.partial(_conv_sample_kernel, rows=rows),
        out_shape=[jax.ShapeDtypeStruct((batch, rows, width), BF16),
                   jax.ShapeDtypeStruct((batch, CONV_BUF, width), F32)],
        grid=(batch,),
        in_specs=[buf, tok, pl.BlockSpec((CONV_K, width), lambda b: (0, 0)), vec, vec, vec, vec],
        out_specs=[tok, buf],
        scratch_shapes=[pltpu.VMEM((CONV_BUF + rows + 6, width), F32)],
        compiler_params=_params(("parallel",), 32),
        name="conv_sample",
    )(state, u, w_dw, b_dw, g_ln, b_ln, g_co)


def _out_proj_kernel(o_ref, cn_ref, w_ref, x_ref, gao_ref, gpm_ref, gpf_ref, gt_ref, sc_ref, sh_ref,
                     x1_ref, h2_ref):
    an = _rms(o_ref[...], gao_ref[...]).astype(BF16)
    mix = jnp.concatenate([an, cn_ref[...]], axis=1)
    merged = jnp.dot(mix, w_ref[...], preferred_element_type=F32)
    x1 = x_ref[...] + _mod2d(gt_ref) * _rms(merged, gpm_ref[...])
    x1_ref[...] = x1
    h2_ref[...] = (_rms(x1, gpf_ref[...]) * (1.0 + _mod2d(sc_ref)) + _mod2d(sh_ref)).astype(BF16)


def _out_proj(o_attn, conv_n, w_bf, x, g_ao, g_pm, g_pf, mod, tm, rows_per_group):
    m, d = x.shape
    aw = o_attn.shape[1]
    cw = conv_n.shape[1]
    gt, gt_spec = _mod_spec(mod, 2, tm, rows_per_group)
    sc, sc_spec = _mod_spec(mod, 4, tm, rows_per_group)
    sh, sh_spec = _mod_spec(mod, 3, tm, rows_per_group)
    row = lambda w: pl.BlockSpec((tm, w), lambda i: (i, 0))
    vec = lambda w: pl.BlockSpec((1, w), lambda i: (0, 0))
    return pl.pallas_call(
        _out_proj_kernel,
        out_shape=[jax.ShapeDtypeStruct((m, d), F32), jax.ShapeDtypeStruct((m, d), BF16)],
        grid=(m // tm,),
        in_specs=[row(aw), row(cw), _resident((aw + cw, d), lambda i: (0, 0)), row(d),
                  vec(aw), vec(d), vec(d), gt_spec, sc_spec, sh_spec],
        out_specs=[row(d), row(d)],
        compiler_params=_params(("parallel",), 56),
        name="out_proj",
    )(o_attn, conv_n, w_bf, x, g_ao, g_pm, g_pf, gt, sc, sh)


def _ffn_up_kernel(h_ref, wg_ref, wu_ref, hs_ref, wd_ref, a_ref, as_ref, wdbf_ref, wbf_ref):
    tn = wg_ref.shape[1]

    def swiglu(rows):
        gu = jnp.dot(rows, wbf_ref[...], preferred_element_type=F32)
        return (_silu(gu[:, 0:tn]) * gu[:, tn:2 * tn]).astype(BF16)

    @pl.when(pl.program_id(1) == 0)
    def _():
        wbf_ref[:, 0:tn] = wg_ref[...].astype(BF16)
        wbf_ref[:, tn:2 * tn] = wu_ref[...].astype(BF16)
        as_ref[...] = swiglu(hs_ref[...])

    a_ref[...] = swiglu(h_ref[...])
    wdbf_ref[...] = wd_ref[...].astype(BF16)


def _ffn_up(h, w_gate_up, hs, w_down):
    tm, tn = 1024, 512
    m, d = h.shape
    ms = hs.shape[0]
    d_ff = w_gate_up.shape[1] // 2
    n_blocks, m_tiles = d_ff // tn, m // tm
    slab = w_down.shape[0] // (n_blocks * m_tiles)
    return pl.pallas_call(
        _ffn_up_kernel,
        out_shape=[jax.ShapeDtypeStruct((m, d_ff), BF16), jax.ShapeDtypeStruct((ms, d_ff), BF16),
                   jax.ShapeDtypeStruct(w_down.shape, BF16)],
        grid=(n_blocks, m_tiles),
        in_specs=[pl.BlockSpec((tm, d), lambda n, i: (i, 0)),
                  pl.BlockSpec((d, tn), lambda n, i: (0, n)),
                  pl.BlockSpec((d, tn), lambda n, i: (0, n_blocks + n)),
                  _resident((ms, d), lambda n, i: (0, 0)),
                  pl.BlockSpec((slab, w_down.shape[1]), lambda n, i: (n * m_tiles + i, 0))],
        out_specs=[pl.BlockSpec((tm, tn), lambda n, i: (i, n)),
                   pl.BlockSpec((ms, tn), lambda n, i: (0, n)),
                   pl.BlockSpec((slab, w_down.shape[1]), lambda n, i: (n * m_tiles + i, 0))],
        scratch_shapes=[pltpu.VMEM((d, 2 * tn), BF16)],
        compiler_params=_params(("arbitrary", "arbitrary"), 56),
        name="ffn_up",
    )(h, w_gate_up, w_gate_up, hs, w_down)


def _ffn_down_kernel(a_ref, w_ref, x1_ref, g_ref, gt_ref, y_ref):
    z = jnp.dot(a_ref[...], w_ref[...], preferred_element_type=F32)
    y_ref[...] = x1_ref[...] + _mod2d(gt_ref) * _rms(z, g_ref[...])


def _ffn_down(act, w_bf, x1, g_post, mod, tm, rows_per_group):
    m, d_ff = act.shape
    d = w_bf.shape[1]
    gt, gt_spec = _mod_spec(mod, 5, tm, rows_per_group)
    return pl.pallas_call(
        _ffn_down_kernel,
        out_shape=jax.ShapeDtypeStruct((m, d), F32),
        grid=(m // tm,),
        in_specs=[pl.BlockSpec((tm, d_ff), lambda i: (i, 0)),
                  _resident((d_ff, d), lambda i: (0, 0)),
                  pl.BlockSpec((tm, d), lambda i: (i, 0)),
                  pl.BlockSpec((1, d), lambda i: (0, 0)), gt_spec],
        out_specs=pl.BlockSpec((tm, d), lambda i: (i, 0)),
        compiler_params=_params(("parallel",), 48),
        name="ffn_down",
    )(act, w_bf, x1, g_post, gt)


SELECT_ROWS = 16


def _select_sample_kernel(pt_ref, q_ref, km_ref, idx_ref, *, n_q, n_pages):
    b = pl.program_id(0)
    nb = km_ref.shape[2]
    lane = lax.broadcasted_iota(jnp.int32, (nb, 128), 1)
    rowi = lax.broadcasted_iota(jnp.int32, (nb, 128), 0)
    s = jnp.full((nb, 128), NEG, F32)
    for hh in range(N_HEADS):
        km = km_ref[0, hh]
        for t in range(n_q):
            col = jnp.sum(km * q_ref[0, hh, t:t + 1, :], axis=1, keepdims=True)
            s = jnp.where(lane == hh * n_q + t, col, s)
    out_row = lax.broadcasted_iota(jnp.int32, (SELECT_ROWS, 128), 0)
    out = jnp.zeros((SELECT_ROWS, 128), jnp.int32)
    for r in range(MOBA_TOPK):
        top = jnp.max(s, axis=0, keepdims=True)
        arg = jnp.min(jnp.where(s == top, rowi, nb), axis=0, keepdims=True)
        out = jnp.where(out_row == r, arg, out)
        s = jnp.where(rowi == arg, -jnp.inf, s)
        for pp in range(PAGES_PER_BLOCK):
            page = jnp.zeros((1, 128), jnp.int32)
            for j in range(nb):
                page = jnp.where(arg == j, pt_ref[b * n_pages + j * PAGES_PER_BLOCK + pp], page)
            out = jnp.where(out_row == 8 + r * PAGES_PER_BLOCK + pp, page, out)
    idx_ref[0] = out


def _select_sample(page_table_flat, q, kmean, n_pages):
    batch, _, n_q, _ = q.shape
    nb = kmean.shape[2]
    grid_spec = pltpu.PrefetchScalarGridSpec(
        num_scalar_prefetch=1,
        grid=(batch,),
        in_specs=[pl.BlockSpec((1, N_HEADS, n_q, HEAD_DIM), lambda b, pt: (b, 0, 0, 0)),
                  pl.BlockSpec((1, N_HEADS, nb, HEAD_DIM), lambda b, pt: (b, 0, 0, 0))],
        out_specs=pl.BlockSpec((1, SELECT_ROWS, 128), lambda b, pt: (b, 0, 0)),
    )
    return pl.pallas_call(
        functools.partial(_select_sample_kernel, n_q=n_q, n_pages=n_pages),
        out_shape=jax.ShapeDtypeStruct((batch, SELECT_ROWS, 128), jnp.int32),
        grid_spec=grid_spec,
        compiler_params=_params(("arbitrary",), 32),
        name="select_sample",
    )(page_table_flat, q, kmean)


def _moba_sample_kernel(page_ref, idx_ref, slopes_ref, q_ref, kn_ref, vn_ref, *refs, n_q, past):
    n_sel = n_q * MOBA_TOPK * PAGES_PER_BLOCK
    k_refs, v_refs, o_ref = refs[:n_sel], refs[n_sel:2 * n_sel], refs[2 * n_sel]
    b = pl.program_id(0)
    h = pl.program_id(1)
    slope = slopes_ref[h]
    key = lax.broadcasted_iota(jnp.int32, (PAGE, 1), 0)
    for t in range(n_q):
        q = q_ref[0, 0, t:t + 1, :] * (HEAD_DIM ** -0.5)
        scores = []
        for s in range(MOBA_TOPK):
            blk = idx_ref[((b * N_HEADS + h) * n_q + t) * MOBA_TOPK + s]
            for pp in range(PAGES_PER_BLOCK):
                kp = k_refs[(t * MOBA_TOPK + s) * PAGES_PER_BLOCK + pp][0, 0]
                dist = (past + t - blk * MOBA_BLOCK - pp * PAGE - key).astype(F32)
                scores.append(jnp.sum(kp * q, axis=1, keepdims=True) - slope * dist)
        own = []
        for t2 in range(t + 1):
            own.append(jnp.sum(kn_ref[0, 0, t2:t2 + 1, :] * q, axis=1, keepdims=True) - slope * float(t - t2))
        m = own[0]
        for sc in scores:
            m = jnp.maximum(m, jnp.max(sc, axis=0, keepdims=True))
        for sc in own[1:]:
            m = jnp.maximum(m, sc)
        l = jnp.zeros((1, 1), F32)
        acc = jnp.zeros((1, HEAD_DIM), F32)
        for n, sc in enumerate(scores):
            p = jnp.exp(sc - m)
            l = l + jnp.sum(p, axis=0, keepdims=True)
            acc = acc + jnp.sum(p * v_refs[n + t * MOBA_TOPK * PAGES_PER_BLOCK][0, 0], axis=0, keepdims=True)
        for t2, sc in enumerate(own):
            p = jnp.exp(sc - m)
            l = l + p
            acc = acc + p * vn_ref[0, 0, t2:t2 + 1, :]
        o_ref[0, 0, t:t + 1, :] = acc / l


def _moba_sample(pages_flat, idx_flat, slopes, q, k_new, v_new, pool_k, pool_v, n_pages):
    batch, _, n_q, _ = q.shape
    past = n_pages * PAGE
    per_query = MOBA_TOPK * PAGES_PER_BLOCK

    def sel_spec(n):
        return pl.BlockSpec((1, 1, PAGE, HEAD_DIM),
                            lambda b, h, pg, idx, sl: (pg[(b * N_HEADS + h) * n_q * per_query + n], h, 0, 0))

    sel_specs = [sel_spec(n) for n in range(n_q * per_query)]
    tok = pl.BlockSpec((1, 1, n_q, HEAD_DIM), lambda b, h, pg, idx, sl: (b, h, 0, 0))
    grid_spec = pltpu.PrefetchScalarGridSpec(
        num_scalar_prefetch=3,
        grid=(batch, N_HEADS),
        in_specs=[tok, tok, tok] + sel_specs + sel_specs,
        out_specs=tok,
    )
    n_sel = len(sel_specs)
    return pl.pallas_call(
        functools.partial(_moba_sample_kernel, n_q=n_q, past=past),
        out_shape=jax.ShapeDtypeStruct((batch, N_HEADS, n_q, HEAD_DIM), F32),
        grid_spec=grid_spec,
        compiler_params=_params(("parallel", "parallel"), 32),
        name="moba_sample",
    )(pages_flat, idx_flat, slopes, q, k_new, v_new, *([pool_k] * n_sel), *([pool_v] * n_sel))


def kernel(x_prompt, x_sample, cache_k, cache_v, state_conv, page_table, c_prompt, c_sample, w_ada, b_ada, g_pre_mix, w_in, w_dw, b_dw, g_conv_ln, b_conv_ln, g_attn_out, g_conv_out, w_out, g_post_mix, g_pre_ffn, w_gate_up, w_down, g_post_ffn):
    depth = w_ada.shape[0]
    assert depth == 1, "single layer: the prompt and sample residual streams are not chained across layers here"
    batch, seq, d = x_prompt.shape
    dec_batch, dec_seq, _ = x_sample.shape
    n_pages = page_table.shape[1]
    past = n_pages * PAGE
    conv_w = w_dw.shape[-1]
    assert past % MOBA_BLOCK == 0 and dec_seq <= MOBA_BLOCK and past // MOBA_BLOCK >= MOBA_TOPK
    assert seq % ATTN_TILE == 0 and d == ATTN_WIDTH + conv_w

    slopes = 2.0 ** (-8.0 * (jnp.arange(N_HEADS, dtype=F32) + 1.0) / N_HEADS)
    l = 0
    vec = lambda a: a[l].reshape(1, -1)
    g_pm, g_ao, g_co = vec(g_pre_mix), vec(g_attn_out), vec(g_conv_out)
    g_post, g_pf, g_pffn = vec(g_post_mix), vec(g_pre_ffn), vec(g_post_ffn)
    bdw, gln, bln = vec(b_dw), vec(g_conv_ln), vec(b_conv_ln)
    wdw = w_dw[l].reshape(CONV_K, conv_w)

    n_c = batch + dec_batch
    c_rows = -(-n_c // 8) * 8
    c_all = jnp.concatenate([c_prompt, c_sample, jnp.zeros((c_rows - n_c, d), F32)], axis=0)
    mod = _ada(c_all, w_ada[l], vec(b_ada))
    mod_p = mod.reshape(c_rows * 6, 1, d)
    mod_rows = jnp.repeat(mod[batch:n_c].reshape(dec_batch, 6, d), dec_seq, axis=0)
    mod_s = [mod_rows[:, w] for w in range(6)]

    m_s = dec_batch * dec_seq
    xp = x_prompt.reshape(batch * seq, d)
    xs = x_sample.reshape(m_s, d)
    h = _modnorm(xp, g_pm, mod_p, 1, 0, 512, seq)
    hs = _modnorm(xs, g_pm, mod_s, 1, 0, m_s, None)

    k_pages, k_heads, k_rows_s, kmean, w_out_bf = _kv_proj(h, w_in[l], 1, hs, batch, seq, True, w_out[l])
    v_pages, v_heads, v_rows_s = _kv_proj(h, w_in[l], 2, hs, batch, seq, False)
    q_heads, pen, q_rows_s = _q_proj(h, w_in[l], hs, kmean.reshape(batch, seq // MOBA_BLOCK, ATTN_WIDTH),
                                     batch, seq)
    u, u_s = _glu_proj(h, w_in[l], hs, 3 * ATTN_WIDTH, 3 * ATTN_WIDTH + conv_w, conv_w)

    pt_flat = page_table.reshape(-1)
    o_attn, kmean_s = _moba_prompt(slopes, pt_flat, q_heads, pen, k_heads, v_heads, cache_k[l], dec_batch, n_pages)
    o_attn = o_attn.reshape(batch * seq, ATTN_WIDTH)
    conv_n, conv_buf_p = _conv_prompt(u, wdw, bdw, gln, bln, g_co, batch, seq)
    x1, h2 = _out_proj(o_attn, conv_n, w_out_bf, xp, g_ao, g_post, g_pf, mod_p, 512, seq)

    to_heads = lambda t: t.reshape(dec_batch, dec_seq, N_HEADS, HEAD_DIM).transpose(0, 2, 1, 3)
    q_s, k_s, v_s = to_heads(q_rows_s), to_heads(k_rows_s), to_heads(v_rows_s)
    kmean_s = kmean_s.transpose(0, 2, 1, 3)
    sel = _select_sample(pt_flat, q_s, kmean_s, n_pages)[:, :, :N_HEADS * dec_seq]
    per_query = lambda rows: rows.reshape(dec_batch, -1, N_HEADS, dec_seq).transpose(0, 2, 3, 1).reshape(-1)
    idx_flat = per_query(sel[:, :MOBA_TOPK])
    pages_flat = per_query(sel[:, 8:8 + MOBA_TOPK * PAGES_PER_BLOCK])
    o_s = _moba_sample(pages_flat, idx_flat, slopes, q_s, k_s, v_s, cache_k[l], cache_v[l], n_pages)
    o_attn_s = o_s.transpose(0, 2, 1, 3).reshape(m_s, ATTN_WIDTH)
    conv_n_s, conv_buf_s = _conv_sample(state_conv[l], u_s.reshape(dec_batch, dec_seq, conv_w),
                                        wdw, bdw, gln, bln, g_co)
    x1_s, h2_s = _out_proj(o_attn_s, conv_n_s.reshape(m_s, conv_w), w_out_bf, xs, g_ao, g_post, g_pf,
                           mod_s, m_s, None)

    act, act_s, w_down_bf = _ffn_up(h2, w_gate_up[l], h2_s, w_down[l])
    y_p = _ffn_down(act, w_down_bf, x1, g_pffn, mod_p, 256, seq).reshape(batch, seq, d)
    y_s = _ffn_down(act_s, w_down_bf, x1_s, g_pffn, mod_s, m_s, None).reshape(dec_batch, dec_seq, d)

    return (y_p, y_s, k_pages[None], v_pages[None], conv_buf_p[None],
            k_s[None], v_s[None], conv_buf_s[None])
```

```python
import functools

import jax
import jax.numpy as jnp
from jax import lax
from jax.experimental import pallas as pl
from jax.experimental.pallas import tpu as pltpu

F32 = jnp.float32
BF16 = jnp.bfloat16

N_HEADS = 8
HEAD_DIM = 128
ATTN_WIDTH = N_HEADS * HEAD_DIM
CONV_K = 31
CONV_BUF = CONV_K - 1
CONV_HALO = 32
CONV_CHUNK = 64
SUBLANES = 8
PAGE = 128
MOBA_BLOCK = 256
MOBA_TOPK = 3
PAGES_PER_BLOCK = MOBA_BLOCK // PAGE
ATTN_TILE = 4 * MOBA_BLOCK
PROJ_TILE = 512
ATTN_KEYS = 4 * MOBA_BLOCK
LOG2E = 1.4426950408889634
EPS = 1e-6
NEG = -1e30
MIB = 1024 * 1024

_NT = (((1,), (1,)), ((), ()))


def _params(sem, vmem_mib, flags=None):
    return pltpu.CompilerParams(dimension_semantics=sem, vmem_limit_bytes=vmem_mib * MIB, flags=flags)


def _resident(shape, index_map):
    return pl.BlockSpec(shape, index_map, pipeline_mode=pl.Buffered(1))


def _rms(x, g):
    return x * lax.rsqrt(jnp.mean(x * x, axis=-1, keepdims=True) + EPS) * g


def _silu(x):
    return x * jax.nn.sigmoid(x)


def _mod2d(ref):
    v = ref[...]
    return v.reshape(v.shape[-2], v.shape[-1])


def _mod_spec(mod, which, tm, rows_per_group):
    if rows_per_group is None:
        return mod[which], pl.BlockSpec((tm, mod[which].shape[1]), lambda i, *_: (i, 0))
    tiles = rows_per_group // tm
    return mod, pl.BlockSpec((1, 1, mod.shape[2]), lambda i, *_: ((i // tiles) * 6 + which, 0, 0))


def _ada_kernel(c_ref, w_ref, b_ref, o_ref):
    s = _silu(c_ref[...]).astype(BF16)
    o_ref[...] = jnp.dot(s, w_ref[...].astype(BF16), preferred_element_type=F32) + b_ref[...]


def _ada(c_all, w, b):
    rows, d = c_all.shape
    n = w.shape[1]
    tn = 1024
    return pl.pallas_call(
        _ada_kernel,
        out_shape=jax.ShapeDtypeStruct((rows, n), F32),
        grid=(n // tn,),
        in_specs=[pl.BlockSpec((rows, d), lambda j: (0, 0)),
                  pl.BlockSpec((d, tn), lambda j: (0, j)),
                  pl.BlockSpec((1, tn), lambda j: (0, j))],
        out_specs=pl.BlockSpec((rows, tn), lambda j: (0, j)),
        compiler_params=_params(("arbitrary",), 40),
        name="ada_mod",
    )(c_all, w, b)


def _modnorm_kernel(x_ref, g_ref, sc_ref, sh_ref, o_ref):
    h = _rms(x_ref[...], g_ref[...]) * (1.0 + _mod2d(sc_ref)) + _mod2d(sh_ref)
    o_ref[...] = h.astype(BF16)


def _modnorm(x, g, mod, which_sc, which_sh, tm, rows_per_group):
    m, d = x.shape
    sc, sc_spec = _mod_spec(mod, which_sc, tm, rows_per_group)
    sh, sh_spec = _mod_spec(mod, which_sh, tm, rows_per_group)
    return pl.pallas_call(
        _modnorm_kernel,
        out_shape=jax.ShapeDtypeStruct((m, d), BF16),
        grid=(m // tm,),
        in_specs=[pl.BlockSpec((tm, d), lambda i: (i, 0)),
                  pl.BlockSpec((1, d), lambda i: (0, 0)), sc_spec, sh_spec],
        out_specs=pl.BlockSpec((tm, d), lambda i: (i, 0)),
        compiler_params=_params(("parallel",), 32),
        name="modnorm",
    )(x, g, sc, sh)


def _kv_proj_kernel(*refs, tm, with_mean, with_cast):
    it = iter(refs)
    h_ref, w_ref, hs_ref = next(it), next(it), next(it)
    cast_ref = next(it) if with_cast else None
    conv_in = [next(it) for _ in range(7)]
    pages_ref, heads_ref, s_ref = next(it), next(it), next(it)
    mean_ref = next(it) if with_mean else None
    cast_out_ref = next(it) if with_cast else None
    conv_out = [next(it), next(it)]
    wbf_ref = next(it)
    conv_scratch = [next(it), next(it), next(it)]
    i = pl.program_id(0)

    @pl.when(i == 0)
    def _():
        wbf_ref[...] = w_ref[...].astype(BF16)
        s_ref[...] = jnp.dot(hs_ref[...], wbf_ref[...], preferred_element_type=F32)

    for pair in range(N_HEADS // 2):
        ps = slice(pair * 2 * HEAD_DIM, (pair + 1) * 2 * HEAD_DIM)
        acc = jnp.dot(h_ref[...], wbf_ref[:, ps], preferred_element_type=F32)
        for sub in range(2):
            hh = pair * 2 + sub
            cs = slice(sub * HEAD_DIM, (sub + 1) * HEAD_DIM)
            heads_ref[0, hh] = acc[:, cs].astype(BF16)
            for p in range(tm // PAGE):
                pages_ref[0, p, hh] = acc[p * PAGE:(p + 1) * PAGE, cs]
        if with_mean:
            for mb in range(tm // MOBA_BLOCK):
                mean_ref[mb, :, ps] = jnp.mean(acc[mb * MOBA_BLOCK:(mb + 1) * MOBA_BLOCK, :], axis=0, keepdims=True)
    if with_cast:
        cast_out_ref[...] = cast_ref[...].astype(BF16)
    _conv_tile(i == 0, i == pl.num_programs(0) - 1, *conv_in, *conv_out, *conv_scratch)


def _kv_proj(h, w_in, col_block, hs, batch, seq, with_mean, conv_seq, u, conv_params, cast_src=None):
    tm = PROJ_TILE
    m, d = h.shape
    ms = hs.shape[0]
    tiles = seq // tm
    steps = m // tm
    width = u.shape[1]
    tt = seq // steps
    assert tt * steps == seq and tt % CONV_HALO == 0 and tt >= CONV_BUF
    halo_per_tile = tt // CONV_HALO
    in_specs = [pl.BlockSpec((tm, d), lambda i: (i, 0)),
                _resident((d, ATTN_WIDTH), lambda i: (0, col_block)),
                _resident((ms, d), lambda i: (0, 0))]
    args = [h, w_in, hs]
    out_shape = [jax.ShapeDtypeStruct((batch, seq // PAGE, N_HEADS, PAGE, HEAD_DIM), F32),
                 jax.ShapeDtypeStruct((batch, N_HEADS, seq, HEAD_DIM), BF16),
                 jax.ShapeDtypeStruct((ms, ATTN_WIDTH), F32)]
    out_specs = [pl.BlockSpec((1, tm // PAGE, N_HEADS, PAGE, HEAD_DIM), lambda i: (i // tiles, i % tiles, 0, 0, 0)),
                 pl.BlockSpec((1, N_HEADS, tm, HEAD_DIM), lambda i: (i // tiles, 0, i % tiles, 0)),
                 pl.BlockSpec((ms, ATTN_WIDTH), lambda i: (0, 0))]
    if with_mean:
        out_shape.append(jax.ShapeDtypeStruct((m // MOBA_BLOCK, 1, ATTN_WIDTH), F32))
        out_specs.append(pl.BlockSpec((tm // MOBA_BLOCK, 1, ATTN_WIDTH), lambda i: (i, 0, 0)))
    if cast_src is not None:
        rows, cols = cast_src.shape
        slab = rows // steps
        in_specs.append(pl.BlockSpec((slab, cols), lambda i: (i, 0)))
        args.append(cast_src)
        out_shape.append(jax.ShapeDtypeStruct((rows, cols), BF16))
        out_specs.append(pl.BlockSpec((slab, cols), lambda i: (i, 0)))
    vec = pl.BlockSpec((1, width), lambda i: (0, 0))
    in_specs += [pl.BlockSpec((tt, width), lambda i: (conv_seq * steps + i, 0)),
                 pl.BlockSpec((CONV_HALO, width),
                              lambda i: (jnp.maximum((conv_seq * steps + i) * halo_per_tile - 1, 0), 0)),
                 pl.BlockSpec((CONV_K, width), lambda i: (0, 0)), vec, vec, vec, vec]
    args += [u, u, *conv_params]
    out_shape += [jax.ShapeDtypeStruct((seq, width), BF16), jax.ShapeDtypeStruct((1, CONV_BUF, width), F32)]
    out_specs += [pl.BlockSpec((tt, width), lambda i: (i, 0)),
                  pl.BlockSpec((1, CONV_BUF, width), lambda i: (0, 0, 0))]
    return pl.pallas_call(
        functools.partial(_kv_proj_kernel, tm=tm, with_mean=with_mean, with_cast=cast_src is not None),
        out_shape=out_shape,
        grid=(steps,),
        in_specs=in_specs,
        out_specs=out_specs,
        scratch_shapes=[pltpu.VMEM((d, ATTN_WIDTH), BF16),
                        pltpu.VMEM((CONV_HALO + tt + SUBLANES, width), F32),
                        pltpu.VMEM((CONV_CHUNK + SUBLANES, 128), F32),
                        pltpu.VMEM((tt, width), F32)],
        compiler_params=_params(("arbitrary",), 56),
        name="kv_proj",
    )(*args)


def _q_proj_kernel(h_ref, w_ref, hs_ref, km_ref, q_ref, pen_ref, qs_ref, wbf_ref, sm_ref, *, tm, tiles):
    i = pl.program_id(0)

    @pl.when(i == 0)
    def _():
        wbf_ref[...] = w_ref[...].astype(BF16)
        qs_ref[...] = jnp.dot(hs_ref[...], wbf_ref[...], preferred_element_type=F32)

    acc = jnp.dot(h_ref[...], wbf_ref[...], preferred_element_type=F32)
    km = km_ref[0]
    nb = km.shape[0]
    jrow = lax.broadcasted_iota(jnp.int32, (nb, tm), 0)
    tcol = lax.broadcasted_iota(jnp.int32, (nb, tm), 1)
    n_full = (i % tiles) * (tm // MOBA_BLOCK) + lax.shift_right_logical(tcol, MOBA_BLOCK.bit_length() - 1)
    valid = jrow < n_full
    own = jrow == n_full
    pens = []
    for hh in range(N_HEADS):
        cs = slice(hh * HEAD_DIM, (hh + 1) * HEAD_DIM)
        qh = acc[:, cs]
        q_ref[0, hh] = (qh * (HEAD_DIM ** -0.5 * LOG2E)).astype(BF16)
        s = lax.dot_general(km[:, cs], qh, _NT, precision=lax.Precision.HIGHEST, preferred_element_type=F32)
        sm = jnp.where(valid, s, NEG)
        sm_ref[...] = sm
        rank = jnp.zeros((nb, tm), jnp.int32)
        for jp in range(nb):
            row = sm_ref[jp:jp + 1, :]
            gt = jnp.where(row > sm, 1, 0)
            ge = jnp.where(row >= sm, 1, 0)
            rank = rank + jnp.where(jrow > jp, ge, gt)
        keep = (valid & (rank < MOBA_TOPK)) | own
        pens.append(jnp.where(keep, 0.0, NEG))
    pen = jnp.concatenate(pens, axis=0)
    pen_ref[0] = pen.T.astype(BF16)


def _q_proj(h, w_in, hs, kmean, batch, seq):
    tm = PROJ_TILE
    m, d = h.shape
    ms = hs.shape[0]
    tiles = seq // tm
    nb = kmean.shape[1]
    return pl.pallas_call(
        functools.partial(_q_proj_kernel, tm=tm, tiles=tiles),
        out_shape=[jax.ShapeDtypeStruct((batch, N_HEADS, seq, HEAD_DIM), BF16),
                   jax.ShapeDtypeStruct((batch, seq, N_HEADS * nb), BF16),
                   jax.ShapeDtypeStruct((ms, ATTN_WIDTH), F32)],
        grid=(m // tm,),
        in_specs=[pl.BlockSpec((tm, d), lambda i: (i, 0)),
                  _resident((d, ATTN_WIDTH), lambda i: (0, 0)),
                  _resident((ms, d), lambda i: (0, 0)),
                  pl.BlockSpec((1, nb, ATTN_WIDTH), lambda i: (i // tiles, 0, 0))],
        out_specs=[pl.BlockSpec((1, N_HEADS, tm, HEAD_DIM), lambda i: (i // tiles, 0, i % tiles, 0)),
                   pl.BlockSpec((1, tm, N_HEADS * nb), lambda i: (i // tiles, i % tiles, 0)),
                   pl.BlockSpec((ms, ATTN_WIDTH), lambda i: (0, 0))],
        scratch_shapes=[pltpu.VMEM((d, ATTN_WIDTH), BF16), pltpu.VMEM((nb, tm), F32)],
        compiler_params=_params(("arbitrary",), 48),
        name="q_proj",
    )(h, w_in, hs, kmean)


def _fold_lanes(x, op):
    out = x[:, 0:128]
    for c in range(1, x.shape[1] // 128):
        out = op(out, x[:, c * 128:(c + 1) * 128])
    return out


def _block_key_means(page_refs, o_ref):
    for blk in range(len(page_refs) // PAGES_PER_BLOCK):
        tot = jnp.zeros((N_HEADS, HEAD_DIM), F32)
        for pp in range(PAGES_PER_BLOCK):
            tot = tot + jnp.sum(page_refs[blk * PAGES_PER_BLOCK + pp][0], axis=1)
        o_ref[0, blk] = tot * (1.0 / MOBA_BLOCK)


def _moba_prompt_kernel(slopes_ref, pt_ref, q_ref, pen_ref, k_ref, v_ref, *refs, nb, n_side):
    page_refs = refs[:n_side]
    o_ref, km_ref, ke_ref, ve_ref, s_ref = refs[n_side:]
    _block_key_means(page_refs, km_ref)

    tq, tk = ATTN_TILE, ATTN_KEYS
    h = pl.program_id(1)
    g = pl.program_id(2)
    slope = slopes_ref[h] * LOG2E
    seq = k_ref.shape[2]

    @pl.when(g == 0)
    def _():
        blk = lax.shift_right_logical(lax.broadcasted_iota(jnp.int32, (seq, HEAD_DIM), 0),
                                      MOBA_BLOCK.bit_length() - 1)
        lane = lax.broadcasted_iota(jnp.int32, (seq, HEAD_DIM), 1)
        ke_ref[:, 0:HEAD_DIM] = k_ref[0, 0]
        ke_ref[:, HEAD_DIM:2 * HEAD_DIM] = jnp.where(lane == h * nb + blk, 1.0, 0.0).astype(BF16)
        ve_ref[:, 0:HEAD_DIM] = v_ref[0, 0]
        ve_ref[:, HEAD_DIM:2 * HEAD_DIM] = jnp.where(lane == 0, 1.0, 0.0).astype(BF16)

    qe = jnp.concatenate([q_ref[0, 0], pen_ref[0]], axis=1)
    key_lane = lax.broadcasted_iota(jnp.int32, (1, tk), 1)

    def scores(c):
        ke = ke_ref[pl.ds(pl.multiple_of(c * tk, tk), tk), :]
        s = lax.dot_general(qe, ke, _NT, preferred_element_type=F32)
        return s + slope * (c * tk + key_lane).astype(F32)

    last = lax.shift_right_logical(g * tq, tk.bit_length() - 1)
    qpos = g * tq + lax.broadcasted_iota(jnp.int32, (tq, tk), 0)
    kpos = last * tk + lax.broadcasted_iota(jnp.int32, (tq, tk), 1)
    s = jnp.where(kpos <= qpos, scores(last), NEG)
    s_ref[last] = s
    mx = _fold_lanes(s, jnp.maximum)

    def pass1(c, mx):
        s = scores(c)
        s_ref[c] = s
        return jnp.maximum(mx, _fold_lanes(s, jnp.maximum))

    mx = lax.fori_loop(0, last, pass1, mx)
    m = jnp.max(mx, axis=1, keepdims=True)

    def pass2(c, acc):
        p = jnp.exp2(s_ref[c] - m)
        ve = ve_ref[pl.ds(pl.multiple_of(c * tk, tk), tk), :]
        return acc + jnp.dot(p.astype(BF16), ve, preferred_element_type=F32)

    acc = lax.fori_loop(0, last + 1, pass2, jnp.zeros((tq, 2 * HEAD_DIM), F32))
    o_ref[0] = acc[:, 0:HEAD_DIM] / acc[:, HEAD_DIM:HEAD_DIM + 1]


def _moba_prompt(slopes, page_table_flat, q, pen, k, v, pool_k, side_batch, n_pages):
    batch, _, seq, _ = q.shape
    nb = seq // MOBA_BLOCK
    nt = seq // ATTN_TILE
    steps = batch * N_HEADS * nt
    n_side = side_batch * n_pages // steps
    assert n_side * steps == side_batch * n_pages and n_side % PAGES_PER_BLOCK == 0 and n_pages % n_side == 0
    steps_per_seq = n_pages // n_side

    def step(b, h, g):
        return (b * N_HEADS + h) * nt + g

    def page_spec(p):
        return pl.BlockSpec((1, N_HEADS, PAGE, HEAD_DIM),
                            lambda b, h, g, sl, pt: (pt[step(b, h, g) * n_side + p], 0, 0, 0))

    grid_spec = pltpu.PrefetchScalarGridSpec(
        num_scalar_prefetch=2,
        grid=(batch, N_HEADS, nt),
        in_specs=[pl.BlockSpec((1, 1, ATTN_TILE, HEAD_DIM), lambda b, h, g, sl, pt: (b, h, g, 0)),
                  pl.BlockSpec((1, ATTN_TILE, N_HEADS * nb), lambda b, h, g, sl, pt: (b, g, 0)),
                  pl.BlockSpec((1, 1, seq, HEAD_DIM), lambda b, h, g, sl, pt: (b, h, 0, 0)),
                  pl.BlockSpec((1, 1, seq, HEAD_DIM), lambda b, h, g, sl, pt: (b, h, 0, 0))]
                 + [page_spec(p) for p in range(n_side)],
        out_specs=[pl.BlockSpec((1, ATTN_TILE, HEAD_DIM), lambda b, h, g, sl, pt: (b, g, h)),
                   pl.BlockSpec((1, n_side // PAGES_PER_BLOCK, N_HEADS, HEAD_DIM),
                                lambda b, h, g, sl, pt: (step(b, h, g) // steps_per_seq,
                                                         step(b, h, g) % steps_per_seq, 0, 0))],
        scratch_shapes=[pltpu.VMEM((seq, 2 * HEAD_DIM), BF16),
                        pltpu.VMEM((seq, 2 * HEAD_DIM), BF16),
                        pltpu.VMEM((seq // ATTN_KEYS, ATTN_TILE, ATTN_KEYS), F32)],
    )
    return pl.pallas_call(
        functools.partial(_moba_prompt_kernel, nb=nb, n_side=n_side),
        out_shape=[jax.ShapeDtypeStruct((batch, seq, ATTN_WIDTH), F32),
                   jax.ShapeDtypeStruct((side_batch, n_pages // PAGES_PER_BLOCK, N_HEADS, HEAD_DIM), F32)],
        grid_spec=grid_spec,
        compiler_params=_params(("arbitrary", "arbitrary", "arbitrary"), 56),
        name="moba_prompt",
    )(slopes, page_table_flat, q, pen, k, v, *([pool_k] * n_side))


def _glu_proj_kernel(h_ref, wa_ref, wg_ref, hs_ref, u_ref, us_ref, wbf_ref):
    tn = wa_ref.shape[1]

    def glu(rows):
        ag = jnp.dot(rows, wbf_ref[...], preferred_element_type=F32)
        return ag[:, 0:tn] * jax.nn.sigmoid(ag[:, tn:2 * tn])

    @pl.when(pl.program_id(1) == 0)
    def _():
        wbf_ref[:, 0:tn] = wa_ref[...].astype(BF16)
        wbf_ref[:, tn:2 * tn] = wg_ref[...].astype(BF16)
        us_ref[...] = glu(hs_ref[...])

    u_ref[...] = glu(h_ref[...])


def _glu_proj(h, w_in, hs, a_col0, g_col0, width):
    tm, tn = 512, 512
    m, d = h.shape
    ms = hs.shape[0]
    return pl.pallas_call(
        _glu_proj_kernel,
        out_shape=[jax.ShapeDtypeStruct((m, width), F32), jax.ShapeDtypeStruct((ms, width), F32)],
        grid=(width // tn, m // tm),
        in_specs=[pl.BlockSpec((tm, d), lambda n, i: (i, 0)),
                  pl.BlockSpec((d, tn), lambda n, i: (0, a_col0 // tn + n)),
                  pl.BlockSpec((d, tn), lambda n, i: (0, g_col0 // tn + n)),
                  _resident((ms, d), lambda n, i: (0, 0))],
        out_specs=[pl.BlockSpec((tm, tn), lambda n, i: (i, n)),
                   pl.BlockSpec((ms, tn), lambda n, i: (0, n))],
        scratch_shapes=[pltpu.VMEM((d, 2 * tn), BF16)],
        compiler_params=_params(("arbitrary", "arbitrary"), 48),
        name="glu_proj",
    )(h, w_in, w_in, hs)


def _conv_tail(y, gln, bln, gco):
    mu = jnp.mean(y, axis=-1, keepdims=True)
    var = jnp.mean(jnp.square(y - mu), axis=-1, keepdims=True)
    z = _silu((y - mu) * lax.rsqrt(var + EPS) * gln + bln)
    return _rms(z, gco)


def _conv_tile(first, last, u_ref, halo_ref, w_ref, bdw_ref, gln_ref, bln_ref, gco_ref, cn_ref, buf_ref,
               ext_ref, z_ref, y_ref):
    tt = u_ref.shape[0]
    ext_ref[0:CONV_HALO, :] = jnp.where(first, 0.0, halo_ref[...])
    ext_ref[CONV_HALO:CONV_HALO + tt, :] = u_ref[...]
    width = u_ref.shape[1]
    ext_ref[CONV_HALO + tt:CONV_HALO + tt + SUBLANES, :] = jnp.zeros((SUBLANES, width), F32)
    lead = CONV_HALO - CONV_BUF
    rc = z_ref.shape[0] - SUBLANES
    zr = rc + SUBLANES
    for cb in range(width // 128):
        cs = slice(cb * 128, (cb + 1) * 128)
        for c0 in range(0, tt, rc):
            y = None
            for b in range(SUBLANES):
                taps = [k for k in range(CONV_K) if (lead + k) % SUBLANES == b]
                z = None
                for k in taps:
                    r0 = c0 + lead + k - b
                    term = w_ref[k:k + 1, cs] * ext_ref[r0:r0 + zr, cs]
                    z = term if z is None else z + term
                if b == 0:
                    y = z[0:rc, :]
                else:
                    z_ref[...] = z
                    y = y + z_ref[b:b + rc, :]
            y_ref[c0:c0 + rc, cs] = y + bdw_ref[:, cs]
    cn_ref[...] = _conv_tail(y_ref[...], gln_ref[...], bln_ref[...], gco_ref[...]).astype(BF16)

    @pl.when(last)
    def _():
        buf_ref[0] = ext_ref[CONV_HALO + tt - CONV_BUF:CONV_HALO + tt, :]


def _conv_sample_kernel(st_ref, u_ref, w_ref, bdw_ref, gln_ref, bln_ref, gco_ref, cn_ref, buf_ref,
                        ext_ref, *, rows):
    ext_ref[0:CONV_BUF, :] = st_ref[0]
    ext_ref[CONV_BUF:CONV_BUF + rows, :] = u_ref[0]
    acc = jnp.zeros((rows, u_ref.shape[2]), F32)
    for k in range(CONV_K):
        acc = acc + w_ref[k:k + 1, :] * ext_ref[k:k + rows, :]
    y = acc + bdw_ref[...]
    cn_ref[0] = _conv_tail(y, gln_ref[...], bln_ref[...], gco_ref[...]).astype(BF16)
    buf_ref[0] = ext_ref[rows:rows + CONV_BUF, :]


def _conv_sample(state, u, w_dw, b_dw, g_ln, b_ln, g_co):
    batch, rows, width = u.shape
    vec = pl.BlockSpec((1, width), lambda b: (0, 0))
    tok = pl.BlockSpec((1, rows, width), lambda b: (b, 0, 0))
    buf = pl.BlockSpec((1, CONV_BUF, width), lambda b: (b, 0, 0))
    return pl.pallas_call(
        functools.partial(_conv_sample_kernel, rows=rows),
        out_shape=[jax.ShapeDtypeStruct((batch, rows, width), BF16),
                   jax.ShapeDtypeStruct((batch, CONV_BUF, width), F32)],
        grid=(batch,),
        in_specs=[buf, tok, pl.BlockSpec((CONV_K, width), lambda b: (0, 0)), vec, vec, vec, vec],
        out_specs=[tok, buf],
        scratch_shapes=[pltpu.VMEM((CONV_BUF + rows + 6, width), F32)],
        compiler_params=_params(("parallel",), 32),
        name="conv_sample",
    )(state, u, w_dw, b_dw, g_ln, b_ln, g_co)


def _out_proj_kernel(o_ref, *refs, tiles_per_seq):
    n_conv = len(refs) - 10
    cn_refs = refs[:n_conv]
    w_ref, x_ref, gao_ref, gpm_ref, gpf_ref, gt_ref, sc_ref, sh_ref, x1_ref, h2_ref = refs[n_conv:]
    tm = o_ref.shape[0]
    gt, sc, sh = _mod2d(gt_ref), _mod2d(sc_ref), _mod2d(sh_ref)
    chunk = tm // 2 if tm % 512 == 0 else tm
    for r0 in range(0, tm, chunk):
        rs = slice(r0, r0 + chunk)
        rows = lambda v: v if v.shape[0] == 1 else v[rs]
        an = _rms(o_ref[rs, :], gao_ref[...]).astype(BF16)
        cn = cn_refs[0][rs, :]
        for s in range(1, n_conv):
            cn = jnp.where(pl.program_id(0) >= s * tiles_per_seq, cn_refs[s][rs, :], cn)
        mix = jnp.concatenate([an, cn], axis=1)
        merged = jnp.dot(mix, w_ref[...], preferred_element_type=F32)
        x1 = x_ref[rs, :] + rows(gt) * _rms(merged, gpm_ref[...])
        x1_ref[rs, :] = x1
        h2_ref[rs, :] = (_rms(x1, gpf_ref[...]) * (1.0 + rows(sc)) + rows(sh)).astype(BF16)


def _out_proj(o_attn, conv_n, w_bf, x, g_ao, g_pm, g_pf, mod, tm, rows_per_group):
    m, d = x.shape
    aw = o_attn.shape[1]
    cw = conv_n[0].shape[1]
    tiles_per_seq = conv_n[0].shape[0] // tm
    assert tiles_per_seq * tm == conv_n[0].shape[0] and tiles_per_seq * len(conv_n) == m // tm
    gt, gt_spec = _mod_spec(mod, 2, tm, rows_per_group)
    sc, sc_spec = _mod_spec(mod, 4, tm, rows_per_group)
    sh, sh_spec = _mod_spec(mod, 3, tm, rows_per_group)
    row = lambda w: pl.BlockSpec((tm, w), lambda i: (i, 0))
    vec = lambda w: pl.BlockSpec((1, w), lambda i: (0, 0))

    def conv_spec(s):
        return pl.BlockSpec((tm, cw), lambda i: (jnp.clip(i - s * tiles_per_seq, 0, tiles_per_seq - 1), 0))

    return pl.pallas_call(
        functools.partial(_out_proj_kernel, tiles_per_seq=tiles_per_seq),
        out_shape=[jax.ShapeDtypeStruct((m, d), F32), jax.ShapeDtypeStruct((m, d), BF16)],
        grid=(m // tm,),
        in_specs=[row(aw)] + [conv_spec(s) for s in range(len(conv_n))]
                 + [_resident((aw + cw, d), lambda i: (0, 0)), row(d),
                    vec(aw), vec(d), vec(d), gt_spec, sc_spec, sh_spec],
        out_specs=[row(d), row(d)],
        compiler_params=_params(("parallel",), 56),
        name="out_proj",
    )(o_attn, *conv_n, w_bf, x, g_ao, g_pm, g_pf, gt, sc, sh)


def _ffn_up_kernel(h_ref, wg_ref, wu_ref, hs_ref, wd_ref, a_ref, as_ref, wdbf_ref, wbf_ref):
    tn = wg_ref.shape[1]

    def swiglu(rows):
        gu = jnp.dot(rows, wbf_ref[...], preferred_element_type=F32)
        return (_silu(gu[:, 0:tn]) * gu[:, tn:2 * tn]).astype(BF16)

    @pl.when(pl.program_id(1) == 0)
    def _():
        wbf_ref[:, 0:tn] = wg_ref[...].astype(BF16)
        wbf_ref[:, tn:2 * tn] = wu_ref[...].astype(BF16)
        as_ref[...] = swiglu(hs_ref[...])

    a_ref[...] = swiglu(h_ref[...])
    wdbf_ref[...] = wd_ref[...].astype(BF16)


def _ffn_up(h, w_gate_up, hs, w_down):
    tm, tn = 1024, 512
    m, d = h.shape
    ms = hs.shape[0]
    d_ff = w_gate_up.shape[1] // 2
    n_blocks, m_tiles = d_ff // tn, m // tm
    slab = w_down.shape[0] // (n_blocks * m_tiles)
    return pl.pallas_call(
        _ffn_up_kernel,
        out_shape=[jax.ShapeDtypeStruct((m, d_ff), BF16), jax.ShapeDtypeStruct((ms, d_ff), BF16),
                   jax.ShapeDtypeStruct(w_down.shape, BF16)],
        grid=(n_blocks, m_tiles),
        in_specs=[pl.BlockSpec((tm, d), lambda n, i: (i, 0)),
                  pl.BlockSpec((d, tn), lambda n, i: (0, n)),
                  pl.BlockSpec((d, tn), lambda n, i: (0, n_blocks + n)),
                  _resident((ms, d), lambda n, i: (0, 0)),
                  pl.BlockSpec((slab, w_down.shape[1]), lambda n, i: (n * m_tiles + i, 0))],
        out_specs=[pl.BlockSpec((tm, tn), lambda n, i: (i, n)),
                   pl.BlockSpec((ms, tn), lambda n, i: (0, n)),
                   pl.BlockSpec((slab, w_down.shape[1]), lambda n, i: (n * m_tiles + i, 0))],
        scratch_shapes=[pltpu.VMEM((d, 2 * tn), BF16)],
        compiler_params=_params(("arbitrary", "arbitrary"), 56),
        name="ffn_up",
    )(h, w_gate_up, w_gate_up, hs, w_down)


def _ffn_down_kernel(a_ref, w_ref, x1_ref, g_ref, gt_ref, y_ref):
    z = jnp.dot(a_ref[...], w_ref[...], preferred_element_type=F32)
    y_ref[...] = x1_ref[...] + _mod2d(gt_ref) * _rms(z, g_ref[...])


def _ffn_down(act, w_bf, x1, g_post, mod, tm, rows_per_group):
    m, d_ff = act.shape
    d = w_bf.shape[1]
    gt, gt_spec = _mod_spec(mod, 5, tm, rows_per_group)
    return pl.pallas_call(
        _ffn_down_kernel,
        out_shape=jax.ShapeDtypeStruct((m, d), F32),
        grid=(m // tm,),
        in_specs=[pl.BlockSpec((tm, d_ff), lambda i: (i, 0)),
                  _resident((d_ff, d), lambda i: (0, 0)),
                  pl.BlockSpec((tm, d), lambda i: (i, 0)),
                  pl.BlockSpec((1, d), lambda i: (0, 0)), gt_spec],
        out_specs=pl.BlockSpec((tm, d), lambda i: (i, 0)),
        compiler_params=_params(("parallel",), 48),
        name="ffn_down",
    )(act, w_bf, x1, g_post, gt)


SELECT_ROWS = 16


def _select_sample_kernel(pt_ref, q_ref, km_ref, idx_ref, *, n_q, n_pages):
    b = pl.program_id(0)
    nb = km_ref.shape[2]
    lane = lax.broadcasted_iota(jnp.int32, (nb, 128), 1)
    rowi = lax.broadcasted_iota(jnp.int32, (nb, 128), 0)
    s = jnp.full((nb, 128), NEG, F32)
    for hh in range(N_HEADS):
        km = km_ref[0, hh]
        for t in range(n_q):
            col = jnp.sum(km * q_ref[0, hh, t:t + 1, :], axis=1, keepdims=True)
            s = jnp.where(lane == hh * n_q + t, col, s)
    out_row = lax.broadcasted_iota(jnp.int32, (SELECT_ROWS, 128), 0)
    out = jnp.zeros((SELECT_ROWS, 128), jnp.int32)
    for r in range(MOBA_TOPK):
        top = jnp.max(s, axis=0, keepdims=True)
        arg = jnp.min(jnp.where(s == top, rowi, nb), axis=0, keepdims=True)
        out = jnp.where(out_row == r, arg, out)
        s = jnp.where(rowi == arg, -jnp.inf, s)
        for pp in range(PAGES_PER_BLOCK):
            page = jnp.zeros((1, 128), jnp.int32)
            for j in range(nb):
                page = jnp.where(arg == j, pt_ref[b * n_pages + j * PAGES_PER_BLOCK + pp], page)
            out = jnp.where(out_row == 8 + r * PAGES_PER_BLOCK + pp, page, out)
    idx_ref[0] = out


def _select_sample(page_table_flat, q, kmean, n_pages):
    batch, _, n_q, _ = q.shape
    nb = kmean.shape[2]
    grid_spec = pltpu.PrefetchScalarGridSpec(
        num_scalar_prefetch=1,
        grid=(batch,),
        in_specs=[pl.BlockSpec((1, N_HEADS, n_q, HEAD_DIM), lambda b, pt: (b, 0, 0, 0)),
                  pl.BlockSpec((1, N_HEADS, nb, HEAD_DIM), lambda b, pt: (b, 0, 0, 0))],
        out_specs=pl.BlockSpec((1, SELECT_ROWS, 128), lambda b, pt: (b, 0, 0)),
    )
    return pl.pallas_call(
        functools.partial(_select_sample_kernel, n_q=n_q, n_pages=n_pages),
        out_shape=jax.ShapeDtypeStruct((batch, SELECT_ROWS, 128), jnp.int32),
        grid_spec=grid_spec,
        compiler_params=_params(("arbitrary",), 32),
        name="select_sample",
    )(page_table_flat, q, kmean)


def _moba_sample_kernel(page_ref, idx_ref, slopes_ref, q_ref, kn_ref, vn_ref, *refs, n_q, past):
    n_sel = n_q * MOBA_TOPK * PAGES_PER_BLOCK
    k_refs, v_refs, o_ref = refs[:n_sel], refs[n_sel:2 * n_sel], refs[2 * n_sel]
    b = pl.program_id(0)
    h = pl.program_id(1)
    slope = slopes_ref[h]
    key = lax.broadcasted_iota(jnp.int32, (PAGE, 1), 0)
    for t in range(n_q):
        q = q_ref[0, 0, t:t + 1, :] * (HEAD_DIM ** -0.5)
        scores = []
        for s in range(MOBA_TOPK):
            blk = idx_ref[((b * N_HEADS + h) * n_q + t) * MOBA_TOPK + s]
            for pp in range(PAGES_PER_BLOCK):
                kp = k_refs[(t * MOBA_TOPK + s) * PAGES_PER_BLOCK + pp][0, 0]
                dist = (past + t - blk * MOBA_BLOCK - pp * PAGE - key).astype(F32)
                scores.append(jnp.sum(kp * q, axis=1, keepdims=True) - slope * dist)
        own = []
        for t2 in range(t + 1):
            own.append(jnp.sum(kn_ref[0, 0, t2:t2 + 1, :] * q, axis=1, keepdims=True) - slope * float(t - t2))
        m = own[0]
        for sc in scores:
            m = jnp.maximum(m, jnp.max(sc, axis=0, keepdims=True))
        for sc in own[1:]:
            m = jnp.maximum(m, sc)
        l = jnp.zeros((1, 1), F32)
        acc = jnp.zeros((1, HEAD_DIM), F32)
        for n, sc in enumerate(scores):
            p = jnp.exp(sc - m)
            l = l + jnp.sum(p, axis=0, keepdims=True)
            acc = acc + jnp.sum(p * v_refs[n + t * MOBA_TOPK * PAGES_PER_BLOCK][0, 0], axis=0, keepdims=True)
        for t2, sc in enumerate(own):
            p = jnp.exp(sc - m)
            l = l + p
            acc = acc + p * vn_ref[0, 0, t2:t2 + 1, :]
        o_ref[0, 0, t:t + 1, :] = acc / l


def _moba_sample(pages_flat, idx_flat, slopes, q, k_new, v_new, pool_k, pool_v, n_pages):
    batch, _, n_q, _ = q.shape
    past = n_pages * PAGE
    per_query = MOBA_TOPK * PAGES_PER_BLOCK

    def sel_spec(n):
        return pl.BlockSpec((1, 1, PAGE, HEAD_DIM),
                            lambda b, h, pg, idx, sl: (pg[(b * N_HEADS + h) * n_q * per_query + n], h, 0, 0))

    sel_specs = [sel_spec(n) for n in range(n_q * per_query)]
    tok = pl.BlockSpec((1, 1, n_q, HEAD_DIM), lambda b, h, pg, idx, sl: (b, h, 0, 0))
    grid_spec = pltpu.PrefetchScalarGridSpec(
        num_scalar_prefetch=3,
        grid=(batch, N_HEADS),
        in_specs=[tok, tok, tok] + sel_specs + sel_specs,
        out_specs=tok,
    )
    n_sel = len(sel_specs)
    return pl.pallas_call(
        functools.partial(_moba_sample_kernel, n_q=n_q, past=past),
        out_shape=jax.ShapeDtypeStruct((batch, N_HEADS, n_q, HEAD_DIM), F32),
        grid_spec=grid_spec,
        compiler_params=_params(("parallel", "parallel"), 32),
        name="moba_sample",
    )(pages_flat, idx_flat, slopes, q, k_new, v_new, *([pool_k] * n_sel), *([pool_v] * n_sel))


def kernel(x_prompt, x_sample, cache_k, cache_v, state_conv, page_table, c_prompt, c_sample, w_ada, b_ada, g_pre_mix, w_in, w_dw, b_dw, g_conv_ln, b_conv_ln, g_attn_out, g_conv_out, w_out, g_post_mix, g_pre_ffn, w_gate_up, w_down, g_post_ffn):
    depth = w_ada.shape[0]
    assert depth == 1, "single layer: the prompt and sample residual streams are not chained across layers here"
    batch, seq, d = x_prompt.shape
    dec_batch, dec_seq, _ = x_sample.shape
    n_pages = page_table.shape[1]
    past = n_pages * PAGE
    conv_w = w_dw.shape[-1]
    assert past % MOBA_BLOCK == 0 and dec_seq <= MOBA_BLOCK and past // MOBA_BLOCK >= MOBA_TOPK
    assert seq % ATTN_TILE == 0 and d == ATTN_WIDTH + conv_w

    slopes = 2.0 ** (-8.0 * (jnp.arange(N_HEADS, dtype=F32) + 1.0) / N_HEADS)
    l = 0
    vec = lambda a: a[l].reshape(1, -1)
    g_pm, g_ao, g_co = vec(g_pre_mix), vec(g_attn_out), vec(g_conv_out)
    g_post, g_pf, g_pffn = vec(g_post_mix), vec(g_pre_ffn), vec(g_post_ffn)
    bdw, gln, bln = vec(b_dw), vec(g_conv_ln), vec(b_conv_ln)
    wdw = w_dw[l].reshape(CONV_K, conv_w)

    n_c = batch + dec_batch
    c_rows = -(-n_c // 8) * 8
    c_all = jnp.concatenate([c_prompt, c_sample, jnp.zeros((c_rows - n_c, d), F32)], axis=0)
    mod = _ada(c_all, w_ada[l], vec(b_ada))
    mod_p = mod.reshape(c_rows * 6, 1, d)
    mod_rows = jnp.repeat(mod[batch:n_c].reshape(dec_batch, 6, d), dec_seq, axis=0)
    mod_s = [mod_rows[:, w] for w in range(6)]

    m_s = dec_batch * dec_seq
    xp = x_prompt.reshape(batch * seq, d)
    xs = x_sample.reshape(m_s, d)
    h = _modnorm(xp, g_pm, mod_p, 1, 0, 512, seq)
    hs = _modnorm(xs, g_pm, mod_s, 1, 0, m_s, None)

    assert batch == 2, "the conv side jobs are mapped one prompt sequence per K/V projection call"
    u, u_s = _glu_proj(h, w_in[l], hs, 3 * ATTN_WIDTH, 3 * ATTN_WIDTH + conv_w, conv_w)
    conv_params = (wdw, bdw, gln, bln, g_co)
    k_pages, k_heads, k_rows_s, kmean, w_out_bf, conv_n0, conv_buf0 = _kv_proj(
        h, w_in[l], 1, hs, batch, seq, True, 0, u, conv_params, w_out[l])
    v_pages, v_heads, v_rows_s, conv_n1, conv_buf1 = _kv_proj(
        h, w_in[l], 2, hs, batch, seq, False, 1, u, conv_params)
    conv_buf_p = jnp.concatenate([conv_buf0, conv_buf1], axis=0)
    q_heads, pen, q_rows_s = _q_proj(h, w_in[l], hs, kmean.reshape(batch, seq // MOBA_BLOCK, ATTN_WIDTH),
                                     batch, seq)

    pt_flat = page_table.reshape(-1)
    o_attn, kmean_s = _moba_prompt(slopes, pt_flat, q_heads, pen, k_heads, v_heads, cache_k[l], dec_batch, n_pages)
    o_attn = o_attn.reshape(batch * seq, ATTN_WIDTH)
    x1, h2 = _out_proj(o_attn, [conv_n0, conv_n1], w_out_bf, xp, g_ao, g_post, g_pf, mod_p, 512, seq)

    to_heads = lambda t: t.reshape(dec_batch, dec_seq, N_HEADS, HEAD_DIM).transpose(0, 2, 1, 3)
    q_s, k_s, v_s = to_heads(q_rows_s), to_heads(k_rows_s), to_heads(v_rows_s)
    kmean_s = kmean_s.transpose(0, 2, 1, 3)
    sel = _select_sample(pt_flat, q_s, kmean_s, n_pages)[:, :, :N_HEADS * dec_seq]
    per_query = lambda rows: rows.reshape(dec_batch, -1, N_HEADS, dec_seq).transpose(0, 2, 3, 1).reshape(-1)
    idx_flat = per_query(sel[:, :MOBA_TOPK])
    pages_flat = per_query(sel[:, 8:8 + MOBA_TOPK * PAGES_PER_BLOCK])
    o_s = _moba_sample(pages_flat, idx_flat, slopes, q_s, k_s, v_s, cache_k[l], cache_v[l], n_pages)
    o_attn_s = o_s.transpose(0, 2, 1, 3).reshape(m_s, ATTN_WIDTH)
    conv_n_s, conv_buf_s = _conv_sample(state_conv[l], u_s.reshape(dec_batch, dec_seq, conv_w),
                                        wdw, bdw, gln, bln, g_co)
    x1_s, h2_s = _out_proj(o_attn_s, [conv_n_s.reshape(m_s, conv_w)], w_out_bf, xs, g_ao, g_post, g_pf,
                           mod_s, m_s, None)

    act, act_s, w_down_bf = _ffn_up(h2, w_gate_up[l], h2_s, w_down[l])
    y_p = _ffn_down(act, w_down_bf, x1, g_pffn, mod_p, 256, seq).reshape(batch, seq, d)
    y_s = _ffn_down(act_s, w_down_bf, x1_s, g_pffn, mod_s, m_s, None).reshape(dec_batch, dec_seq, d)

    return (y_p, y_s, k_pages[None], v_pages[None], conv_buf_p[None],
            k_s[None], v_s[None], conv_buf_s[None])
```

```python
import functools

import jax
import jax.numpy as jnp
from jax import lax
from jax.experimental import pallas as pl
from jax.experimental.pallas import tpu as pltpu

F32 = jnp.float32
BF16 = jnp.bfloat16

N_HEADS = 8
HEAD_DIM = 128
ATTN_WIDTH = N_HEADS * HEAD_DIM
CONV_K = 31
CONV_BUF = CONV_K - 1
CONV_HALO = 32
CONV_CHUNK = 64
SUBLANES = 8
PAGE = 128
MOBA_BLOCK = 256
MOBA_TOPK = 3
PAGES_PER_BLOCK = MOBA_BLOCK // PAGE
ATTN_TILE = 4 * MOBA_BLOCK
PROJ_TILE = 512
ATTN_KEYS = 4 * MOBA_BLOCK
LOG2E = 1.4426950408889634
EPS = 1e-6
NEG = -1e30
MIB = 1024 * 1024

_NT = (((1,), (1,)), ((), ()))


def _params(sem, vmem_mib, flags=None):
    return pltpu.CompilerParams(dimension_semantics=sem, vmem_limit_bytes=vmem_mib * MIB, flags=flags)


def _resident(shape, index_map):
    return pl.BlockSpec(shape, index_map, pipeline_mode=pl.Buffered(1))


def _rms(x, g):
    return x * lax.rsqrt(jnp.mean(x * x, axis=-1, keepdims=True) + EPS) * g


def _silu(x):
    return x * jax.nn.sigmoid(x)


def _mod2d(ref):
    v = ref[...]
    return v.reshape(v.shape[-2], v.shape[-1])


def _mod_spec(mod, which, tm, rows_per_group):
    if rows_per_group is None:
        return mod[which], pl.BlockSpec((tm, mod[which].shape[1]), lambda i, *_: (i, 0))
    tiles = rows_per_group // tm
    return mod, pl.BlockSpec((1, 1, mod.shape[2]), lambda i, *_: ((i // tiles) * 6 + which, 0, 0))


def _ada_kernel(c_ref, w_ref, b_ref, o_ref):
    s = _silu(c_ref[...]).astype(BF16)
    o_ref[...] = jnp.dot(s, w_ref[...].astype(BF16), preferred_element_type=F32) + b_ref[...]


def _ada(c_all, w, b):
    rows, d = c_all.shape
    n = w.shape[1]
    tn = 1024
    return pl.pallas_call(
        _ada_kernel,
        out_shape=jax.ShapeDtypeStruct((rows, n), F32),
        grid=(n // tn,),
        in_specs=[pl.BlockSpec((rows, d), lambda j: (0, 0)),
                  pl.BlockSpec((d, tn), lambda j: (0, j)),
                  pl.BlockSpec((1, tn), lambda j: (0, j))],
        out_specs=pl.BlockSpec((rows, tn), lambda j: (0, j)),
        compiler_params=_params(("arbitrary",), 40),
        name="ada_mod",
    )(c_all, w, b)


def _modnorm_kernel(x_ref, g_ref, sc_ref, sh_ref, o_ref):
    h = _rms(x_ref[...], g_ref[...]) * (1.0 + _mod2d(sc_ref)) + _mod2d(sh_ref)
    o_ref[...] = h.astype(BF16)


def _modnorm(x, g, mod, which_sc, which_sh, tm, rows_per_group):
    m, d = x.shape
    sc, sc_spec = _mod_spec(mod, which_sc, tm, rows_per_group)
    sh, sh_spec = _mod_spec(mod, which_sh, tm, rows_per_group)
    return pl.pallas_call(
        _modnorm_kernel,
        out_shape=jax.ShapeDtypeStruct((m, d), BF16),
        grid=(m // tm,),
        in_specs=[pl.BlockSpec((tm, d), lambda i: (i, 0)),
                  pl.BlockSpec((1, d), lambda i: (0, 0)), sc_spec, sh_spec],
        out_specs=pl.BlockSpec((tm, d), lambda i: (i, 0)),
        compiler_params=_params(("parallel",), 32),
        name="modnorm",
    )(x, g, sc, sh)


def _kv_proj_kernel(*refs, tm, with_mean, with_cast):
    it = iter(refs)
    h_ref, w_ref, hs_ref = next(it), next(it), next(it)
    cast_ref = next(it) if with_cast else None
    conv_in = [next(it) for _ in range(7)]
    pages_ref, heads_ref, s_ref = next(it), next(it), next(it)
    mean_ref = next(it) if with_mean else None
    cast_out_ref = next(it) if with_cast else None
    conv_out = [next(it), next(it)]
    wbf_ref = next(it)
    conv_scratch = [next(it), next(it), next(it)]
    i = pl.program_id(0)

    @pl.when(i == 0)
    def _():
        wbf_ref[...] = w_ref[...].astype(BF16)
        s_ref[...] = jnp.dot(hs_ref[...], wbf_ref[...], preferred_element_type=F32)

    for pair in range(N_HEADS // 2):
        ps = slice(pair * 2 * HEAD_DIM, (pair + 1) * 2 * HEAD_DIM)
        acc = jnp.dot(h_ref[...], wbf_ref[:, ps], preferred_element_type=F32)
        for sub in range(2):
            hh = pair * 2 + sub
            cs = slice(sub * HEAD_DIM, (sub + 1) * HEAD_DIM)
            heads_ref[0, hh] = acc[:, cs].astype(BF16)
            for p in range(tm // PAGE):
                pages_ref[0, p, hh] = acc[p * PAGE:(p + 1) * PAGE, cs]
        if with_mean:
            for mb in range(tm // MOBA_BLOCK):
                mean_ref[mb, :, ps] = jnp.mean(acc[mb * MOBA_BLOCK:(mb + 1) * MOBA_BLOCK, :], axis=0, keepdims=True)
    if with_cast:
        cast_out_ref[...] = cast_ref[...].astype(BF16)
    _conv_tile(i == 0, i == pl.num_programs(0) - 1, *conv_in, *conv_out, *conv_scratch)


def _kv_proj(h, w_in, col_block, hs, batch, seq, with_mean, conv_seq, u, conv_params, cast_src=None):
    tm = PROJ_TILE
    m, d = h.shape
    ms = hs.shape[0]
    tiles = seq // tm
    steps = m // tm
    width = u.shape[1]
    tt = seq // steps
    assert tt * steps == seq and tt % CONV_HALO == 0 and tt >= CONV_BUF
    halo_per_tile = tt // CONV_HALO
    in_specs = [pl.BlockSpec((tm, d), lambda i: (i, 0)),
                _resident((d, ATTN_WIDTH), lambda i: (0, col_block)),
                _resident((ms, d), lambda i: (0, 0))]
    args = [h, w_in, hs]
    out_shape = [jax.ShapeDtypeStruct((batch, seq // PAGE, N_HEADS, PAGE, HEAD_DIM), F32),
                 jax.ShapeDtypeStruct((batch, N_HEADS, seq, HEAD_DIM), BF16),
                 jax.ShapeDtypeStruct((ms, ATTN_WIDTH), F32)]
    out_specs = [pl.BlockSpec((1, tm // PAGE, N_HEADS, PAGE, HEAD_DIM), lambda i: (i // tiles, i % tiles, 0, 0, 0)),
                 pl.BlockSpec((1, N_HEADS, tm, HEAD_DIM), lambda i: (i // tiles, 0, i % tiles, 0)),
                 pl.BlockSpec((ms, ATTN_WIDTH), lambda i: (0, 0))]
    if with_mean:
        out_shape.append(jax.ShapeDtypeStruct((m // MOBA_BLOCK, 1, ATTN_WIDTH), F32))
        out_specs.append(pl.BlockSpec((tm // MOBA_BLOCK, 1, ATTN_WIDTH), lambda i: (i, 0, 0)))
    if cast_src is not None:
        rows, cols = cast_src.shape
        slab = rows // steps
        in_specs.append(pl.BlockSpec((slab, cols), lambda i: (i, 0)))
        args.append(cast_src)
        out_shape.append(jax.ShapeDtypeStruct((rows, cols), BF16))
        out_specs.append(pl.BlockSpec((slab, cols), lambda i: (i, 0)))
    vec = pl.BlockSpec((1, width), lambda i: (0, 0))
    in_specs += [pl.BlockSpec((tt, width), lambda i: (conv_seq * steps + i, 0)),
                 pl.BlockSpec((CONV_HALO, width),
                              lambda i: (jnp.maximum((conv_seq * steps + i) * halo_per_tile - 1, 0), 0)),
                 pl.BlockSpec((CONV_K, width), lambda i: (0, 0)), vec, vec, vec, vec]
    args += [u, u, *conv_params]
    out_shape += [jax.ShapeDtypeStruct((seq, width), BF16), jax.ShapeDtypeStruct((1, CONV_BUF, width), F32)]
    out_specs += [pl.BlockSpec((tt, width), lambda i: (i, 0)),
                  pl.BlockSpec((1, CONV_BUF, width), lambda i: (0, 0, 0))]
    return pl.pallas_call(
        functools.partial(_kv_proj_kernel, tm=tm, with_mean=with_mean, with_cast=cast_src is not None),
        out_shape=out_shape,
        grid=(steps,),
        in_specs=in_specs,
        out_specs=out_specs,
        scratch_shapes=[pltpu.VMEM((d, ATTN_WIDTH), BF16),
                        pltpu.VMEM((CONV_HALO + tt + SUBLANES, width), F32),
                        pltpu.VMEM((CONV_CHUNK + SUBLANES, 128), F32),
                        pltpu.VMEM((tt, width), F32)],
        compiler_params=_params(("arbitrary",), 56),
        name="kv_proj",
    )(*args)


def _q_proj_kernel(h_ref, w_ref, hs_ref, km_ref, q_ref, pen_ref, qs_ref, wbf_ref, sm_ref, *, tm, tiles):
    i = pl.program_id(0)

    @pl.when(i == 0)
    def _():
        wbf_ref[...] = w_ref[...].astype(BF16)
        qs_ref[...] = jnp.dot(hs_ref[...], wbf_ref[...], preferred_element_type=F32)

    acc = jnp.dot(h_ref[...], wbf_ref[...], preferred_element_type=F32)
    km = km_ref[0]
    nb = km.shape[0]
    jrow = lax.broadcasted_iota(jnp.int32, (nb, tm), 0)
    tcol = lax.broadcasted_iota(jnp.int32, (nb, tm), 1)
    n_full = (i % tiles) * (tm // MOBA_BLOCK) + lax.shift_right_logical(tcol, MOBA_BLOCK.bit_length() - 1)
    valid = jrow < n_full
    own = jrow == n_full
    pens = []
    for hh in range(N_HEADS):
        cs = slice(hh * HEAD_DIM, (hh + 1) * HEAD_DIM)
        qh = acc[:, cs]
        q_ref[0, hh] = (qh * (HEAD_DIM ** -0.5 * LOG2E)).astype(BF16)
        s = lax.dot_general(km[:, cs], qh, _NT, precision=lax.Precision.HIGHEST, preferred_element_type=F32)
        sm = jnp.where(valid, s, NEG)
        sm_ref[...] = sm
        rank = jnp.zeros((nb, tm), jnp.int32)
        for jp in range(nb):
            row = sm_ref[jp:jp + 1, :]
            gt = jnp.where(row > sm, 1, 0)
            ge = jnp.where(row >= sm, 1, 0)
            rank = rank + jnp.where(jrow > jp, ge, gt)
        keep = (valid & (rank < MOBA_TOPK)) | own
        pens.append(jnp.where(keep, 0.0, NEG))
    pen = jnp.concatenate(pens, axis=0)
    pen_ref[0] = pen.T.astype(BF16)


def _q_proj(h, w_in, hs, kmean, batch, seq):
    tm = PROJ_TILE
    m, d = h.shape
    ms = hs.shape[0]
    tiles = seq // tm
    nb = kmean.shape[1]
    return pl.pallas_call(
        functools.partial(_q_proj_kernel, tm=tm, tiles=tiles),
        out_shape=[jax.ShapeDtypeStruct((batch, N_HEADS, seq, HEAD_DIM), BF16),
                   jax.ShapeDtypeStruct((batch, seq, N_HEADS * nb), BF16),
                   jax.ShapeDtypeStruct((ms, ATTN_WIDTH), F32)],
        grid=(m // tm,),
        in_specs=[pl.BlockSpec((tm, d), lambda i: (i, 0)),
                  _resident((d, ATTN_WIDTH), lambda i: (0, 0)),
                  _resident((ms, d), lambda i: (0, 0)),
                  pl.BlockSpec((1, nb, ATTN_WIDTH), lambda i: (i // tiles, 0, 0))],
        out_specs=[pl.BlockSpec((1, N_HEADS, tm, HEAD_DIM), lambda i: (i // tiles, 0, i % tiles, 0)),
                   pl.BlockSpec((1, tm, N_HEADS * nb), lambda i: (i // tiles, i % tiles, 0)),
                   pl.BlockSpec((ms, ATTN_WIDTH), lambda i: (0, 0))],
        scratch_shapes=[pltpu.VMEM((d, ATTN_WIDTH), BF16), pltpu.VMEM((nb, tm), F32)],
        compiler_params=_params(("arbitrary",), 48),
        name="q_proj",
    )(h, w_in, hs, kmean)


def _fold_lanes(x, op):
    out = x[:, 0:128]
    for c in range(1, x.shape[1] // 128):
        out = op(out, x[:, c * 128:(c + 1) * 128])
    return out


def _block_key_means(page_refs, o_ref):
    for blk in range(len(page_refs) // PAGES_PER_BLOCK):
        tot = jnp.zeros((N_HEADS, HEAD_DIM), F32)
        for pp in range(PAGES_PER_BLOCK):
            tot = tot + jnp.sum(page_refs[blk * PAGES_PER_BLOCK + pp][0], axis=1)
        o_ref[0, blk] = tot * (1.0 / MOBA_BLOCK)


def _moba_prompt_kernel(slopes_ref, pt_ref, q_ref, pen_ref, k_ref, v_ref, *refs, nb, n_side):
    page_refs = refs[:n_side]
    o_ref, km_ref, ke_ref, ve_ref, s_ref = refs[n_side:]
    _block_key_means(page_refs, km_ref)

    tq, tk = ATTN_TILE, ATTN_KEYS
    h = pl.program_id(1)
    g = pl.program_id(2)
    slope = slopes_ref[h] * LOG2E
    seq = k_ref.shape[2]

    @pl.when(g == 0)
    def _():
        blk = lax.shift_right_logical(lax.broadcasted_iota(jnp.int32, (seq, HEAD_DIM), 0),
                                      MOBA_BLOCK.bit_length() - 1)
        lane = lax.broadcasted_iota(jnp.int32, (seq, HEAD_DIM), 1)
        ke_ref[:, 0:HEAD_DIM] = k_ref[0, 0]
        ke_ref[:, HEAD_DIM:2 * HEAD_DIM] = jnp.where(lane == h * nb + blk, 1.0, 0.0).astype(BF16)
        ve_ref[:, 0:HEAD_DIM] = v_ref[0, 0]
        ve_ref[:, HEAD_DIM:2 * HEAD_DIM] = jnp.where(lane == 0, 1.0, 0.0).astype(BF16)

    qe = jnp.concatenate([q_ref[0, 0], pen_ref[0]], axis=1)
    key_lane = lax.broadcasted_iota(jnp.int32, (1, tk), 1)

    def scores(c):
        ke = ke_ref[pl.ds(pl.multiple_of(c * tk, tk), tk), :]
        s = lax.dot_general(qe, ke, _NT, preferred_element_type=F32)
        return s + slope * (c * tk + key_lane).astype(F32)

    last = lax.shift_right_logical(g * tq, tk.bit_length() - 1)
    qpos = g * tq + lax.broadcasted_iota(jnp.int32, (tq, tk), 0)
    kpos = last * tk + lax.broadcasted_iota(jnp.int32, (tq, tk), 1)
    s = jnp.where(kpos <= qpos, scores(last), NEG)
    s_ref[last] = s
    mx = _fold_lanes(s, jnp.maximum)

    def pass1(c, mx):
        s = scores(c)
        s_ref[c] = s
        return jnp.maximum(mx, _fold_lanes(s, jnp.maximum))

    mx = lax.fori_loop(0, last, pass1, mx)
    m = jnp.max(mx, axis=1, keepdims=True)

    def pass2(c, acc):
        p = jnp.exp2(s_ref[c] - m)
        ve = ve_ref[pl.ds(pl.multiple_of(c * tk, tk), tk), :]
        return acc + jnp.dot(p.astype(BF16), ve, preferred_element_type=F32)

    acc = lax.fori_loop(0, last + 1, pass2, jnp.zeros((tq, 2 * HEAD_DIM), F32))
    o_ref[0] = acc[:, 0:HEAD_DIM] / acc[:, HEAD_DIM:HEAD_DIM + 1]


def _moba_prompt(slopes, page_table_flat, q, pen, k, v, pool_k, side_batch, n_pages):
    batch, _, seq, _ = q.shape
    nb = seq // MOBA_BLOCK
    nt = seq // ATTN_TILE
    steps = batch * N_HEADS * nt
    n_side = side_batch * n_pages // steps
    assert n_side * steps == side_batch * n_pages and n_side % PAGES_PER_BLOCK == 0 and n_pages % n_side == 0
    steps_per_seq = n_pages // n_side

    def step(b, h, g):
        return (b * N_HEADS + h) * nt + g

    def page_spec(p):
        return pl.BlockSpec((1, N_HEADS, PAGE, HEAD_DIM),
                            lambda b, h, g, sl, pt: (pt[step(b, h, g) * n_side + p], 0, 0, 0))

    grid_spec = pltpu.PrefetchScalarGridSpec(
        num_scalar_prefetch=2,
        grid=(batch, N_HEADS, nt),
        in_specs=[pl.BlockSpec((1, 1, ATTN_TILE, HEAD_DIM), lambda b, h, g, sl, pt: (b, h, g, 0)),
                  pl.BlockSpec((1, ATTN_TILE, N_HEADS * nb), lambda b, h, g, sl, pt: (b, g, 0)),
                  pl.BlockSpec((1, 1, seq, HEAD_DIM), lambda b, h, g, sl, pt: (b, h, 0, 0)),
                  pl.BlockSpec((1, 1, seq, HEAD_DIM), lambda b, h, g, sl, pt: (b, h, 0, 0))]
                 + [page_spec(p) for p in range(n_side)],
        out_specs=[pl.BlockSpec((1, ATTN_TILE, HEAD_DIM), lambda b, h, g, sl, pt: (b, g, h)),
                   pl.BlockSpec((1, n_side // PAGES_PER_BLOCK, N_HEADS, HEAD_DIM),
                                lambda b, h, g, sl, pt: (step(b, h, g) // steps_per_seq,
                                                         step(b, h, g) % steps_per_seq, 0, 0))],
        scratch_shapes=[pltpu.VMEM((seq, 2 * HEAD_DIM), BF16),
                        pltpu.VMEM((seq, 2 * HEAD_DIM), BF16),
                        pltpu.VMEM((seq // ATTN_KEYS, ATTN_TILE, ATTN_KEYS), F32)],
    )
    return pl.pallas_call(
        functools.partial(_moba_prompt_kernel, nb=nb, n_side=n_side),
        out_shape=[jax.ShapeDtypeStruct((batch, seq, ATTN_WIDTH), F32),
                   jax.ShapeDtypeStruct((side_batch, n_pages // PAGES_PER_BLOCK, N_HEADS, HEAD_DIM), F32)],
        grid_spec=grid_spec,
        compiler_params=_params(("arbitrary", "arbitrary", "arbitrary"), 56),
        name="moba_prompt",
    )(slopes, page_table_flat, q, pen, k, v, *([pool_k] * n_side))


def _glu_proj_kernel(x_ref, gpm_ref, sc_ref, sh_ref, wa_ref, wg_ref, hs_ref, h_ref, u_ref, us_ref, wbf_ref):
    width = wa_ref.shape[1]

    def glu(rows):
        ag = jnp.dot(rows, wbf_ref[...], preferred_element_type=F32)
        return ag[:, 0:width] * jax.nn.sigmoid(ag[:, width:2 * width])

    @pl.when(pl.program_id(0) == 0)
    def _():
        wbf_ref[:, 0:width] = wa_ref[...].astype(BF16)
        wbf_ref[:, width:2 * width] = wg_ref[...].astype(BF16)
        us_ref[...] = glu(hs_ref[...])

    sc, sh = _mod2d(sc_ref), _mod2d(sh_ref)
    tm = x_ref.shape[0]
    chunk = tm // 2
    for r0 in range(0, tm, chunk):
        rs = slice(r0, r0 + chunk)
        h = (_rms(x_ref[rs, :], gpm_ref[...]) * (1.0 + sc) + sh).astype(BF16)
        h_ref[rs, :] = h
        u_ref[rs, :] = glu(h)


def _glu_proj(x, g_pm, mod, seq, w_in, hs, a_col0, g_col0, width):
    tm = PROJ_TILE
    m, d = x.shape
    ms = hs.shape[0]
    sc, sc_spec = _mod_spec(mod, 1, tm, seq)
    sh, sh_spec = _mod_spec(mod, 0, tm, seq)
    return pl.pallas_call(
        _glu_proj_kernel,
        out_shape=[jax.ShapeDtypeStruct((m, d), BF16), jax.ShapeDtypeStruct((m, width), F32),
                   jax.ShapeDtypeStruct((ms, width), F32)],
        grid=(m // tm,),
        in_specs=[pl.BlockSpec((tm, d), lambda i: (i, 0)),
                  pl.BlockSpec((1, d), lambda i: (0, 0)), sc_spec, sh_spec,
                  _resident((d, width), lambda i: (0, a_col0 // width)),
                  _resident((d, width), lambda i: (0, g_col0 // width)),
                  _resident((ms, d), lambda i: (0, 0))],
        out_specs=[pl.BlockSpec((tm, d), lambda i: (i, 0)),
                   pl.BlockSpec((tm, width), lambda i: (i, 0)),
                   pl.BlockSpec((ms, width), lambda i: (0, 0))],
        scratch_shapes=[pltpu.VMEM((d, 2 * width), BF16)],
        compiler_params=_params(("arbitrary",), 56),
        name="glu_proj",
    )(x, g_pm, sc, sh, w_in, w_in, hs)


def _conv_tail(y, gln, bln, gco):
    mu = jnp.mean(y, axis=-1, keepdims=True)
    var = jnp.mean(jnp.square(y - mu), axis=-1, keepdims=True)
    z = _silu((y - mu) * lax.rsqrt(var + EPS) * gln + bln)
    return _rms(z, gco)


def _conv_tile(first, last, u_ref, halo_ref, w_ref, bdw_ref, gln_ref, bln_ref, gco_ref, cn_ref, buf_ref,
               ext_ref, z_ref, y_ref):
    tt = u_ref.shape[0]
    ext_ref[0:CONV_HALO, :] = jnp.where(first, 0.0, halo_ref[...])
    ext_ref[CONV_HALO:CONV_HALO + tt, :] = u_ref[...]
    width = u_ref.shape[1]
    ext_ref[CONV_HALO + tt:CONV_HALO + tt + SUBLANES, :] = jnp.zeros((SUBLANES, width), F32)
    lead = CONV_HALO - CONV_BUF
    rc = z_ref.shape[0] - SUBLANES
    zr = rc + SUBLANES
    for cb in range(width // 128):
        cs = slice(cb * 128, (cb + 1) * 128)
        for c0 in range(0, tt, rc):
            y = None
            for b in range(SUBLANES):
                taps = [k for k in range(CONV_K) if (lead + k) % SUBLANES == b]
                z = None
                for k in taps:
                    r0 = c0 + lead + k - b
                    term = w_ref[k:k + 1, cs] * ext_ref[r0:r0 + zr, cs]
                    z = term if z is None else z + term
                if b == 0:
                    y = z[0:rc, :]
                else:
                    z_ref[...] = z
                    y = y + z_ref[b:b + rc, :]
            y_ref[c0:c0 + rc, cs] = y + bdw_ref[:, cs]
    cn_ref[...] = _conv_tail(y_ref[...], gln_ref[...], bln_ref[...], gco_ref[...]).astype(BF16)

    @pl.when(last)
    def _():
        buf_ref[0] = ext_ref[CONV_HALO + tt - CONV_BUF:CONV_HALO + tt, :]


def _conv_sample_kernel(st_ref, u_ref, w_ref, bdw_ref, gln_ref, bln_ref, gco_ref, cn_ref, buf_ref,
                        ext_ref, *, rows):
    ext_ref[0:CONV_BUF, :] = st_ref[0]
    ext_ref[CONV_BUF:CONV_BUF + rows, :] = u_ref[0]
    acc = jnp.zeros((rows, u_ref.shape[2]), F32)
    for k in range(CONV_K):
        acc = acc + w_ref[k:k + 1, :] * ext_ref[k:k + rows, :]
    y = acc + bdw_ref[...]
    cn_ref[0] = _conv_tail(y, gln_ref[...], bln_ref[...], gco_ref[...]).astype(BF16)
    buf_ref[0] = ext_ref[rows:rows + CONV_BUF, :]


def _conv_sample(state, u, w_dw, b_dw, g_ln, b_ln, g_co):
    batch, rows, width = u.shape
    vec = pl.BlockSpec((1, width), lambda b: (0, 0))
    tok = pl.BlockSpec((1, rows, width), lambda b: (b, 0, 0))
    buf = pl.BlockSpec((1, CONV_BUF, width), lambda b: (b, 0, 0))
    return pl.pallas_call(
        functools.partial(_conv_sample_kernel, rows=rows),
        out_shape=[jax.ShapeDtypeStruct((batch, rows, width), BF16),
                   jax.ShapeDtypeStruct((batch, CONV_BUF, width), F32)],
        grid=(batch,),
        in_specs=[buf, tok, pl.BlockSpec((CONV_K, width), lambda b: (0, 0)), vec, vec, vec, vec],
        out_specs=[tok, buf],
        scratch_shapes=[pltpu.VMEM((CONV_BUF + rows + 6, width), F32)],
        compiler_params=_params(("parallel",), 32),
        name="conv_sample",
    )(state, u, w_dw, b_dw, g_ln, b_ln, g_co)


def _out_proj_kernel(o_ref, *refs, tiles_per_seq):
    n_conv = len(refs) - 10
    cn_refs = refs[:n_conv]
    w_ref, x_ref, gao_ref, gpm_ref, gpf_ref, gt_ref, sc_ref, sh_ref, x1_ref, h2_ref = refs[n_conv:]
    tm = o_ref.shape[0]
    gt, sc, sh = _mod2d(gt_ref), _mod2d(sc_ref), _mod2d(sh_ref)
    chunk = tm // 2 if tm % 512 == 0 else tm
    for r0 in range(0, tm, chunk):
        rs = slice(r0, r0 + chunk)
        rows = lambda v: v if v.shape[0] == 1 else v[rs]
        an = _rms(o_ref[rs, :], gao_ref[...]).astype(BF16)
        cn = cn_refs[0][rs, :]
        for s in range(1, n_conv):
            cn = jnp.where(pl.program_id(0) >= s * tiles_per_seq, cn_refs[s][rs, :], cn)
        mix = jnp.concatenate([an, cn], axis=1)
        merged = jnp.dot(mix, w_ref[...], preferred_element_type=F32)
        x1 = x_ref[rs, :] + rows(gt) * _rms(merged, gpm_ref[...])
        x1_ref[rs, :] = x1
        h2_ref[rs, :] = (_rms(x1, gpf_ref[...]) * (1.0 + rows(sc)) + rows(sh)).astype(BF16)


def _out_proj(o_attn, conv_n, w_bf, x, g_ao, g_pm, g_pf, mod, tm, rows_per_group):
    m, d = x.shape
    aw = o_attn.shape[1]
    cw = conv_n[0].shape[1]
    tiles_per_seq = conv_n[0].shape[0] // tm
    assert tiles_per_seq * tm == conv_n[0].shape[0] and tiles_per_seq * len(conv_n) == m // tm
    gt, gt_spec = _mod_spec(mod, 2, tm, rows_per_group)
    sc, sc_spec = _mod_spec(mod, 4, tm, rows_per_group)
    sh, sh_spec = _mod_spec(mod, 3, tm, rows_per_group)
    row = lambda w: pl.BlockSpec((tm, w), lambda i: (i, 0))
    vec = lambda w: pl.BlockSpec((1, w), lambda i: (0, 0))

    def conv_spec(s):
        return pl.BlockSpec((tm, cw), lambda i: (jnp.clip(i - s * tiles_per_seq, 0, tiles_per_seq - 1), 0))

    return pl.pallas_call(
        functools.partial(_out_proj_kernel, tiles_per_seq=tiles_per_seq),
        out_shape=[jax.ShapeDtypeStruct((m, d), F32), jax.ShapeDtypeStruct((m, d), BF16)],
        grid=(m // tm,),
        in_specs=[row(aw)] + [conv_spec(s) for s in range(len(conv_n))]
                 + [_resident((aw + cw, d), lambda i: (0, 0)), row(d),
                    vec(aw), vec(d), vec(d), gt_spec, sc_spec, sh_spec],
        out_specs=[row(d), row(d)],
        compiler_params=_params(("parallel",), 56),
        name="out_proj",
    )(o_attn, *conv_n, w_bf, x, g_ao, g_pm, g_pf, gt, sc, sh)


def _ffn_up_kernel(h_ref, wg_ref, wu_ref, hs_ref, wd_ref, a_ref, as_ref, wdbf_ref, wbf_ref):
    tn = wg_ref.shape[1]

    def swiglu(rows):
        gu = jnp.dot(rows, wbf_ref[...], preferred_element_type=F32)
        return (_silu(gu[:, 0:tn]) * gu[:, tn:2 * tn]).astype(BF16)

    @pl.when(pl.program_id(1) == 0)
    def _():
        wbf_ref[:, 0:tn] = wg_ref[...].astype(BF16)
        wbf_ref[:, tn:2 * tn] = wu_ref[...].astype(BF16)
        as_ref[...] = swiglu(hs_ref[...])

    a_ref[...] = swiglu(h_ref[...])
    wdbf_ref[...] = wd_ref[...].astype(BF16)


def _ffn_up(h, w_gate_up, hs, w_down):
    tm, tn = 1024, 512
    m, d = h.shape
    ms = hs.shape[0]
    d_ff = w_gate_up.shape[1] // 2
    n_blocks, m_tiles = d_ff // tn, m // tm
    slab = w_down.shape[0] // (n_blocks * m_tiles)
    return pl.pallas_call(
        _ffn_up_kernel,
        out_shape=[jax.ShapeDtypeStruct((m, d_ff), BF16), jax.ShapeDtypeStruct((ms, d_ff), BF16),
                   jax.ShapeDtypeStruct(w_down.shape, BF16)],
        grid=(n_blocks, m_tiles),
        in_specs=[pl.BlockSpec((tm, d), lambda n, i: (i, 0)),
                  pl.BlockSpec((d, tn), lambda n, i: (0, n)),
                  pl.BlockSpec((d, tn), lambda n, i: (0, n_blocks + n)),
                  _resident((ms, d), lambda n, i: (0, 0)),
                  pl.BlockSpec((slab, w_down.shape[1]), lambda n, i: (n * m_tiles + i, 0))],
        out_specs=[pl.BlockSpec((tm, tn), lambda n, i: (i, n)),
                   pl.BlockSpec((ms, tn), lambda n, i: (0, n)),
                   pl.BlockSpec((slab, w_down.shape[1]), lambda n, i: (n * m_tiles + i, 0))],
        scratch_shapes=[pltpu.VMEM((d, 2 * tn), BF16)],
        compiler_params=_params(("arbitrary", "arbitrary"), 56),
        name="ffn_up",
    )(h, w_gate_up, w_gate_up, hs, w_down)


def _ffn_down_kernel(a_ref, w_ref, x1_ref, g_ref, gt_ref, y_ref):
    z = jnp.dot(a_ref[...], w_ref[...], preferred_element_type=F32)
    y_ref[...] = x1_ref[...] + _mod2d(gt_ref) * _rms(z, g_ref[...])


def _ffn_down(act, w_bf, x1, g_post, mod, tm, rows_per_group):
    m, d_ff = act.shape
    d = w_bf.shape[1]
    gt, gt_spec = _mod_spec(mod, 5, tm, rows_per_group)
    return pl.pallas_call(
        _ffn_down_kernel,
        out_shape=jax.ShapeDtypeStruct((m, d), F32),
        grid=(m // tm,),
        in_specs=[pl.BlockSpec((tm, d_ff), lambda i: (i, 0)),
                  _resident((d_ff, d), lambda i: (0, 0)),
                  pl.BlockSpec((tm, d), lambda i: (i, 0)),
                  pl.BlockSpec((1, d), lambda i: (0, 0)), gt_spec],
        out_specs=pl.BlockSpec((tm, d), lambda i: (i, 0)),
        compiler_params=_params(("parallel",), 48),
        name="ffn_down",
    )(act, w_bf, x1, g_post, gt)


SELECT_ROWS = 16


def _select_sample_kernel(pt_ref, q_ref, km_ref, idx_ref, *, n_q, n_pages):
    b = pl.program_id(0)
    nb = km_ref.shape[2]
    lane = lax.broadcasted_iota(jnp.int32, (nb, 128), 1)
    rowi = lax.broadcasted_iota(jnp.int32, (nb, 128), 0)
    s = jnp.full((nb, 128), NEG, F32)
    for hh in range(N_HEADS):
        km = km_ref[0, hh]
        for t in range(n_q):
            col = jnp.sum(km * q_ref[0, hh, t:t + 1, :], axis=1, keepdims=True)
            s = jnp.where(lane == hh * n_q + t, col, s)
    out_row = lax.broadcasted_iota(jnp.int32, (SELECT_ROWS, 128), 0)
    out = jnp.zeros((SELECT_ROWS, 128), jnp.int32)
    for r in range(MOBA_TOPK):
        top = jnp.max(s, axis=0, keepdims=True)
        arg = jnp.min(jnp.where(s == top, rowi, nb), axis=0, keepdims=True)
        out = jnp.where(out_row == r, arg, out)
        s = jnp.where(rowi == arg, -jnp.inf, s)
        for pp in range(PAGES_PER_BLOCK):
            page = jnp.zeros((1, 128), jnp.int32)
            for j in range(nb):
                page = jnp.where(arg == j, pt_ref[b * n_pages + j * PAGES_PER_BLOCK + pp], page)
            out = jnp.where(out_row == 8 + r * PAGES_PER_BLOCK + pp, page, out)
    idx_ref[0] = out


def _select_sample(page_table_flat, q, kmean, n_pages):
    batch, _, n_q, _ = q.shape
    nb = kmean.shape[2]
    grid_spec = pltpu.PrefetchScalarGridSpec(
        num_scalar_prefetch=1,
        grid=(batch,),
        in_specs=[pl.BlockSpec((1, N_HEADS, n_q, HEAD_DIM), lambda b, pt: (b, 0, 0, 0)),
                  pl.BlockSpec((1, N_HEADS, nb, HEAD_DIM), lambda b, pt: (b, 0, 0, 0))],
        out_specs=pl.BlockSpec((1, SELECT_ROWS, 128), lambda b, pt: (b, 0, 0)),
    )
    return pl.pallas_call(
        functools.partial(_select_sample_kernel, n_q=n_q, n_pages=n_pages),
        out_shape=jax.ShapeDtypeStruct((batch, SELECT_ROWS, 128), jnp.int32),
        grid_spec=grid_spec,
        compiler_params=_params(("arbitrary",), 32),
        name="select_sample",
    )(page_table_flat, q, kmean)


def _moba_sample_kernel(page_ref, idx_ref, slopes_ref, q_ref, kn_ref, vn_ref, *refs, n_q, past):
    n_sel = n_q * MOBA_TOPK * PAGES_PER_BLOCK
    k_refs, v_refs, o_ref = refs[:n_sel], refs[n_sel:2 * n_sel], refs[2 * n_sel]
    b = pl.program_id(0)
    h = pl.program_id(1)
    slope = slopes_ref[h]
    key = lax.broadcasted_iota(jnp.int32, (PAGE, 1), 0)
    for t in range(n_q):
        q = q_ref[0, 0, t:t + 1, :] * (HEAD_DIM ** -0.5)
        scores = []
        for s in range(MOBA_TOPK):
            blk = idx_ref[((b * N_HEADS + h) * n_q + t) * MOBA_TOPK + s]
            for pp in range(PAGES_PER_BLOCK):
                kp = k_refs[(t * MOBA_TOPK + s) * PAGES_PER_BLOCK + pp][0, 0]
                dist = (past + t - blk * MOBA_BLOCK - pp * PAGE - key).astype(F32)
                scores.append(jnp.sum(kp * q, axis=1, keepdims=True) - slope * dist)
        own = []
        for t2 in range(t + 1):
            own.append(jnp.sum(kn_ref[0, 0, t2:t2 + 1, :] * q, axis=1, keepdims=True) - slope * float(t - t2))
        m = own[0]
        for sc in scores:
            m = jnp.maximum(m, jnp.max(sc, axis=0, keepdims=True))
        for sc in own[1:]:
            m = jnp.maximum(m, sc)
        l = jnp.zeros((1, 1), F32)
        acc = jnp.zeros((1, HEAD_DIM), F32)
        for n, sc in enumerate(scores):
            p = jnp.exp(sc - m)
            l = l + jnp.sum(p, axis=0, keepdims=True)
            acc = acc + jnp.sum(p * v_refs[n + t * MOBA_TOPK * PAGES_PER_BLOCK][0, 0], axis=0, keepdims=True)
        for t2, sc in enumerate(own):
            p = jnp.exp(sc - m)
            l = l + p
            acc = acc + p * vn_ref[0, 0, t2:t2 + 1, :]
        o_ref[0, 0, t:t + 1, :] = acc / l


def _moba_sample(pages_flat, idx_flat, slopes, q, k_new, v_new, pool_k, pool_v, n_pages):
    batch, _, n_q, _ = q.shape
    past = n_pages * PAGE
    per_query = MOBA_TOPK * PAGES_PER_BLOCK

    def sel_spec(n):
        return pl.BlockSpec((1, 1, PAGE, HEAD_DIM),
                            lambda b, h, pg, idx, sl: (pg[(b * N_HEADS + h) * n_q * per_query + n], h, 0, 0))

    sel_specs = [sel_spec(n) for n in range(n_q * per_query)]
    tok = pl.BlockSpec((1, 1, n_q, HEAD_DIM), lambda b, h, pg, idx, sl: (b, h, 0, 0))
    grid_spec = pltpu.PrefetchScalarGridSpec(
        num_scalar_prefetch=3,
        grid=(batch, N_HEADS),
        in_specs=[tok, tok, tok] + sel_specs + sel_specs,
        out_specs=tok,
    )
    n_sel = len(sel_specs)
    return pl.pallas_call(
        functools.partial(_moba_sample_kernel, n_q=n_q, past=past),
        out_shape=jax.ShapeDtypeStruct((batch, N_HEADS, n_q, HEAD_DIM), F32),
        grid_spec=grid_spec,
        compiler_params=_params(("parallel", "parallel"), 32),
        name="moba_sample",
    )(pages_flat, idx_flat, slopes, q, k_new, v_new, *([pool_k] * n_sel), *([pool_v] * n_sel))


def kernel(x_prompt, x_sample, cache_k, cache_v, state_conv, page_table, c_prompt, c_sample, w_ada, b_ada, g_pre_mix, w_in, w_dw, b_dw, g_conv_ln, b_conv_ln, g_attn_out, g_conv_out, w_out, g_post_mix, g_pre_ffn, w_gate_up, w_down, g_post_ffn):
    depth = w_ada.shape[0]
    assert depth == 1, "single layer: the prompt and sample residual streams are not chained across layers here"
    batch, seq, d = x_prompt.shape
    dec_batch, dec_seq, _ = x_sample.shape
    n_pages = page_table.shape[1]
    past = n_pages * PAGE
    conv_w = w_dw.shape[-1]
    assert past % MOBA_BLOCK == 0 and dec_seq <= MOBA_BLOCK and past // MOBA_BLOCK >= MOBA_TOPK
    assert seq % ATTN_TILE == 0 and d == ATTN_WIDTH + conv_w

    slopes = 2.0 ** (-8.0 * (jnp.arange(N_HEADS, dtype=F32) + 1.0) / N_HEADS)
    l = 0
    vec = lambda a: a[l].reshape(1, -1)
    g_pm, g_ao, g_co = vec(g_pre_mix), vec(g_attn_out), vec(g_conv_out)
    g_post, g_pf, g_pffn = vec(g_post_mix), vec(g_pre_ffn), vec(g_post_ffn)
    bdw, gln, bln = vec(b_dw), vec(g_conv_ln), vec(b_conv_ln)
    wdw = w_dw[l].reshape(CONV_K, conv_w)

    n_c = batch + dec_batch
    c_rows = -(-n_c // 8) * 8
    c_all = jnp.concatenate([c_prompt, c_sample, jnp.zeros((c_rows - n_c, d), F32)], axis=0)
    mod = _ada(c_all, w_ada[l], vec(b_ada))
    mod_p = mod.reshape(c_rows * 6, 1, d)
    mod_rows = jnp.repeat(mod[batch:n_c].reshape(dec_batch, 6, d), dec_seq, axis=0)
    mod_s = [mod_rows[:, w] for w in range(6)]

    m_s = dec_batch * dec_seq
    xp = x_prompt.reshape(batch * seq, d)
    xs = x_sample.reshape(m_s, d)
    hs = _modnorm(xs, g_pm, mod_s, 1, 0, m_s, None)

    assert batch == 2, "the conv side jobs are mapped one prompt sequence per K/V projection call"
    h, u, u_s = _glu_proj(xp, g_pm, mod_p, seq, w_in[l], hs, 3 * ATTN_WIDTH, 3 * ATTN_WIDTH + conv_w, conv_w)
    conv_params = (wdw, bdw, gln, bln, g_co)
    k_pages, k_heads, k_rows_s, kmean, w_out_bf, conv_n0, conv_buf0 = _kv_proj(
        h, w_in[l], 1, hs, batch, seq, True, 0, u, conv_params, w_out[l])
    v_pages, v_heads, v_rows_s, conv_n1, conv_buf1 = _kv_proj(
        h, w_in[l], 2, hs, batch, seq, False, 1, u, conv_params)
    conv_buf_p = jnp.concatenate([conv_buf0, conv_buf1], axis=0)
    q_heads, pen, q_rows_s = _q_proj(h, w_in[l], hs, kmean.reshape(batch, seq // MOBA_BLOCK, ATTN_WIDTH),
                                     batch, seq)

    pt_flat = page_table.reshape(-1)
    o_attn, kmean_s = _moba_prompt(slopes, pt_flat, q_heads, pen, k_heads, v_heads, cache_k[l], dec_batch, n_pages)
    o_attn = o_attn.reshape(batch * seq, ATTN_WIDTH)
    x1, h2 = _out_proj(o_attn, [conv_n0, conv_n1], w_out_bf, xp, g_ao, g_post, g_pf, mod_p, 512, seq)

    to_heads = lambda t: t.reshape(dec_batch, dec_seq, N_HEADS, HEAD_DIM).transpose(0, 2, 1, 3)
    q_s, k_s, v_s = to_heads(q_rows_s), to_heads(k_rows_s), to_heads(v_rows_s)
    kmean_s = kmean_s.transpose(0, 2, 1, 3)
    sel = _select_sample(pt_flat, q_s, kmean_s, n_pages)[:, :, :N_HEADS * dec_seq]
    per_query = lambda rows: rows.reshape(dec_batch, -1, N_HEADS, dec_seq).transpose(0, 2, 3, 1).reshape(-1)
    idx_flat = per_query(sel[:, :MOBA_TOPK])
    pages_flat = per_query(sel[:, 8:8 + MOBA_TOPK * PAGES_PER_BLOCK])
    o_s = _moba_sample(pages_flat, idx_flat, slopes, q_s, k_s, v_s, cache_k[l], cache_v[l], n_pages)
    o_attn_s = o_s.transpose(0, 2, 1, 3).reshape(m_s, ATTN_WIDTH)
    conv_n_s, conv_buf_s = _conv_sample(state_conv[l], u_s.reshape(dec_batch, dec_seq, conv_w),
                                        wdw, bdw, gln, bln, g_co)
    x1_s, h2_s = _out_proj(o_attn_s, [conv_n_s.reshape(m_s, conv_w)], w_out_bf, xs, g_ao, g_post, g_pf,
                           mod_s, m_s, None)

    act, act_s, w_down_bf = _ffn_up(h2, w_gate_up[l], h2_s, w_down[l])
    y_p = _ffn_down(act, w_down_bf, x1, g_pffn, mod_p, 256, seq).reshape(batch, seq, d)
    y_s = _ffn_down(act_s, w_down_bf, x1_s, g_pffn, mod_s, m_s, None).reshape(dec_batch, dec_seq, d)

    return (y_p, y_s, k_pages[None], v_pages[None], conv_buf_p[None],
            k_s[None], v_s[None], conv_buf_s[None])
```

```python
import functools

import jax
import jax.numpy as jnp
from jax import lax
from jax.experimental import pallas as pl
from jax.experimental.pallas import tpu as pltpu

F32 = jnp.float32
BF16 = jnp.bfloat16

N_HEADS = 8
HEAD_DIM = 128
ATTN_WIDTH = N_HEADS * HEAD_DIM
CONV_K = 31
CONV_BUF = CONV_K - 1
CONV_HALO = 32
CONV_CHUNK = 64
SUBLANES = 8
PAGE = 128
MOBA_BLOCK = 256
MOBA_TOPK = 3
PAGES_PER_BLOCK = MOBA_BLOCK // PAGE
ATTN_TILE = 4 * MOBA_BLOCK
PROJ_TILE = 512
ATTN_KEYS = 4 * MOBA_BLOCK
LOG2E = 1.4426950408889634
EPS = 1e-6
NEG = -1e30
MIB = 1024 * 1024

_NT = (((1,), (1,)), ((), ()))


def _params(sem, vmem_mib, flags=None):
    return pltpu.CompilerParams(dimension_semantics=sem, vmem_limit_bytes=vmem_mib * MIB, flags=flags)


def _resident(shape, index_map):
    return pl.BlockSpec(shape, index_map, pipeline_mode=pl.Buffered(1))


def _rms(x, g):
    return x * lax.rsqrt(jnp.mean(x * x, axis=-1, keepdims=True) + EPS) * g


def _silu(x):
    return x * jax.nn.sigmoid(x)


def _mod2d(ref):
    v = ref[...]
    return v.reshape(v.shape[-2], v.shape[-1])


def _mod_spec(mod, which, tm, rows_per_group):
    if rows_per_group is None:
        return mod[which], pl.BlockSpec((tm, mod[which].shape[1]), lambda i, *_: (i, 0))
    tiles = rows_per_group // tm
    return mod, pl.BlockSpec((1, 1, mod.shape[2]), lambda i, *_: ((i // tiles) * 6 + which, 0, 0))


def _ada_kernel(c_ref, w_ref, b_ref, o_ref):
    s = _silu(c_ref[...]).astype(BF16)
    o_ref[...] = jnp.dot(s, w_ref[...].astype(BF16), preferred_element_type=F32) + b_ref[...]


def _ada(c_all, w, b):
    rows, d = c_all.shape
    n = w.shape[1]
    tn = 1024
    return pl.pallas_call(
        _ada_kernel,
        out_shape=jax.ShapeDtypeStruct((rows, n), F32),
        grid=(n // tn,),
        in_specs=[pl.BlockSpec((rows, d), lambda j: (0, 0)),
                  pl.BlockSpec((d, tn), lambda j: (0, j)),
                  pl.BlockSpec((1, tn), lambda j: (0, j))],
        out_specs=pl.BlockSpec((rows, tn), lambda j: (0, j)),
        compiler_params=_params(("arbitrary",), 40),
        name="ada_mod",
    )(c_all, w, b)


def _modnorm_kernel(x_ref, g_ref, sc_ref, sh_ref, o_ref):
    h = _rms(x_ref[...], g_ref[...]) * (1.0 + _mod2d(sc_ref)) + _mod2d(sh_ref)
    o_ref[...] = h.astype(BF16)


def _modnorm(x, g, mod, which_sc, which_sh, tm, rows_per_group):
    m, d = x.shape
    sc, sc_spec = _mod_spec(mod, which_sc, tm, rows_per_group)
    sh, sh_spec = _mod_spec(mod, which_sh, tm, rows_per_group)
    return pl.pallas_call(
        _modnorm_kernel,
        out_shape=jax.ShapeDtypeStruct((m, d), BF16),
        grid=(m // tm,),
        in_specs=[pl.BlockSpec((tm, d), lambda i: (i, 0)),
                  pl.BlockSpec((1, d), lambda i: (0, 0)), sc_spec, sh_spec],
        out_specs=pl.BlockSpec((tm, d), lambda i: (i, 0)),
        compiler_params=_params(("parallel",), 32),
        name="modnorm",
    )(x, g, sc, sh)


def _kv_proj_kernel(*refs, tm, with_mean, with_cast):
    it = iter(refs)
    h_ref, w_ref, hs_ref = next(it), next(it), next(it)
    cast_ref = next(it) if with_cast else None
    conv_in = [next(it) for _ in range(7)]
    pages_ref, heads_ref, s_ref = next(it), next(it), next(it)
    mean_ref = next(it) if with_mean else None
    cast_out_ref = next(it) if with_cast else None
    conv_out = [next(it), next(it)]
    wbf_ref = next(it)
    conv_scratch = [next(it), next(it), next(it)]
    i = pl.program_id(0)

    @pl.when(i == 0)
    def _():
        wbf_ref[...] = w_ref[...].astype(BF16)
        s_ref[...] = jnp.dot(hs_ref[...], wbf_ref[...], preferred_element_type=F32)

    for pair in range(N_HEADS // 2):
        ps = slice(pair * 2 * HEAD_DIM, (pair + 1) * 2 * HEAD_DIM)
        acc = jnp.dot(h_ref[...], wbf_ref[:, ps], preferred_element_type=F32)
        for sub in range(2):
            hh = pair * 2 + sub
            cs = slice(sub * HEAD_DIM, (sub + 1) * HEAD_DIM)
            heads_ref[0, hh] = acc[:, cs].astype(BF16)
            for p in range(tm // PAGE):
                pages_ref[0, p, hh] = acc[p * PAGE:(p + 1) * PAGE, cs]
        if with_mean:
            for mb in range(tm // MOBA_BLOCK):
                mean_ref[mb, :, ps] = jnp.mean(acc[mb * MOBA_BLOCK:(mb + 1) * MOBA_BLOCK, :], axis=0, keepdims=True)
    if with_cast:
        cast_out_ref[...] = cast_ref[...].astype(BF16)
    _conv_tile(i == 0, i == pl.num_programs(0) - 1, *conv_in, *conv_out, *conv_scratch)


def _kv_proj(h, w_in, col_block, hs, batch, seq, with_mean, conv_seq, u, conv_params, cast_src=None):
    tm = PROJ_TILE
    m, d = h.shape
    ms = hs.shape[0]
    tiles = seq // tm
    steps = m // tm
    width = u.shape[1]
    tt = seq // steps
    assert tt * steps == seq and tt % CONV_HALO == 0 and tt >= CONV_BUF
    halo_per_tile = tt // CONV_HALO
    in_specs = [pl.BlockSpec((tm, d), lambda i: (i, 0)),
                _resident((d, ATTN_WIDTH), lambda i: (0, col_block)),
                _resident((ms, d), lambda i: (0, 0))]
    args = [h, w_in, hs]
    out_shape = [jax.ShapeDtypeStruct((batch, seq // PAGE, N_HEADS, PAGE, HEAD_DIM), F32),
                 jax.ShapeDtypeStruct((batch, N_HEADS, seq, HEAD_DIM), BF16),
                 jax.ShapeDtypeStruct((ms, ATTN_WIDTH), F32)]
    out_specs = [pl.BlockSpec((1, tm // PAGE, N_HEADS, PAGE, HEAD_DIM), lambda i: (i // tiles, i % tiles, 0, 0, 0)),
                 pl.BlockSpec((1, N_HEADS, tm, HEAD_DIM), lambda i: (i // tiles, 0, i % tiles, 0)),
                 pl.BlockSpec((ms, ATTN_WIDTH), lambda i: (0, 0))]
    if with_mean:
        out_shape.append(jax.ShapeDtypeStruct((m // MOBA_BLOCK, 1, ATTN_WIDTH), F32))
        out_specs.append(pl.BlockSpec((tm // MOBA_BLOCK, 1, ATTN_WIDTH), lambda i: (i, 0, 0)))
    if cast_src is not None:
        rows, cols = cast_src.shape
        slab = rows // steps
        in_specs.append(pl.BlockSpec((slab, cols), lambda i: (i, 0)))
        args.append(cast_src)
        out_shape.append(jax.ShapeDtypeStruct((rows, cols), BF16))
        out_specs.append(pl.BlockSpec((slab, cols), lambda i: (i, 0)))
    vec = pl.BlockSpec((1, width), lambda i: (0, 0))
    in_specs += [pl.BlockSpec((tt, width), lambda i: (conv_seq * steps + i, 0)),
                 pl.BlockSpec((CONV_HALO, width),
                              lambda i: (jnp.maximum((conv_seq * steps + i) * halo_per_tile - 1, 0), 0)),
                 pl.BlockSpec((CONV_K, width), lambda i: (0, 0)), vec, vec, vec, vec]
    args += [u, u, *conv_params]
    out_shape += [jax.ShapeDtypeStruct((seq, width), BF16), jax.ShapeDtypeStruct((1, CONV_BUF, width), F32)]
    out_specs += [pl.BlockSpec((tt, width), lambda i: (i, 0)),
                  pl.BlockSpec((1, CONV_BUF, width), lambda i: (0, 0, 0))]
    return pl.pallas_call(
        functools.partial(_kv_proj_kernel, tm=tm, with_mean=with_mean, with_cast=cast_src is not None),
        out_shape=out_shape,
        grid=(steps,),
        in_specs=in_specs,
        out_specs=out_specs,
        scratch_shapes=[pltpu.VMEM((d, ATTN_WIDTH), BF16),
                        pltpu.VMEM((CONV_HALO + tt + SUBLANES, width), F32),
                        pltpu.VMEM((CONV_CHUNK + SUBLANES, 128), F32),
                        pltpu.VMEM((tt, width), F32)],
        compiler_params=_params(("arbitrary",), 56),
        name="kv_proj",
    )(*args)


def _q_proj_kernel(h_ref, w_ref, hs_ref, km_ref, q_ref, pen_ref, qs_ref, wbf_ref, sm_ref, *, tm, tiles):
    i = pl.program_id(0)

    @pl.when(i == 0)
    def _():
        wbf_ref[...] = w_ref[...].astype(BF16)
        qs_ref[...] = jnp.dot(hs_ref[...], wbf_ref[...], preferred_element_type=F32)

    acc = jnp.dot(h_ref[...], wbf_ref[...], preferred_element_type=F32)
    km = km_ref[0]
    nb = km.shape[0]
    jrow = lax.broadcasted_iota(jnp.int32, (nb, tm), 0)
    tcol = lax.broadcasted_iota(jnp.int32, (nb, tm), 1)
    n_full = (i % tiles) * (tm // MOBA_BLOCK) + lax.shift_right_logical(tcol, MOBA_BLOCK.bit_length() - 1)
    valid = jrow < n_full
    own = jrow == n_full
    pens = []
    for hh in range(N_HEADS):
        cs = slice(hh * HEAD_DIM, (hh + 1) * HEAD_DIM)
        qh = acc[:, cs]
        q_ref[0, hh] = (qh * (HEAD_DIM ** -0.5 * LOG2E)).astype(BF16)
        s = lax.dot_general(km[:, cs], qh, _NT, precision=lax.Precision.HIGHEST, preferred_element_type=F32)
        sm = jnp.where(valid, s, NEG)
        sm_ref[...] = sm
        rank = jnp.zeros((nb, tm), jnp.int32)
        for jp in range(nb):
            row = sm_ref[jp:jp + 1, :]
            gt = jnp.where(row > sm, 1, 0)
            ge = jnp.where(row >= sm, 1, 0)
            rank = rank + jnp.where(jrow > jp, ge, gt)
        keep = (valid & (rank < MOBA_TOPK)) | own
        pens.append(jnp.where(keep, 0.0, NEG))
    pen = jnp.concatenate(pens, axis=0)
    pen_ref[0] = pen.T.astype(BF16)


def _q_proj(h, w_in, hs, kmean, batch, seq):
    tm = PROJ_TILE
    m, d = h.shape
    ms = hs.shape[0]
    tiles = seq // tm
    nb = kmean.shape[1]
    return pl.pallas_call(
        functools.partial(_q_proj_kernel, tm=tm, tiles=tiles),
        out_shape=[jax.ShapeDtypeStruct((batch, N_HEADS, seq, HEAD_DIM), BF16),
                   jax.ShapeDtypeStruct((batch, seq, N_HEADS * nb), BF16),
                   jax.ShapeDtypeStruct((ms, ATTN_WIDTH), F32)],
        grid=(m // tm,),
        in_specs=[pl.BlockSpec((tm, d), lambda i: (i, 0)),
                  _resident((d, ATTN_WIDTH), lambda i: (0, 0)),
                  _resident((ms, d), lambda i: (0, 0)),
                  pl.BlockSpec((1, nb, ATTN_WIDTH), lambda i: (i // tiles, 0, 0))],
        out_specs=[pl.BlockSpec((1, N_HEADS, tm, HEAD_DIM), lambda i: (i // tiles, 0, i % tiles, 0)),
                   pl.BlockSpec((1, tm, N_HEADS * nb), lambda i: (i // tiles, i % tiles, 0)),
                   pl.BlockSpec((ms, ATTN_WIDTH), lambda i: (0, 0))],
        scratch_shapes=[pltpu.VMEM((d, ATTN_WIDTH), BF16), pltpu.VMEM((nb, tm), F32)],
        compiler_params=_params(("arbitrary",), 48),
        name="q_proj",
    )(h, w_in, hs, kmean)


def _fold_lanes(x, op):
    out = x[:, 0:128]
    for c in range(1, x.shape[1] // 128):
        out = op(out, x[:, c * 128:(c + 1) * 128])
    return out


def _block_key_means(page_refs, o_ref):
    for blk in range(len(page_refs) // PAGES_PER_BLOCK):
        tot = jnp.zeros((N_HEADS, HEAD_DIM), F32)
        for pp in range(PAGES_PER_BLOCK):
            tot = tot + jnp.sum(page_refs[blk * PAGES_PER_BLOCK + pp][0], axis=1)
        o_ref[0, blk] = tot * (1.0 / MOBA_BLOCK)


def _moba_prompt_kernel(slopes_ref, pt_ref, q_ref, pen_ref, k_ref, v_ref, *refs, nb, n_side):
    page_refs = refs[:n_side]
    o_ref, km_ref, ke_ref, ve_ref, s_ref = refs[n_side:]
    _block_key_means(page_refs, km_ref)

    tq, tk = ATTN_TILE, ATTN_KEYS
    h = pl.program_id(1)
    g = pl.program_id(2)
    slope = slopes_ref[h] * LOG2E
    seq = k_ref.shape[2]

    @pl.when(g == 0)
    def _():
        blk = lax.shift_right_logical(lax.broadcasted_iota(jnp.int32, (seq, HEAD_DIM), 0),
                                      MOBA_BLOCK.bit_length() - 1)
        lane = lax.broadcasted_iota(jnp.int32, (seq, HEAD_DIM), 1)
        ke_ref[:, 0:HEAD_DIM] = k_ref[0, 0]
        ke_ref[:, HEAD_DIM:2 * HEAD_DIM] = jnp.where(lane == h * nb + blk, 1.0, 0.0).astype(BF16)
        ve_ref[:, 0:HEAD_DIM] = v_ref[0, 0]
        ve_ref[:, HEAD_DIM:2 * HEAD_DIM] = jnp.where(lane == 0, 1.0, 0.0).astype(BF16)

    qe = jnp.concatenate([q_ref[0, 0], pen_ref[0]], axis=1)
    key_lane = lax.broadcasted_iota(jnp.int32, (1, tk), 1)

    def scores(c):
        ke = ke_ref[pl.ds(pl.multiple_of(c * tk, tk), tk), :]
        s = lax.dot_general(qe, ke, _NT, preferred_element_type=F32)
        return s + slope * (c * tk + key_lane).astype(F32)

    last = lax.shift_right_logical(g * tq, tk.bit_length() - 1)
    qpos = g * tq + lax.broadcasted_iota(jnp.int32, (tq, tk), 0)
    kpos = last * tk + lax.broadcasted_iota(jnp.int32, (tq, tk), 1)
    s = jnp.where(kpos <= qpos, scores(last), NEG)
    s_ref[last] = s
    mx = _fold_lanes(s, jnp.maximum)

    def pass1(c, mx):
        s = scores(c)
        s_ref[c] = s
        return jnp.maximum(mx, _fold_lanes(s, jnp.maximum))

    mx = lax.fori_loop(0, last, pass1, mx)
    m = jnp.max(mx, axis=1, keepdims=True)

    def pass2(c, acc):
        p = jnp.exp2(s_ref[c] - m)
        ve = ve_ref[pl.ds(pl.multiple_of(c * tk, tk), tk), :]
        return acc + jnp.dot(p.astype(BF16), ve, preferred_element_type=F32)

    acc = lax.fori_loop(0, last + 1, pass2, jnp.zeros((tq, 2 * HEAD_DIM), F32))
    o_ref[0] = acc[:, 0:HEAD_DIM] / acc[:, HEAD_DIM:HEAD_DIM + 1]


def _moba_prompt(slopes, page_table_flat, q, pen, k, v, pool_k, side_batch, n_pages):
    batch, _, seq, _ = q.shape
    nb = seq // MOBA_BLOCK
    nt = seq // ATTN_TILE
    steps = batch * N_HEADS * nt
    n_side = side_batch * n_pages // steps
    assert n_side * steps == side_batch * n_pages and n_side % PAGES_PER_BLOCK == 0 and n_pages % n_side == 0
    steps_per_seq = n_pages // n_side

    def step(b, h, g):
        return (b * N_HEADS + h) * nt + g

    def page_spec(p):
        return pl.BlockSpec((1, N_HEADS, PAGE, HEAD_DIM),
                            lambda b, h, g, sl, pt: (pt[step(b, h, g) * n_side + p], 0, 0, 0))

    grid_spec = pltpu.PrefetchScalarGridSpec(
        num_scalar_prefetch=2,
        grid=(batch, N_HEADS, nt),
        in_specs=[pl.BlockSpec((1, 1, ATTN_TILE, HEAD_DIM), lambda b, h, g, sl, pt: (b, h, g, 0)),
                  pl.BlockSpec((1, ATTN_TILE, N_HEADS * nb), lambda b, h, g, sl, pt: (b, g, 0)),
                  pl.BlockSpec((1, 1, seq, HEAD_DIM), lambda b, h, g, sl, pt: (b, h, 0, 0)),
                  pl.BlockSpec((1, 1, seq, HEAD_DIM), lambda b, h, g, sl, pt: (b, h, 0, 0))]
                 + [page_spec(p) for p in range(n_side)],
        out_specs=[pl.BlockSpec((1, ATTN_TILE, HEAD_DIM), lambda b, h, g, sl, pt: (b, g, h)),
                   pl.BlockSpec((1, n_side // PAGES_PER_BLOCK, N_HEADS, HEAD_DIM),
                                lambda b, h, g, sl, pt: (step(b, h, g) // steps_per_seq,
                                                         step(b, h, g) % steps_per_seq, 0, 0))],
        scratch_shapes=[pltpu.VMEM((seq, 2 * HEAD_DIM), BF16),
                        pltpu.VMEM((seq, 2 * HEAD_DIM), BF16),
                        pltpu.VMEM((seq // ATTN_KEYS, ATTN_TILE, ATTN_KEYS), F32)],
    )
    return pl.pallas_call(
        functools.partial(_moba_prompt_kernel, nb=nb, n_side=n_side),
        out_shape=[jax.ShapeDtypeStruct((batch, seq, ATTN_WIDTH), F32),
                   jax.ShapeDtypeStruct((side_batch, n_pages // PAGES_PER_BLOCK, N_HEADS, HEAD_DIM), F32)],
        grid_spec=grid_spec,
        compiler_params=_params(("arbitrary", "arbitrary", "arbitrary"), 56),
        name="moba_prompt",
    )(slopes, page_table_flat, q, pen, k, v, *([pool_k] * n_side))


def _glu_proj_kernel(x_ref, gpm_ref, sc_ref, sh_ref, wa_ref, wg_ref, hs_ref, h_ref, u_ref, us_ref, wbf_ref):
    width = wa_ref.shape[1]

    def glu(rows):
        ag = jnp.dot(rows, wbf_ref[...], preferred_element_type=F32)
        return ag[:, 0:width] * jax.nn.sigmoid(ag[:, width:2 * width])

    @pl.when(pl.program_id(0) == 0)
    def _():
        wbf_ref[:, 0:width] = wa_ref[...].astype(BF16)
        wbf_ref[:, width:2 * width] = wg_ref[...].astype(BF16)
        us_ref[...] = glu(hs_ref[...])

    sc, sh = _mod2d(sc_ref), _mod2d(sh_ref)
    tm = x_ref.shape[0]
    chunk = tm // 2
    for r0 in range(0, tm, chunk):
        rs = slice(r0, r0 + chunk)
        h = (_rms(x_ref[rs, :], gpm_ref[...]) * (1.0 + sc) + sh).astype(BF16)
        h_ref[rs, :] = h
        u_ref[rs, :] = glu(h)


def _glu_proj(x, g_pm, mod, seq, w_in, hs, a_col0, g_col0, width):
    tm = PROJ_TILE
    m, d = x.shape
    ms = hs.shape[0]
    sc, sc_spec = _mod_spec(mod, 1, tm, seq)
    sh, sh_spec = _mod_spec(mod, 0, tm, seq)
    return pl.pallas_call(
        _glu_proj_kernel,
        out_shape=[jax.ShapeDtypeStruct((m, d), BF16), jax.ShapeDtypeStruct((m, width), F32),
                   jax.ShapeDtypeStruct((ms, width), F32)],
        grid=(m // tm,),
        in_specs=[pl.BlockSpec((tm, d), lambda i: (i, 0)),
                  pl.BlockSpec((1, d), lambda i: (0, 0)), sc_spec, sh_spec,
                  _resident((d, width), lambda i: (0, a_col0 // width)),
                  _resident((d, width), lambda i: (0, g_col0 // width)),
                  _resident((ms, d), lambda i: (0, 0))],
        out_specs=[pl.BlockSpec((tm, d), lambda i: (i, 0)),
                   pl.BlockSpec((tm, width), lambda i: (i, 0)),
                   pl.BlockSpec((ms, width), lambda i: (0, 0))],
        scratch_shapes=[pltpu.VMEM((d, 2 * width), BF16)],
        compiler_params=_params(("arbitrary",), 56),
        name="glu_proj",
    )(x, g_pm, sc, sh, w_in, w_in, hs)


def _conv_tail(y, gln, bln, gco):
    mu = jnp.mean(y, axis=-1, keepdims=True)
    var = jnp.mean(jnp.square(y - mu), axis=-1, keepdims=True)
    z = _silu((y - mu) * lax.rsqrt(var + EPS) * gln + bln)
    return _rms(z, gco)


def _conv_tile(first, last, u_ref, halo_ref, w_ref, bdw_ref, gln_ref, bln_ref, gco_ref, cn_ref, buf_ref,
               ext_ref, z_ref, y_ref):
    tt = u_ref.shape[0]
    ext_ref[0:CONV_HALO, :] = jnp.where(first, 0.0, halo_ref[...])
    ext_ref[CONV_HALO:CONV_HALO + tt, :] = u_ref[...]
    width = u_ref.shape[1]
    ext_ref[CONV_HALO + tt:CONV_HALO + tt + SUBLANES, :] = jnp.zeros((SUBLANES, width), F32)
    lead = CONV_HALO - CONV_BUF
    rc = z_ref.shape[0] - SUBLANES
    zr = rc + SUBLANES
    for cb in range(width // 128):
        cs = slice(cb * 128, (cb + 1) * 128)
        for c0 in range(0, tt, rc):
            y = None
            for b in range(SUBLANES):
                taps = [k for k in range(CONV_K) if (lead + k) % SUBLANES == b]
                z = None
                for k in taps:
                    r0 = c0 + lead + k - b
                    term = w_ref[k:k + 1, cs] * ext_ref[r0:r0 + zr, cs]
                    z = term if z is None else z + term
                if b == 0:
                    y = z[0:rc, :]
                else:
                    z_ref[...] = z
                    y = y + z_ref[b:b + rc, :]
            y_ref[c0:c0 + rc, cs] = y + bdw_ref[:, cs]
    cn_ref[...] = _conv_tail(y_ref[...], gln_ref[...], bln_ref[...], gco_ref[...]).astype(BF16)

    @pl.when(last)
    def _():
        buf_ref[0] = ext_ref[CONV_HALO + tt - CONV_BUF:CONV_HALO + tt, :]


def _conv_sample_kernel(st_ref, u_ref, w_ref, bdw_ref, gln_ref, bln_ref, gco_ref, cn_ref, buf_ref,
                        ext_ref, *, rows):
    ext_ref[0:CONV_BUF, :] = st_ref[0]
    ext_ref[CONV_BUF:CONV_BUF + rows, :] = u_ref[0]
    acc = jnp.zeros((rows, u_ref.shape[2]), F32)
    for k in range(CONV_K):
        acc = acc + w_ref[k:k + 1, :] * ext_ref[k:k + rows, :]
    y = acc + bdw_ref[...]
    cn_ref[0] = _conv_tail(y, gln_ref[...], bln_ref[...], gco_ref[...]).astype(BF16)
    buf_ref[0] = ext_ref[rows:rows + CONV_BUF, :]


def _conv_sample(state, u, w_dw, b_dw, g_ln, b_ln, g_co):
    batch, rows, width = u.shape
    vec = pl.BlockSpec((1, width), lambda b: (0, 0))
    tok = pl.BlockSpec((1, rows, width), lambda b: (b, 0, 0))
    buf = pl.BlockSpec((1, CONV_BUF, width), lambda b: (b, 0, 0))
    return pl.pallas_call(
        functools.partial(_conv_sample_kernel, rows=rows),
        out_shape=[jax.ShapeDtypeStruct((batch, rows, width), BF16),
                   jax.ShapeDtypeStruct((batch, CONV_BUF, width), F32)],
        grid=(batch,),
        in_specs=[buf, tok, pl.BlockSpec((CONV_K, width), lambda b: (0, 0)), vec, vec, vec, vec],
        out_specs=[tok, buf],
        scratch_shapes=[pltpu.VMEM((CONV_BUF + rows + 6, width), F32)],
        compiler_params=_params(("parallel",), 32),
        name="conv_sample",
    )(state, u, w_dw, b_dw, g_ln, b_ln, g_co)


def _out_proj_kernel(o_ref, *refs, tiles_per_seq):
    n_conv = len(refs) - 10
    cn_refs = refs[:n_conv]
    w_ref, x_ref, gao_ref, gpm_ref, gpf_ref, gt_ref, sc_ref, sh_ref, x1_ref, h2_ref = refs[n_conv:]
    tm = o_ref.shape[0]
    gt, sc, sh = _mod2d(gt_ref), _mod2d(sc_ref), _mod2d(sh_ref)
    chunk = tm // 2 if tm % 512 == 0 else tm
    for r0 in range(0, tm, chunk):
        rs = slice(r0, r0 + chunk)
        rows = lambda v: v if v.shape[0] == 1 else v[rs]
        an = _rms(o_ref[rs, :], gao_ref[...]).astype(BF16)
        cn = cn_refs[0][rs, :]
        for s in range(1, n_conv):
            cn = jnp.where(pl.program_id(0) >= s * tiles_per_seq, cn_refs[s][rs, :], cn)
        mix = jnp.concatenate([an, cn], axis=1)
        merged = jnp.dot(mix, w_ref[...], preferred_element_type=F32)
        x1 = x_ref[rs, :] + rows(gt) * _rms(merged, gpm_ref[...])
        x1_ref[rs, :] = x1
        h2_ref[rs, :] = (_rms(x1, gpf_ref[...]) * (1.0 + rows(sc)) + rows(sh)).astype(BF16)


def _out_proj(o_attn, conv_n, w_bf, x, g_ao, g_pm, g_pf, mod, tm, rows_per_group):
    m, d = x.shape
    aw = o_attn.shape[1]
    cw = conv_n[0].shape[1]
    tiles_per_seq = conv_n[0].shape[0] // tm
    assert tiles_per_seq * tm == conv_n[0].shape[0] and tiles_per_seq * len(conv_n) == m // tm
    gt, gt_spec = _mod_spec(mod, 2, tm, rows_per_group)
    sc, sc_spec = _mod_spec(mod, 4, tm, rows_per_group)
    sh, sh_spec = _mod_spec(mod, 3, tm, rows_per_group)
    row = lambda w: pl.BlockSpec((tm, w), lambda i: (i, 0))
    vec = lambda w: pl.BlockSpec((1, w), lambda i: (0, 0))

    def conv_spec(s):
        return pl.BlockSpec((tm, cw), lambda i: (jnp.clip(i - s * tiles_per_seq, 0, tiles_per_seq - 1), 0))

    return pl.pallas_call(
        functools.partial(_out_proj_kernel, tiles_per_seq=tiles_per_seq),
        out_shape=[jax.ShapeDtypeStruct((m, d), F32), jax.ShapeDtypeStruct((m, d), BF16)],
        grid=(m // tm,),
        in_specs=[row(aw)] + [conv_spec(s) for s in range(len(conv_n))]
                 + [_resident((aw + cw, d), lambda i: (0, 0)), row(d),
                    vec(aw), vec(d), vec(d), gt_spec, sc_spec, sh_spec],
        out_specs=[row(d), row(d)],
        compiler_params=_params(("parallel",), 56),
        name="out_proj",
    )(o_attn, *conv_n, w_bf, x, g_ao, g_pm, g_pf, gt, sc, sh)


def _ffn_up_kernel(h_ref, wg_ref, wu_ref, hs_ref, wd_ref, a_ref, as_ref, wdbf_ref, wbf_ref):
    tn = wg_ref.shape[1]

    def swiglu(rows):
        gu = jnp.dot(rows, wbf_ref[...], preferred_element_type=F32)
        return (_silu(gu[:, 0:tn]) * gu[:, tn:2 * tn]).astype(BF16)

    @pl.when(pl.program_id(1) == 0)
    def _():
        wbf_ref[:, 0:tn] = wg_ref[...].astype(BF16)
        wbf_ref[:, tn:2 * tn] = wu_ref[...].astype(BF16)
        as_ref[...] = swiglu(hs_ref[...])

    a_ref[...] = swiglu(h_ref[...])
    wdbf_ref[...] = wd_ref[...].astype(BF16)


def _ffn_up(h, w_gate_up, hs, w_down):
    tm, tn = 1024, 512
    m, d = h.shape
    ms = hs.shape[0]
    d_ff = w_gate_up.shape[1] // 2
    n_blocks, m_tiles = d_ff // tn, m // tm
    slab = w_down.shape[0] // (n_blocks * m_tiles)
    return pl.pallas_call(
        _ffn_up_kernel,
        out_shape=[jax.ShapeDtypeStruct((m, d_ff), BF16), jax.ShapeDtypeStruct((ms, d_ff), BF16),
                   jax.ShapeDtypeStruct(w_down.shape, BF16)],
        grid=(n_blocks, m_tiles),
        in_specs=[pl.BlockSpec((tm, d), lambda n, i: (i, 0)),
                  pl.BlockSpec((d, tn), lambda n, i: (0, n)),
                  pl.BlockSpec((d, tn), lambda n, i: (0, n_blocks + n)),
                  _resident((ms, d), lambda n, i: (0, 0)),
                  pl.BlockSpec((slab, w_down.shape[1]), lambda n, i: (n * m_tiles + i, 0))],
        out_specs=[pl.BlockSpec((tm, tn), lambda n, i: (i, n)),
                   pl.BlockSpec((ms, tn), lambda n, i: (0, n)),
                   pl.BlockSpec((slab, w_down.shape[1]), lambda n, i: (n * m_tiles + i, 0))],
        scratch_shapes=[pltpu.VMEM((d, 2 * tn), BF16)],
        compiler_params=_params(("arbitrary", "arbitrary"), 56),
        name="ffn_up",
    )(h, w_gate_up, w_gate_up, hs, w_down)


def _ffn_down_kernel(a_ref, w_ref, x1_ref, g_ref, gt_ref, y_ref):
    z = jnp.dot(a_ref[...], w_ref[...], preferred_element_type=F32)
    y_ref[...] = x1_ref[...] + _mod2d(gt_ref) * _rms(z, g_ref[...])


def _ffn_down(act, w_bf, x1, g_post, mod, tm, rows_per_group):
    m, d_ff = act.shape
    d = w_bf.shape[1]
    gt, gt_spec = _mod_spec(mod, 5, tm, rows_per_group)
    return pl.pallas_call(
        _ffn_down_kernel,
        out_shape=jax.ShapeDtypeStruct((m, d), F32),
        grid=(m // tm,),
        in_specs=[pl.BlockSpec((tm, d_ff), lambda i: (i, 0)),
                  _resident((d_ff, d), lambda i: (0, 0)),
                  pl.BlockSpec((tm, d), lambda i: (i, 0)),
                  pl.BlockSpec((1, d), lambda i: (0, 0)), gt_spec],
        out_specs=pl.BlockSpec((tm, d), lambda i: (i, 0)),
        compiler_params=_params(("parallel",), 48),
        name="ffn_down",
    )(act, w_bf, x1, g_post, gt)


SELECT_ROWS = 16


def _select_sample_kernel(pt_ref, q_ref, km_ref, idx_ref, *, n_q, n_pages):
    b = pl.program_id(0)
    nb = km_ref.shape[2]
    lane = lax.broadcasted_iota(jnp.int32, (nb, 128), 1)
    rowi = lax.broadcasted_iota(jnp.int32, (nb, 128), 0)
    s = jnp.full((nb, 128), NEG, F32)
    for hh in range(N_HEADS):
        km = km_ref[0, hh]
        for t in range(n_q):
            col = jnp.sum(km * q_ref[0, hh, t:t + 1, :], axis=1, keepdims=True)
            s = jnp.where(lane == hh * n_q + t, col, s)
    out_row = lax.broadcasted_iota(jnp.int32, (SELECT_ROWS, 128), 0)
    out = jnp.zeros((SELECT_ROWS, 128), jnp.int32)
    for r in range(MOBA_TOPK):
        top = jnp.max(s, axis=0, keepdims=True)
        arg = jnp.min(jnp.where(s == top, rowi, nb), axis=0, keepdims=True)
        out = jnp.where(out_row == r, arg, out)
        s = jnp.where(rowi == arg, -jnp.inf, s)
        for pp in range(PAGES_PER_BLOCK):
            page = jnp.zeros((1, 128), jnp.int32)
            for j in range(nb):
                page = jnp.where(arg == j, pt_ref[b * n_pages + j * PAGES_PER_BLOCK + pp], page)
            out = jnp.where(out_row == 8 + r * PAGES_PER_BLOCK + pp, page, out)
    idx_ref[0] = out


def _select_sample(page_table_flat, q, kmean, n_pages):
    batch, _, n_q, _ = q.shape
    nb = kmean.shape[2]
    grid_spec = pltpu.PrefetchScalarGridSpec(
        num_scalar_prefetch=1,
        grid=(batch,),
        in_specs=[pl.BlockSpec((1, N_HEADS, n_q, HEAD_DIM), lambda b, pt: (b, 0, 0, 0)),
                  pl.BlockSpec((1, N_HEADS, nb, HEAD_DIM), lambda b, pt: (b, 0, 0, 0))],
        out_specs=pl.BlockSpec((1, SELECT_ROWS, 128), lambda b, pt: (b, 0, 0)),
    )
    return pl.pallas_call(
        functools.partial(_select_sample_kernel, n_q=n_q, n_pages=n_pages),
        out_shape=jax.ShapeDtypeStruct((batch, SELECT_ROWS, 128), jnp.int32),
        grid_spec=grid_spec,
        compiler_params=_params(("arbitrary",), 32),
        name="select_sample",
    )(page_table_flat, q, kmean)


def _moba_sample_kernel(page_ref, idx_ref, slopes_ref, q_ref, kn_ref, vn_ref, pool_k, pool_v, o_ref,
                        kbuf, vbuf, sem, *, n_q, past):
    n_sel = n_q * MOBA_TOPK * PAGES_PER_BLOCK
    step = pl.program_id(0)
    n_steps = pl.num_programs(0)
    slot = step % 2

    def page_copies(st, sl):
        hh = st % N_HEADS
        copies = []
        for n in range(n_sel):
            page = page_ref[st * n_sel + n]
            copies.append(pltpu.make_async_copy(pool_k.at[page, hh], kbuf.at[sl, n], sem.at[sl, 0]))
            copies.append(pltpu.make_async_copy(pool_v.at[page, hh], vbuf.at[sl, n], sem.at[sl, 1]))
        return copies

    @pl.when(step == 0)
    def _():
        for cp in page_copies(step, slot):
            cp.start()

    @pl.when(step + 1 < n_steps)
    def _():
        for cp in page_copies(step + 1, 1 - slot):
            cp.start()

    for cp in page_copies(step, slot):
        cp.wait()

    b = step // N_HEADS
    h = step % N_HEADS
    slope = slopes_ref[h]
    key = lax.broadcasted_iota(jnp.int32, (PAGE, 1), 0)
    k_refs = [kbuf.at[slot, n] for n in range(n_sel)]
    v_refs = [vbuf.at[slot, n] for n in range(n_sel)]
    for t in range(n_q):
        q = q_ref[0, 0, t:t + 1, :] * (HEAD_DIM ** -0.5)
        scores = []
        for s in range(MOBA_TOPK):
            blk = idx_ref[((b * N_HEADS + h) * n_q + t) * MOBA_TOPK + s]
            for pp in range(PAGES_PER_BLOCK):
                kp = k_refs[(t * MOBA_TOPK + s) * PAGES_PER_BLOCK + pp][...]
                dist = (past + t - blk * MOBA_BLOCK - pp * PAGE - key).astype(F32)
                scores.append(jnp.sum(kp * q, axis=1, keepdims=True) - slope * dist)
        own = []
        for t2 in range(t + 1):
            own.append(jnp.sum(kn_ref[0, 0, t2:t2 + 1, :] * q, axis=1, keepdims=True) - slope * float(t - t2))
        m = own[0]
        for sc in scores:
            m = jnp.maximum(m, jnp.max(sc, axis=0, keepdims=True))
        for sc in own[1:]:
            m = jnp.maximum(m, sc)
        l = jnp.zeros((1, 1), F32)
        acc = jnp.zeros((1, HEAD_DIM), F32)
        for n, sc in enumerate(scores):
            p = jnp.exp(sc - m)
            l = l + jnp.sum(p, axis=0, keepdims=True)
            acc = acc + jnp.sum(p * v_refs[n + t * MOBA_TOPK * PAGES_PER_BLOCK][...], axis=0, keepdims=True)
        for t2, sc in enumerate(own):
            p = jnp.exp(sc - m)
            l = l + p
            acc = acc + p * vn_ref[0, 0, t2:t2 + 1, :]
        o_ref[0, 0, t:t + 1, :] = acc / l


def _moba_sample(pages_flat, idx_flat, slopes, q, k_new, v_new, pool_k, pool_v, n_pages):
    batch, _, n_q, _ = q.shape
    past = n_pages * PAGE
    n_sel = n_q * MOBA_TOPK * PAGES_PER_BLOCK
    tok = pl.BlockSpec((1, 1, n_q, HEAD_DIM), lambda i, pg, idx, sl: (i // N_HEADS, i % N_HEADS, 0, 0))
    hbm = pl.BlockSpec(memory_space=pl.ANY)
    grid_spec = pltpu.PrefetchScalarGridSpec(
        num_scalar_prefetch=3,
        grid=(batch * N_HEADS,),
        in_specs=[tok, tok, tok, hbm, hbm],
        out_specs=tok,
        scratch_shapes=[pltpu.VMEM((2, n_sel, PAGE, HEAD_DIM), F32),
                        pltpu.VMEM((2, n_sel, PAGE, HEAD_DIM), F32),
                        pltpu.SemaphoreType.DMA((2, 2))],
    )
    return pl.pallas_call(
        functools.partial(_moba_sample_kernel, n_q=n_q, past=past),
        out_shape=jax.ShapeDtypeStruct((batch, N_HEADS, n_q, HEAD_DIM), F32),
        grid_spec=grid_spec,
        compiler_params=_params(("arbitrary",), 32),
        name="moba_sample",
    )(pages_flat, idx_flat, slopes, q, k_new, v_new, pool_k, pool_v)


def kernel(x_prompt, x_sample, cache_k, cache_v, state_conv, page_table, c_prompt, c_sample, w_ada, b_ada, g_pre_mix, w_in, w_dw, b_dw, g_conv_ln, b_conv_ln, g_attn_out, g_conv_out, w_out, g_post_mix, g_pre_ffn, w_gate_up, w_down, g_post_ffn):
    depth = w_ada.shape[0]
    assert depth == 1, "single layer: the prompt and sample residual streams are not chained across layers here"
    batch, seq, d = x_prompt.shape
    dec_batch, dec_seq, _ = x_sample.shape
    n_pages = page_table.shape[1]
    past = n_pages * PAGE
    conv_w = w_dw.shape[-1]
    assert past % MOBA_BLOCK == 0 and dec_seq <= MOBA_BLOCK and past // MOBA_BLOCK >= MOBA_TOPK
    assert seq % ATTN_TILE == 0 and d == ATTN_WIDTH + conv_w

    slopes = 2.0 ** (-8.0 * (jnp.arange(N_HEADS, dtype=F32) + 1.0) / N_HEADS)
    l = 0
    vec = lambda a: a[l].reshape(1, -1)
    g_pm, g_ao, g_co = vec(g_pre_mix), vec(g_attn_out), vec(g_conv_out)
    g_post, g_pf, g_pffn = vec(g_post_mix), vec(g_pre_ffn), vec(g_post_ffn)
    bdw, gln, bln = vec(b_dw), vec(g_conv_ln), vec(b_conv_ln)
    wdw = w_dw[l].reshape(CONV_K, conv_w)

    n_c = batch + dec_batch
    c_rows = -(-n_c // 8) * 8
    c_all = jnp.concatenate([c_prompt, c_sample, jnp.zeros((c_rows - n_c, d), F32)], axis=0)
    mod = _ada(c_all, w_ada[l], vec(b_ada))
    mod_p = mod.reshape(c_rows * 6, 1, d)
    mod_rows = jnp.repeat(mod[batch:n_c].reshape(dec_batch, 6, d), dec_seq, axis=0)
    mod_s = [mod_rows[:, w] for w in range(6)]

    m_s = dec_batch * dec_seq
    xp = x_prompt.reshape(batch * seq, d)
    xs = x_sample.reshape(m_s, d)
    hs = _modnorm(xs, g_pm, mod_s, 1, 0, m_s, None)

    assert batch == 2, "the conv side jobs are mapped one prompt sequence per K/V projection call"
    h, u, u_s = _glu_proj(xp, g_pm, mod_p, seq, w_in[l], hs, 3 * ATTN_WIDTH, 3 * ATTN_WIDTH + conv_w, conv_w)
    conv_params = (wdw, bdw, gln, bln, g_co)
    k_pages, k_heads, k_rows_s, kmean, w_out_bf, conv_n0, conv_buf0 = _kv_proj(
        h, w_in[l], 1, hs, batch, seq, True, 0, u, conv_params, w_out[l])
    v_pages, v_heads, v_rows_s, conv_n1, conv_buf1 = _kv_proj(
        h, w_in[l], 2, hs, batch, seq, False, 1, u, conv_params)
    conv_buf_p = jnp.concatenate([conv_buf0, conv_buf1], axis=0)
    q_heads, pen, q_rows_s = _q_proj(h, w_in[l], hs, kmean.reshape(batch, seq // MOBA_BLOCK, ATTN_WIDTH),
                                     batch, seq)

    pt_flat = page_table.reshape(-1)
    o_attn, kmean_s = _moba_prompt(slopes, pt_flat, q_heads, pen, k_heads, v_heads, cache_k[l], dec_batch, n_pages)
    o_attn = o_attn.reshape(batch * seq, ATTN_WIDTH)
    x1, h2 = _out_proj(o_attn, [conv_n0, conv_n1], w_out_bf, xp, g_ao, g_post, g_pf, mod_p, 512, seq)

    to_heads = lambda t: t.reshape(dec_batch, dec_seq, N_HEADS, HEAD_DIM).transpose(0, 2, 1, 3)
    q_s, k_s, v_s = to_heads(q_rows_s), to_heads(k_rows_s), to_heads(v_rows_s)
    kmean_s = kmean_s.transpose(0, 2, 1, 3)
    sel = _select_sample(pt_flat, q_s, kmean_s, n_pages)[:, :, :N_HEADS * dec_seq]
    per_query = lambda rows: rows.reshape(dec_batch, -1, N_HEADS, dec_seq).transpose(0, 2, 3, 1).reshape(-1)
    idx_flat = per_query(sel[:, :MOBA_TOPK])
    pages_flat = per_query(sel[:, 8:8 + MOBA_TOPK * PAGES_PER_BLOCK])
    o_s = _moba_sample(pages_flat, idx_flat, slopes, q_s, k_s, v_s, cache_k[l], cache_v[l], n_pages)
    o_attn_s = o_s.transpose(0, 2, 1, 3).reshape(m_s, ATTN_WIDTH)
    conv_n_s, conv_buf_s = _conv_sample(state_conv[l], u_s.reshape(dec_batch, dec_seq, conv_w),
                                        wdw, bdw, gln, bln, g_co)
    x1_s, h2_s = _out_proj(o_attn_s, [conv_n_s.reshape(m_s, conv_w)], w_out_bf, xs, g_ao, g_post, g_pf,
                           mod_s, m_s, None)

    act, act_s, w_down_bf = _ffn_up(h2, w_gate_up[l], h2_s, w_down[l])
    y_p = _ffn_down(act, w_down_bf, x1, g_pffn, mod_p, 256, seq).reshape(batch, seq, d)
    y_s = _ffn_down(act_s, w_down_bf, x1_s, g_pffn, mod_s, m_s, None).reshape(dec_batch, dec_seq, d)

    return (y_p, y_s, k_pages[None], v_pages[None], conv_buf_p[None],
            k_s[None], v_s[None], conv_buf_s[None])
```

```python
import functools

import jax
import jax.numpy as jnp
from jax import lax
from jax.experimental import pallas as pl
from jax.experimental.pallas import tpu as pltpu

F32 = jnp.float32
BF16 = jnp.bfloat16

N_HEADS = 8
HEAD_DIM = 128
ATTN_WIDTH = N_HEADS * HEAD_DIM
CONV_K = 31
CONV_BUF = CONV_K - 1
CONV_HALO = 32
CONV_CHUNK = 64
SUBLANES = 8
PAGE = 128
MOBA_BLOCK = 256
MOBA_TOPK = 3
PAGES_PER_BLOCK = MOBA_BLOCK // PAGE
ATTN_TILE = 4 * MOBA_BLOCK
PROJ_TILE = 512
ATTN_KEYS = 4 * MOBA_BLOCK
LOG2E = 1.4426950408889634
EPS = 1e-6
NEG = -1e30
MIB = 1024 * 1024

_NT = (((1,), (1,)), ((), ()))


def _params(sem, vmem_mib, flags=None):
    return pltpu.CompilerParams(dimension_semantics=sem, vmem_limit_bytes=vmem_mib * MIB, flags=flags)


def _resident(shape, index_map):
    return pl.BlockSpec(shape, index_map, pipeline_mode=pl.Buffered(1))


def _rms(x, g):
    return x * lax.rsqrt(jnp.mean(x * x, axis=-1, keepdims=True) + EPS) * g


def _silu(x):
    return x * jax.nn.sigmoid(x)


def _mod2d(ref):
    v = ref[...]
    return v.reshape(v.shape[-2], v.shape[-1])


def _mod_spec(mod, which, tm, rows_per_group):
    if rows_per_group is None:
        return mod[which], pl.BlockSpec((tm, mod[which].shape[1]), lambda i, *_: (i, 0))
    tiles = rows_per_group // tm
    return mod, pl.BlockSpec((1, 1, mod.shape[2]), lambda i, *_: ((i // tiles) * 6 + which, 0, 0))


def _ada_kernel(c_ref, w_ref, b_ref, o_ref):
    s = _silu(c_ref[...]).astype(BF16)
    o_ref[...] = jnp.dot(s, w_ref[...].astype(BF16), preferred_element_type=F32) + b_ref[...]


def _ada(c_all, w, b):
    rows, d = c_all.shape
    n = w.shape[1]
    tn = 1024
    return pl.pallas_call(
        _ada_kernel,
        out_shape=jax.ShapeDtypeStruct((rows, n), F32),
        grid=(n // tn,),
        in_specs=[pl.BlockSpec((rows, d), lambda j: (0, 0)),
                  pl.BlockSpec((d, tn), lambda j: (0, j)),
                  pl.BlockSpec((1, tn), lambda j: (0, j))],
        out_specs=pl.BlockSpec((rows, tn), lambda j: (0, j)),
        compiler_params=_params(("arbitrary",), 40),
        name="ada_mod",
    )(c_all, w, b)


def _modnorm_kernel(x_ref, g_ref, sc_ref, sh_ref, o_ref):
    h = _rms(x_ref[...], g_ref[...]) * (1.0 + _mod2d(sc_ref)) + _mod2d(sh_ref)
    o_ref[...] = h.astype(BF16)


def _modnorm(x, g, mod, which_sc, which_sh, tm, rows_per_group):
    m, d = x.shape
    sc, sc_spec = _mod_spec(mod, which_sc, tm, rows_per_group)
    sh, sh_spec = _mod_spec(mod, which_sh, tm, rows_per_group)
    return pl.pallas_call(
        _modnorm_kernel,
        out_shape=jax.ShapeDtypeStruct((m, d), BF16),
        grid=(m // tm,),
        in_specs=[pl.BlockSpec((tm, d), lambda i: (i, 0)),
                  pl.BlockSpec((1, d), lambda i: (0, 0)), sc_spec, sh_spec],
        out_specs=pl.BlockSpec((tm, d), lambda i: (i, 0)),
        compiler_params=_params(("parallel",), 32),
        name="modnorm",
    )(x, g, sc, sh)


def _kv_proj_kernel(*refs, tm, with_mean, with_cast):
    it = iter(refs)
    h_ref, w_ref, hs_ref = next(it), next(it), next(it)
    cast_ref = next(it) if with_cast else None
    conv_in = [next(it) for _ in range(7)]
    pages_ref, heads_ref, s_ref = next(it), next(it), next(it)
    mean_ref = next(it) if with_mean else None
    cast_out_ref = next(it) if with_cast else None
    conv_out = [next(it), next(it)]
    wbf_ref = next(it)
    conv_scratch = [next(it), next(it), next(it)]
    i = pl.program_id(0)

    @pl.when(i == 0)
    def _():
        wbf_ref[...] = w_ref[...].astype(BF16)
        s_ref[...] = jnp.dot(hs_ref[...], wbf_ref[...], preferred_element_type=F32)

    for pair in range(N_HEADS // 2):
        ps = slice(pair * 2 * HEAD_DIM, (pair + 1) * 2 * HEAD_DIM)
        acc = jnp.dot(h_ref[...], wbf_ref[:, ps], preferred_element_type=F32)
        for sub in range(2):
            hh = pair * 2 + sub
            cs = slice(sub * HEAD_DIM, (sub + 1) * HEAD_DIM)
            heads_ref[0, hh] = acc[:, cs].astype(BF16)
            for p in range(tm // PAGE):
                pages_ref[0, p, hh] = acc[p * PAGE:(p + 1) * PAGE, cs]
        if with_mean:
            for mb in range(tm // MOBA_BLOCK):
                mean_ref[mb, :, ps] = jnp.mean(acc[mb * MOBA_BLOCK:(mb + 1) * MOBA_BLOCK, :], axis=0, keepdims=True)
    if with_cast:
        cast_out_ref[...] = cast_ref[...].astype(BF16)
    _conv_tile(i == 0, i == pl.num_programs(0) - 1, *conv_in, *conv_out, *conv_scratch)


def _kv_proj(h, w_in, col_block, hs, batch, seq, with_mean, conv_seq, u, conv_params, cast_src=None):
    tm = PROJ_TILE
    m, d = h.shape
    ms = hs.shape[0]
    tiles = seq // tm
    steps = m // tm
    width = u.shape[1]
    tt = seq // steps
    assert tt * steps == seq and tt % CONV_HALO == 0 and tt >= CONV_BUF
    halo_per_tile = tt // CONV_HALO
    in_specs = [pl.BlockSpec((tm, d), lambda i: (i, 0)),
                _resident((d, ATTN_WIDTH), lambda i: (0, col_block)),
                _resident((ms, d), lambda i: (0, 0))]
    args = [h, w_in, hs]
    out_shape = [jax.ShapeDtypeStruct((batch, seq // PAGE, N_HEADS, PAGE, HEAD_DIM), F32),
                 jax.ShapeDtypeStruct((batch, N_HEADS, seq, HEAD_DIM), BF16),
                 jax.ShapeDtypeStruct((ms, ATTN_WIDTH), F32)]
    out_specs = [pl.BlockSpec((1, tm // PAGE, N_HEADS, PAGE, HEAD_DIM), lambda i: (i // tiles, i % tiles, 0, 0, 0)),
                 pl.BlockSpec((1, N_HEADS, tm, HEAD_DIM), lambda i: (i // tiles, 0, i % tiles, 0)),
                 pl.BlockSpec((ms, ATTN_WIDTH), lambda i: (0, 0))]
    if with_mean:
        out_shape.append(jax.ShapeDtypeStruct((m // MOBA_BLOCK, 1, ATTN_WIDTH), F32))
        out_specs.append(pl.BlockSpec((tm // MOBA_BLOCK, 1, ATTN_WIDTH), lambda i: (i, 0, 0)))
    if cast_src is not None:
        rows, cols = cast_src.shape
        slab = rows // steps
        in_specs.append(pl.BlockSpec((slab, cols), lambda i: (i, 0)))
        args.append(cast_src)
        out_shape.append(jax.ShapeDtypeStruct((rows, cols), BF16))
        out_specs.append(pl.BlockSpec((slab, cols), lambda i: (i, 0)))
    vec = pl.BlockSpec((1, width), lambda i: (0, 0))
    in_specs += [pl.BlockSpec((tt, width), lambda i: (conv_seq * steps + i, 0)),
                 pl.BlockSpec((CONV_HALO, width),
                              lambda i: (jnp.maximum((conv_seq * steps + i) * halo_per_tile - 1, 0), 0)),
                 pl.BlockSpec((CONV_K, width), lambda i: (0, 0)), vec, vec, vec, vec]
    args += [u, u, *conv_params]
    out_shape += [jax.ShapeDtypeStruct((seq, width), BF16), jax.ShapeDtypeStruct((1, CONV_BUF, width), F32)]
    out_specs += [pl.BlockSpec((tt, width), lambda i: (i, 0)),
                  pl.BlockSpec((1, CONV_BUF, width), lambda i: (0, 0, 0))]
    return pl.pallas_call(
        functools.partial(_kv_proj_kernel, tm=tm, with_mean=with_mean, with_cast=cast_src is not None),
        out_shape=out_shape,
        grid=(steps,),
        in_specs=in_specs,
        out_specs=out_specs,
        scratch_shapes=[pltpu.VMEM((d, ATTN_WIDTH), BF16),
                        pltpu.VMEM((CONV_HALO + tt + SUBLANES, width), F32),
                        pltpu.VMEM((CONV_CHUNK + SUBLANES, 128), F32),
                        pltpu.VMEM((tt, width), F32)],
        compiler_params=_params(("arbitrary",), 56),
        name="kv_proj",
    )(*args)


def _q_proj_kernel(h_ref, w_ref, hs_ref, km_ref, q_ref, pen_ref, qs_ref, wbf_ref, sm_ref, *, tm, tiles):
    i = pl.program_id(0)

    @pl.when(i == 0)
    def _():
        wbf_ref[...] = w_ref[...].astype(BF16)
        qs_ref[...] = jnp.dot(hs_ref[...], wbf_ref[...], preferred_element_type=F32)

    acc = jnp.dot(h_ref[...], wbf_ref[...], preferred_element_type=F32)
    km = km_ref[0]
    nb = km.shape[0]
    jrow = lax.broadcasted_iota(jnp.int32, (nb, tm), 0)
    tcol = lax.broadcasted_iota(jnp.int32, (nb, tm), 1)
    n_full = (i % tiles) * (tm // MOBA_BLOCK) + lax.shift_right_logical(tcol, MOBA_BLOCK.bit_length() - 1)
    valid = jrow < n_full
    own = jrow == n_full
    pens = []
    for hh in range(N_HEADS):
        cs = slice(hh * HEAD_DIM, (hh + 1) * HEAD_DIM)
        qh = acc[:, cs]
        q_ref[0, hh] = (qh * (HEAD_DIM ** -0.5 * LOG2E)).astype(BF16)
        s = lax.dot_general(km[:, cs], qh, _NT, precision=lax.Precision.HIGHEST, preferred_element_type=F32)
        sm = jnp.where(valid, s, NEG)
        sm_ref[...] = sm
        rank = jnp.zeros((nb, tm), jnp.int32)
        for jp in range(nb):
            row = sm_ref[jp:jp + 1, :]
            gt = jnp.where(row > sm, 1, 0)
            ge = jnp.where(row >= sm, 1, 0)
            rank = rank + jnp.where(jrow > jp, ge, gt)
        keep = (valid & (rank < MOBA_TOPK)) | own
        pens.append(jnp.where(keep, 0.0, NEG))
    pen = jnp.concatenate(pens, axis=0)
    pen_ref[0] = pen.T.astype(BF16)


def _q_proj(h, w_in, hs, kmean, batch, seq):
    tm = PROJ_TILE
    m, d = h.shape
    ms = hs.shape[0]
    tiles = seq // tm
    nb = kmean.shape[1]
    return pl.pallas_call(
        functools.partial(_q_proj_kernel, tm=tm, tiles=tiles),
        out_shape=[jax.ShapeDtypeStruct((batch, N_HEADS, seq, HEAD_DIM), BF16),
                   jax.ShapeDtypeStruct((batch, seq, N_HEADS * nb), BF16),
                   jax.ShapeDtypeStruct((ms, ATTN_WIDTH), F32)],
        grid=(m // tm,),
        in_specs=[pl.BlockSpec((tm, d), lambda i: (i, 0)),
                  _resident((d, ATTN_WIDTH), lambda i: (0, 0)),
                  _resident((ms, d), lambda i: (0, 0)),
                  pl.BlockSpec((1, nb, ATTN_WIDTH), lambda i: (i // tiles, 0, 0))],
        out_specs=[pl.BlockSpec((1, N_HEADS, tm, HEAD_DIM), lambda i: (i // tiles, 0, i % tiles, 0)),
                   pl.BlockSpec((1, tm, N_HEADS * nb), lambda i: (i // tiles, i % tiles, 0)),
                   pl.BlockSpec((ms, ATTN_WIDTH), lambda i: (0, 0))],
        scratch_shapes=[pltpu.VMEM((d, ATTN_WIDTH), BF16), pltpu.VMEM((nb, tm), F32)],
        compiler_params=_params(("arbitrary",), 48),
        name="q_proj",
    )(h, w_in, hs, kmean)


def _fold_lanes(x, op):
    out = x[:, 0:128]
    for c in range(1, x.shape[1] // 128):
        out = op(out, x[:, c * 128:(c + 1) * 128])
    return out


def _block_key_means(page_refs, o_ref):
    for blk in range(len(page_refs) // PAGES_PER_BLOCK):
        tot = jnp.zeros((N_HEADS, HEAD_DIM), F32)
        for pp in range(PAGES_PER_BLOCK):
            tot = tot + jnp.sum(page_refs[blk * PAGES_PER_BLOCK + pp][0], axis=1)
        o_ref[0, blk] = tot * (1.0 / MOBA_BLOCK)


def _moba_prompt_kernel(slopes_ref, pt_ref, q_ref, pen_ref, k_ref, v_ref, *refs, nb, n_side):
    page_refs = refs[:n_side]
    o_ref, km_ref, ke_ref, ve_ref, s_ref = refs[n_side:]
    _block_key_means(page_refs, km_ref)

    tq, tk = ATTN_TILE, ATTN_KEYS
    h = pl.program_id(1)
    g = pl.program_id(2)
    slope = slopes_ref[h] * LOG2E
    seq = k_ref.shape[2]

    @pl.when(g == 0)
    def _():
        blk = lax.shift_right_logical(lax.broadcasted_iota(jnp.int32, (seq, HEAD_DIM), 0),
                                      MOBA_BLOCK.bit_length() - 1)
        lane = lax.broadcasted_iota(jnp.int32, (seq, HEAD_DIM), 1)
        ke_ref[:, 0:HEAD_DIM] = k_ref[0, 0]
        ke_ref[:, HEAD_DIM:2 * HEAD_DIM] = jnp.where(lane == h * nb + blk, 1.0, 0.0).astype(BF16)
        ve_ref[:, 0:HEAD_DIM] = v_ref[0, 0]
        ve_ref[:, HEAD_DIM:2 * HEAD_DIM] = jnp.where(lane == 0, 1.0, 0.0).astype(BF16)

    qe = jnp.concatenate([q_ref[0, 0], pen_ref[0]], axis=1)
    key_lane = lax.broadcasted_iota(jnp.int32, (1, tk), 1)

    def scores(c):
        ke = ke_ref[pl.ds(pl.multiple_of(c * tk, tk), tk), :]
        s = lax.dot_general(qe, ke, _NT, preferred_element_type=F32)
        return s + slope * (c * tk + key_lane).astype(F32)

    last = lax.shift_right_logical(g * tq, tk.bit_length() - 1)
    qpos = g * tq + lax.broadcasted_iota(jnp.int32, (tq, tk), 0)
    kpos = last * tk + lax.broadcasted_iota(jnp.int32, (tq, tk), 1)
    s = jnp.where(kpos <= qpos, scores(last), NEG)
    s_ref[last] = s
    mx = _fold_lanes(s, jnp.maximum)

    def pass1(c, mx):
        s = scores(c)
        s_ref[c] = s
        return jnp.maximum(mx, _fold_lanes(s, jnp.maximum))

    mx = lax.fori_loop(0, last, pass1, mx)
    m = jnp.max(mx, axis=1, keepdims=True)

    def pass2(c, acc):
        p = jnp.exp2(s_ref[c] - m)
        ve = ve_ref[pl.ds(pl.multiple_of(c * tk, tk), tk), :]
        return acc + jnp.dot(p.astype(BF16), ve, preferred_element_type=F32)

    acc = lax.fori_loop(0, last + 1, pass2, jnp.zeros((tq, 2 * HEAD_DIM), F32))
    o_ref[0] = acc[:, 0:HEAD_DIM] / acc[:, HEAD_DIM:HEAD_DIM + 1]


def _moba_prompt(slopes, page_table_flat, q, pen, k, v, pool_k, side_batch, n_pages):
    batch, _, seq, _ = q.shape
    nb = seq // MOBA_BLOCK
    nt = seq // ATTN_TILE
    steps = batch * N_HEADS * nt
    n_side = side_batch * n_pages // steps
    assert n_side * steps == side_batch * n_pages and n_side % PAGES_PER_BLOCK == 0 and n_pages % n_side == 0
    steps_per_seq = n_pages // n_side

    def step(b, h, g):
        return (b * N_HEADS + h) * nt + g

    def page_spec(p):
        return pl.BlockSpec((1, N_HEADS, PAGE, HEAD_DIM),
                            lambda b, h, g, sl, pt: (pt[step(b, h, g) * n_side + p], 0, 0, 0))

    grid_spec = pltpu.PrefetchScalarGridSpec(
        num_scalar_prefetch=2,
        grid=(batch, N_HEADS, nt),
        in_specs=[pl.BlockSpec((1, 1, ATTN_TILE, HEAD_DIM), lambda b, h, g, sl, pt: (b, h, g, 0)),
                  pl.BlockSpec((1, ATTN_TILE, N_HEADS * nb), lambda b, h, g, sl, pt: (b, g, 0)),
                  pl.BlockSpec((1, 1, seq, HEAD_DIM), lambda b, h, g, sl, pt: (b, h, 0, 0)),
                  pl.BlockSpec((1, 1, seq, HEAD_DIM), lambda b, h, g, sl, pt: (b, h, 0, 0))]
                 + [page_spec(p) for p in range(n_side)],
        out_specs=[pl.BlockSpec((1, ATTN_TILE, HEAD_DIM), lambda b, h, g, sl, pt: (b, g, h)),
                   pl.BlockSpec((1, n_side // PAGES_PER_BLOCK, N_HEADS, HEAD_DIM),
                                lambda b, h, g, sl, pt: (step(b, h, g) // steps_per_seq,
                                                         step(b, h, g) % steps_per_seq, 0, 0))],
        scratch_shapes=[pltpu.VMEM((seq, 2 * HEAD_DIM), BF16),
                        pltpu.VMEM((seq, 2 * HEAD_DIM), BF16),
                        pltpu.VMEM((seq // ATTN_KEYS, ATTN_TILE, ATTN_KEYS), F32)],
    )
    return pl.pallas_call(
        functools.partial(_moba_prompt_kernel, nb=nb, n_side=n_side),
        out_shape=[jax.ShapeDtypeStruct((batch, seq, ATTN_WIDTH), F32),
                   jax.ShapeDtypeStruct((side_batch, n_pages // PAGES_PER_BLOCK, N_HEADS, HEAD_DIM), F32)],
        grid_spec=grid_spec,
        compiler_params=_params(("arbitrary", "arbitrary", "arbitrary"), 56),
        name="moba_prompt",
    )(slopes, page_table_flat, q, pen, k, v, *([pool_k] * n_side))


def _glu_proj_kernel(x_ref, gpm_ref, sc_ref, sh_ref, wa_ref, wg_ref, hs_ref, h_ref, u_ref, us_ref, wbf_ref):
    width = wa_ref.shape[1]

    def glu(rows):
        ag = jnp.dot(rows, wbf_ref[...], preferred_element_type=F32)
        return ag[:, 0:width] * jax.nn.sigmoid(ag[:, width:2 * width])

    @pl.when(pl.program_id(0) == 0)
    def _():
        wbf_ref[:, 0:width] = wa_ref[...].astype(BF16)
        wbf_ref[:, width:2 * width] = wg_ref[...].astype(BF16)
        us_ref[...] = glu(hs_ref[...])

    sc, sh = _mod2d(sc_ref), _mod2d(sh_ref)
    tm = x_ref.shape[0]
    chunk = tm // 2
    for r0 in range(0, tm, chunk):
        rs = slice(r0, r0 + chunk)
        h = (_rms(x_ref[rs, :], gpm_ref[...]) * (1.0 + sc) + sh).astype(BF16)
        h_ref[rs, :] = h
        u_ref[rs, :] = glu(h)


def _glu_proj(x, g_pm, mod, seq, w_in, hs, a_col0, g_col0, width):
    tm = PROJ_TILE
    m, d = x.shape
    ms = hs.shape[0]
    sc, sc_spec = _mod_spec(mod, 1, tm, seq)
    sh, sh_spec = _mod_spec(mod, 0, tm, seq)
    return pl.pallas_call(
        _glu_proj_kernel,
        out_shape=[jax.ShapeDtypeStruct((m, d), BF16), jax.ShapeDtypeStruct((m, width), F32),
                   jax.ShapeDtypeStruct((ms, width), F32)],
        grid=(m // tm,),
        in_specs=[pl.BlockSpec((tm, d), lambda i: (i, 0)),
                  pl.BlockSpec((1, d), lambda i: (0, 0)), sc_spec, sh_spec,
                  _resident((d, width), lambda i: (0, a_col0 // width)),
                  _resident((d, width), lambda i: (0, g_col0 // width)),
                  _resident((ms, d), lambda i: (0, 0))],
        out_specs=[pl.BlockSpec((tm, d), lambda i: (i, 0)),
                   pl.BlockSpec((tm, width), lambda i: (i, 0)),
                   pl.BlockSpec((ms, width), lambda i: (0, 0))],
        scratch_shapes=[pltpu.VMEM((d, 2 * width), BF16)],
        compiler_params=_params(("arbitrary",), 56),
        name="glu_proj",
    )(x, g_pm, sc, sh, w_in, w_in, hs)


def _conv_tail(y, gln, bln, gco):
    mu = jnp.mean(y, axis=-1, keepdims=True)
    var = jnp.mean(jnp.square(y - mu), axis=-1, keepdims=True)
    z = _silu((y - mu) * lax.rsqrt(var + EPS) * gln + bln)
    return _rms(z, gco)


def _conv_tile(first, last, u_ref, halo_ref, w_ref, bdw_ref, gln_ref, bln_ref, gco_ref, cn_ref, buf_ref,
               ext_ref, z_ref, y_ref):
    tt = u_ref.shape[0]
    ext_ref[0:CONV_HALO, :] = jnp.where(first, 0.0, halo_ref[...])
    ext_ref[CONV_HALO:CONV_HALO + tt, :] = u_ref[...]
    width = u_ref.shape[1]
    ext_ref[CONV_HALO + tt:CONV_HALO + tt + SUBLANES, :] = jnp.zeros((SUBLANES, width), F32)
    lead = CONV_HALO - CONV_BUF
    rc = z_ref.shape[0] - SUBLANES
    zr = rc + SUBLANES
    for cb in range(width // 128):
        cs = slice(cb * 128, (cb + 1) * 128)
        for c0 in range(0, tt, rc):
            y = None
            for b in range(SUBLANES):
                taps = [k for k in range(CONV_K) if (lead + k) % SUBLANES == b]
                z = None
                for k in taps:
                    r0 = c0 + lead + k - b
                    term = w_ref[k:k + 1, cs] * ext_ref[r0:r0 + zr, cs]
                    z = term if z is None else z + term
                if b == 0:
                    y = z[0:rc, :]
                else:
                    z_ref[...] = z
                    y = y + z_ref[b:b + rc, :]
            y_ref[c0:c0 + rc, cs] = y + bdw_ref[:, cs]
    cn_ref[...] = _conv_tail(y_ref[...], gln_ref[...], bln_ref[...], gco_ref[...]).astype(BF16)

    @pl.when(last)
    def _():
        buf_ref[0] = ext_ref[CONV_HALO + tt - CONV_BUF:CONV_HALO + tt, :]


def _conv_sample_kernel(st_ref, u_ref, w_ref, bdw_ref, gln_ref, bln_ref, gco_ref, cn_ref, buf_ref,
                        ext_ref, *, rows):
    ext_ref[0:CONV_BUF, :] = st_ref[0]
    ext_ref[CONV_BUF:CONV_BUF + rows, :] = u_ref[0]
    acc = jnp.zeros((rows, u_ref.shape[2]), F32)
    for k in range(CONV_K):
        acc = acc + w_ref[k:k + 1, :] * ext_ref[k:k + rows, :]
    y = acc + bdw_ref[...]
    cn_ref[0] = _conv_tail(y, gln_ref[...], bln_ref[...], gco_ref[...]).astype(BF16)
    buf_ref[0] = ext_ref[rows:rows + CONV_BUF, :]


def _conv_sample(state, u, w_dw, b_dw, g_ln, b_ln, g_co):
    batch, rows, width = u.shape
    vec = pl.BlockSpec((1, width), lambda b: (0, 0))
    tok = pl.BlockSpec((1, rows, width), lambda b: (b, 0, 0))
    buf = pl.BlockSpec((1, CONV_BUF, width), lambda b: (b, 0, 0))
    return pl.pallas_call(
        functools.partial(_conv_sample_kernel, rows=rows),
        out_shape=[jax.ShapeDtypeStruct((batch, rows, width), BF16),
                   jax.ShapeDtypeStruct((batch, CONV_BUF, width), F32)],
        grid=(batch,),
        in_specs=[buf, tok, pl.BlockSpec((CONV_K, width), lambda b: (0, 0)), vec, vec, vec, vec],
        out_specs=[tok, buf],
        scratch_shapes=[pltpu.VMEM((CONV_BUF + rows + 6, width), F32)],
        compiler_params=_params(("parallel",), 32),
        name="conv_sample",
    )(state, u, w_dw, b_dw, g_ln, b_ln, g_co)


def _out_proj_kernel(o_ref, *refs, tiles_per_seq):
    n_conv = len(refs) - 10
    cn_refs = refs[:n_conv]
    w_ref, x_ref, gao_ref, gpm_ref, gpf_ref, gt_ref, sc_ref, sh_ref, x1_ref, h2_ref = refs[n_conv:]
    tm = o_ref.shape[0]
    gt, sc, sh = _mod2d(gt_ref), _mod2d(sc_ref), _mod2d(sh_ref)
    chunk = tm // 2 if tm % 512 == 0 else tm
    for r0 in range(0, tm, chunk):
        rs = slice(r0, r0 + chunk)
        rows = lambda v: v if v.shape[0] == 1 else v[rs]
        an = _rms(o_ref[rs, :], gao_ref[...]).astype(BF16)
        cn = cn_refs[0][rs, :]
        for s in range(1, n_conv):
            cn = jnp.where(pl.program_id(0) >= s * tiles_per_seq, cn_refs[s][rs, :], cn)
        mix = jnp.concatenate([an, cn], axis=1)
        merged = jnp.dot(mix, w_ref[...], preferred_element_type=F32)
        x1 = x_ref[rs, :] + rows(gt) * _rms(merged, gpm_ref[...])
        x1_ref[rs, :] = x1
        h2_ref[rs, :] = (_rms(x1, gpf_ref[...]) * (1.0 + rows(sc)) + rows(sh)).astype(BF16)


def _out_proj(o_attn, conv_n, w_bf, x, g_ao, g_pm, g_pf, mod, tm, rows_per_group):
    m, d = x.shape
    aw = o_attn.shape[1]
    cw = conv_n[0].shape[1]
    tiles_per_seq = conv_n[0].shape[0] // tm
    assert tiles_per_seq * tm == conv_n[0].shape[0] and tiles_per_seq * len(conv_n) == m // tm
    gt, gt_spec = _mod_spec(mod, 2, tm, rows_per_group)
    sc, sc_spec = _mod_spec(mod, 4, tm, rows_per_group)
    sh, sh_spec = _mod_spec(mod, 3, tm, rows_per_group)
    row = lambda w: pl.BlockSpec((tm, w), lambda i: (i, 0))
    vec = lambda w: pl.BlockSpec((1, w), lambda i: (0, 0))

    def conv_spec(s):
        return pl.BlockSpec((tm, cw), lambda i: (jnp.clip(i - s * tiles_per_seq, 0, tiles_per_seq - 1), 0))

    return pl.pallas_call(
        functools.partial(_out_proj_kernel, tiles_per_seq=tiles_per_seq),
        out_shape=[jax.ShapeDtypeStruct((m, d), F32), jax.ShapeDtypeStruct((m, d), BF16)],
        grid=(m // tm,),
        in_specs=[row(aw)] + [conv_spec(s) for s in range(len(conv_n))]
                 + [_resident((aw + cw, d), lambda i: (0, 0)), row(d),
                    vec(aw), vec(d), vec(d), gt_spec, sc_spec, sh_spec],
        out_specs=[row(d), row(d)],
        compiler_params=_params(("parallel",), 56),
        name="out_proj",
    )(o_attn, *conv_n, w_bf, x, g_ao, g_pm, g_pf, gt, sc, sh)


def _ffn_up_kernel(h_ref, wg_ref, wu_ref, hs_ref, wd_ref, a_ref, as_ref, wdbf_ref, wbf_ref):
    tn = wg_ref.shape[1]

    def swiglu(rows):
        gu = jnp.dot(rows, wbf_ref[...], preferred_element_type=F32)
        return (_silu(gu[:, 0:tn]) * gu[:, tn:2 * tn]).astype(BF16)

    @pl.when(pl.program_id(1) == 0)
    def _():
        wbf_ref[:, 0:tn] = wg_ref[...].astype(BF16)
        wbf_ref[:, tn:2 * tn] = wu_ref[...].astype(BF16)
        as_ref[...] = swiglu(hs_ref[...])

    a_ref[...] = swiglu(h_ref[...])
    wdbf_ref[...] = wd_ref[...].astype(BF16)


def _ffn_up(h, w_gate_up, hs, w_down):
    tm, tn = 1024, 512
    m, d = h.shape
    ms = hs.shape[0]
    d_ff = w_gate_up.shape[1] // 2
    n_blocks, m_tiles = d_ff // tn, m // tm
    slab = w_down.shape[0] // (n_blocks * m_tiles)
    return pl.pallas_call(
        _ffn_up_kernel,
        out_shape=[jax.ShapeDtypeStruct((m, d_ff), BF16), jax.ShapeDtypeStruct((ms, d_ff), BF16),
                   jax.ShapeDtypeStruct(w_down.shape, BF16)],
        grid=(n_blocks, m_tiles),
        in_specs=[pl.BlockSpec((tm, d), lambda n, i: (i, 0)),
                  pl.BlockSpec((d, tn), lambda n, i: (0, n)),
                  pl.BlockSpec((d, tn), lambda n, i: (0, n_blocks + n)),
                  _resident((ms, d), lambda n, i: (0, 0)),
                  pl.BlockSpec((slab, w_down.shape[1]), lambda n, i: (n * m_tiles + i, 0))],
        out_specs=[pl.BlockSpec((tm, tn), lambda n, i: (i, n)),
                   pl.BlockSpec((ms, tn), lambda n, i: (0, n)),
                   pl.BlockSpec((slab, w_down.shape[1]), lambda n, i: (n * m_tiles + i, 0))],
        scratch_shapes=[pltpu.VMEM((d, 2 * tn), BF16)],
        compiler_params=_params(("arbitrary", "arbitrary"), 56),
        name="ffn_up",
    )(h, w_gate_up, w_gate_up, hs, w_down)


def _ffn_down_kernel(a_ref, w_ref, x1_ref, g_ref, gt_ref, y_ref):
    z = jnp.dot(a_ref[...], w_ref[...], preferred_element_type=F32)
    y_ref[...] = x1_ref[...] + _mod2d(gt_ref) * _rms(z, g_ref[...])


def _ffn_down(act, w_bf, x1, g_post, mod, tm, rows_per_group):
    m, d_ff = act.shape
    d = w_bf.shape[1]
    gt, gt_spec = _mod_spec(mod, 5, tm, rows_per_group)
    return pl.pallas_call(
        _ffn_down_kernel,
        out_shape=jax.ShapeDtypeStruct((m, d), F32),
        grid=(m // tm,),
        in_specs=[pl.BlockSpec((tm, d_ff), lambda i: (i, 0)),
                  _resident((d_ff, d), lambda i: (0, 0)),
                  pl.BlockSpec((tm, d), lambda i: (i, 0)),
                  pl.BlockSpec((1, d), lambda i: (0, 0)), gt_spec],
        out_specs=pl.BlockSpec((tm, d), lambda i: (i, 0)),
        compiler_params=_params(("parallel",), 48),
        name="ffn_down",
    )(act, w_bf, x1, g_post, gt)


SELECT_ROWS = 16


def _select_sample_kernel(pt_ref, q_ref, km_ref, idx_ref, *, n_q, n_pages):
    b = pl.program_id(0)
    nb = km_ref.shape[2]
    lane = lax.broadcasted_iota(jnp.int32, (nb, 128), 1)
    rowi = lax.broadcasted_iota(jnp.int32, (nb, 128), 0)
    s = jnp.full((nb, 128), NEG, F32)
    for hh in range(N_HEADS):
        km = km_ref[0, hh]
        for t in range(n_q):
            col = jnp.sum(km * q_ref[0, hh, t:t + 1, :], axis=1, keepdims=True)
            s = jnp.where(lane == hh * n_q + t, col, s)
    out_row = lax.broadcasted_iota(jnp.int32, (SELECT_ROWS, 128), 0)
    out = jnp.zeros((SELECT_ROWS, 128), jnp.int32)
    for r in range(MOBA_TOPK):
        top = jnp.max(s, axis=0, keepdims=True)
        arg = jnp.min(jnp.where(s == top, rowi, nb), axis=0, keepdims=True)
        out = jnp.where(out_row == r, arg, out)
        s = jnp.where(rowi == arg, -jnp.inf, s)
        for pp in range(PAGES_PER_BLOCK):
            page = jnp.zeros((1, 128), jnp.int32)
            for j in range(nb):
                page = jnp.where(arg == j, pt_ref[b * n_pages + j * PAGES_PER_BLOCK + pp], page)
            out = jnp.where(out_row == 8 + r * PAGES_PER_BLOCK + pp, page, out)
    idx_ref[0] = out


def _select_sample(page_table_flat, q, kmean, n_pages):
    batch, _, n_q, _ = q.shape
    nb = kmean.shape[2]
    grid_spec = pltpu.PrefetchScalarGridSpec(
        num_scalar_prefetch=1,
        grid=(batch,),
        in_specs=[pl.BlockSpec((1, N_HEADS, n_q, HEAD_DIM), lambda b, pt: (b, 0, 0, 0)),
                  pl.BlockSpec((1, N_HEADS, nb, HEAD_DIM), lambda b, pt: (b, 0, 0, 0))],
        out_specs=pl.BlockSpec((1, SELECT_ROWS, 128), lambda b, pt: (b, 0, 0)),
    )
    return pl.pallas_call(
        functools.partial(_select_sample_kernel, n_q=n_q, n_pages=n_pages),
        out_shape=jax.ShapeDtypeStruct((batch, SELECT_ROWS, 128), jnp.int32),
        grid_spec=grid_spec,
        compiler_params=_params(("arbitrary",), 32),
        name="select_sample",
    )(page_table_flat, q, kmean)


def _moba_sample_kernel(page_ref, idx_ref, slopes_ref, q_ref, kn_ref, vn_ref, pool_k, pool_v, o_ref,
                        kbuf, vbuf, sem, *, n_q, past):
    n_sel = n_q * MOBA_TOPK * PAGES_PER_BLOCK
    slots = kbuf.shape[0]
    ahead = slots - 1
    step = pl.program_id(0)
    n_steps = pl.num_programs(0)
    slot = step % slots

    def page_copies(st, sl):
        hh = st % N_HEADS
        copies = []
        for n in range(n_sel):
            page = page_ref[st * n_sel + n]
            copies.append(pltpu.make_async_copy(pool_k.at[page, hh], kbuf.at[sl, n], sem.at[sl, 0]))
            copies.append(pltpu.make_async_copy(pool_v.at[page, hh], vbuf.at[sl, n], sem.at[sl, 1]))
        return copies

    @pl.when(step == 0)
    def _():
        for first in range(ahead):
            for cp in page_copies(first, first):
                cp.start()

    @pl.when(step + ahead < n_steps)
    def _():
        for cp in page_copies(step + ahead, (step + ahead) % slots):
            cp.start()

    for cp in page_copies(step, slot):
        cp.wait()

    b = step // N_HEADS
    h = step % N_HEADS
    slope = slopes_ref[h]
    key = lax.broadcasted_iota(jnp.int32, (PAGE, 1), 0)
    k_refs = [kbuf.at[slot, n] for n in range(n_sel)]
    v_refs = [vbuf.at[slot, n] for n in range(n_sel)]
    for t in range(n_q):
        q = q_ref[0, 0, t:t + 1, :] * (HEAD_DIM ** -0.5)
        scores = []
        for s in range(MOBA_TOPK):
            blk = idx_ref[((b * N_HEADS + h) * n_q + t) * MOBA_TOPK + s]
            for pp in range(PAGES_PER_BLOCK):
                kp = k_refs[(t * MOBA_TOPK + s) * PAGES_PER_BLOCK + pp][...]
                dist = (past + t - blk * MOBA_BLOCK - pp * PAGE - key).astype(F32)
                scores.append(jnp.sum(kp * q, axis=1, keepdims=True) - slope * dist)
        own = []
        for t2 in range(t + 1):
            own.append(jnp.sum(kn_ref[0, 0, t2:t2 + 1, :] * q, axis=1, keepdims=True) - slope * float(t - t2))
        m = own[0]
        for sc in scores:
            m = jnp.maximum(m, jnp.max(sc, axis=0, keepdims=True))
        for sc in own[1:]:
            m = jnp.maximum(m, sc)
        l = jnp.zeros((1, 1), F32)
        acc = jnp.zeros((1, HEAD_DIM), F32)
        for n, sc in enumerate(scores):
            p = jnp.exp(sc - m)
            l = l + jnp.sum(p, axis=0, keepdims=True)
            acc = acc + jnp.sum(p * v_refs[n + t * MOBA_TOPK * PAGES_PER_BLOCK][...], axis=0, keepdims=True)
        for t2, sc in enumerate(own):
            p = jnp.exp(sc - m)
            l = l + p
            acc = acc + p * vn_ref[0, 0, t2:t2 + 1, :]
        o_ref[0, 0, t:t + 1, :] = acc / l


def _moba_sample(pages_flat, idx_flat, slopes, q, k_new, v_new, pool_k, pool_v, n_pages):
    batch, _, n_q, _ = q.shape
    past = n_pages * PAGE
    n_sel = n_q * MOBA_TOPK * PAGES_PER_BLOCK
    slots = 3
    assert batch * N_HEADS >= slots
    tok = pl.BlockSpec((1, 1, n_q, HEAD_DIM), lambda i, pg, idx, sl: (i // N_HEADS, i % N_HEADS, 0, 0))
    hbm = pl.BlockSpec(memory_space=pl.ANY)
    grid_spec = pltpu.PrefetchScalarGridSpec(
        num_scalar_prefetch=3,
        grid=(batch * N_HEADS,),
        in_specs=[tok, tok, tok, hbm, hbm],
        out_specs=tok,
        scratch_shapes=[pltpu.VMEM((slots, n_sel, PAGE, HEAD_DIM), F32),
                        pltpu.VMEM((slots, n_sel, PAGE, HEAD_DIM), F32),
                        pltpu.SemaphoreType.DMA((slots, 2))],
    )
    return pl.pallas_call(
        functools.partial(_moba_sample_kernel, n_q=n_q, past=past),
        out_shape=jax.ShapeDtypeStruct((batch, N_HEADS, n_q, HEAD_DIM), F32),
        grid_spec=grid_spec,
        compiler_params=_params(("arbitrary",), 32),
        name="moba_sample",
    )(pages_flat, idx_flat, slopes, q, k_new, v_new, pool_k, pool_v)


def kernel(x_prompt, x_sample, cache_k, cache_v, state_conv, page_table, c_prompt, c_sample, w_ada, b_ada, g_pre_mix, w_in, w_dw, b_dw, g_conv_ln, b_conv_ln, g_attn_out, g_conv_out, w_out, g_post_mix, g_pre_ffn, w_gate_up, w_down, g_post_ffn):
    depth = w_ada.shape[0]
    assert depth == 1, "single layer: the prompt and sample residual streams are not chained across layers here"
    batch, seq, d = x_prompt.shape
    dec_batch, dec_seq, _ = x_sample.shape
    n_pages = page_table.shape[1]
    past = n_pages * PAGE
    conv_w = w_dw.shape[-1]
    assert past % MOBA_BLOCK == 0 and dec_seq <= MOBA_BLOCK and past // MOBA_BLOCK >= MOBA_TOPK
    assert seq % ATTN_TILE == 0 and d == ATTN_WIDTH + conv_w

    slopes = 2.0 ** (-8.0 * (jnp.arange(N_HEADS, dtype=F32) + 1.0) / N_HEADS)
    l = 0
    vec = lambda a: a[l].reshape(1, -1)
    g_pm, g_ao, g_co = vec(g_pre_mix), vec(g_attn_out), vec(g_conv_out)
    g_post, g_pf, g_pffn = vec(g_post_mix), vec(g_pre_ffn), vec(g_post_ffn)
    bdw, gln, bln = vec(b_dw), vec(g_conv_ln), vec(b_conv_ln)
    wdw = w_dw[l].reshape(CONV_K, conv_w)

    n_c = batch + dec_batch
    c_rows = -(-n_c // 8) * 8
    c_all = jnp.concatenate([c_prompt, c_sample, jnp.zeros((c_rows - n_c, d), F32)], axis=0)
    mod = _ada(c_all, w_ada[l], vec(b_ada))
    mod_p = mod.reshape(c_rows * 6, 1, d)
    mod_rows = jnp.repeat(mod[batch:n_c].reshape(dec_batch, 6, d), dec_seq, axis=0)
    mod_s = [mod_rows[:, w] for w in range(6)]

    m_s = dec_batch * dec_seq
    xp = x_prompt.reshape(batch * seq, d)
    xs = x_sample.reshape(m_s, d)
    hs = _modnorm(xs, g_pm, mod_s, 1, 0, m_s, None)

    assert batch == 2, "the conv side jobs are mapped one prompt sequence per K/V projection call"
    h, u, u_s = _glu_proj(xp, g_pm, mod_p, seq, w_in[l], hs, 3 * ATTN_WIDTH, 3 * ATTN_WIDTH + conv_w, conv_w)
    conv_params = (wdw, bdw, gln, bln, g_co)
    k_pages, k_heads, k_rows_s, kmean, w_out_bf, conv_n0, conv_buf0 = _kv_proj(
        h, w_in[l], 1, hs, batch, seq, True, 0, u, conv_params, w_out[l])
    v_pages, v_heads, v_rows_s, conv_n1, conv_buf1 = _kv_proj(
        h, w_in[l], 2, hs, batch, seq, False, 1, u, conv_params)
    conv_buf_p = jnp.concatenate([conv_buf0, conv_buf1], axis=0)
    q_heads, pen, q_rows_s = _q_proj(h, w_in[l], hs, kmean.reshape(batch, seq // MOBA_BLOCK, ATTN_WIDTH),
                                     batch, seq)

    pt_flat = page_table.reshape(-1)
    o_attn, kmean_s = _moba_prompt(slopes, pt_flat, q_heads, pen, k_heads, v_heads, cache_k[l], dec_batch, n_pages)
    o_attn = o_attn.reshape(batch * seq, ATTN_WIDTH)
    x1, h2 = _out_proj(o_attn, [conv_n0, conv_n1], w_out_bf, xp, g_ao, g_post, g_pf, mod_p, 512, seq)

    to_heads = lambda t: t.reshape(dec_batch, dec_seq, N_HEADS, HEAD_DIM).transpose(0, 2, 1, 3)
    q_s, k_s, v_s = to_heads(q_rows_s), to_heads(k_rows_s), to_heads(v_rows_s)
    kmean_s = kmean_s.transpose(0, 2, 1, 3)
    sel = _select_sample(pt_flat, q_s, kmean_s, n_pages)[:, :, :N_HEADS * dec_seq]
    per_query = lambda rows: rows.reshape(dec_batch, -1, N_HEADS, dec_seq).transpose(0, 2, 3, 1).reshape(-1)
    idx_flat = per_query(sel[:, :MOBA_TOPK])
    pages_flat = per_query(sel[:, 8:8 + MOBA_TOPK * PAGES_PER_BLOCK])
    o_s = _moba_sample(pages_flat, idx_flat, slopes, q_s, k_s, v_s, cache_k[l], cache_v[l], n_pages)
    o_attn_s = o_s.transpose(0, 2, 1, 3).reshape(m_s, ATTN_WIDTH)
    conv_n_s, conv_buf_s = _conv_sample(state_conv[l], u_s.reshape(dec_batch, dec_seq, conv_w),
                                        wdw, bdw, gln, bln, g_co)
    x1_s, h2_s = _out_proj(o_attn_s, [conv_n_s.reshape(m_s, conv_w)], w_out_bf, xs, g_ao, g_post, g_pf,
                           mod_s, m_s, None)

    act, act_s, w_down_bf = _ffn_up(h2, w_gate_up[l], h2_s, w_down[l])
    y_p = _ffn_down(act, w_down_bf, x1, g_pffn, mod_p, 256, seq).reshape(batch, seq, d)
    y_s = _ffn_down(act_s, w_down_bf, x1_s, g_pffn, mod_s, m_s, None).reshape(dec_batch, dec_seq, d)

    return (y_p, y_s, k_pages[None], v_pages[None], conv_buf_p[None],
            k_s[None], v_s[None], conv_buf_s[None])
```

```python
import functools

import jax
import jax.numpy as jnp
from jax import lax
from jax.experimental import pallas as pl
from jax.experimental.pallas import tpu as pltpu

F32 = jnp.float32
BF16 = jnp.bfloat16

N_HEADS = 8
HEAD_DIM = 128
ATTN_WIDTH = N_HEADS * HEAD_DIM
CONV_K = 31
CONV_BUF = CONV_K - 1
CONV_HALO = 32
CONV_CHUNK = 64
SUBLANES = 8
PAGE = 128
MOBA_BLOCK = 256
MOBA_TOPK = 3
PAGES_PER_BLOCK = MOBA_BLOCK // PAGE
ATTN_TILE = 4 * MOBA_BLOCK
PROJ_TILE = 512
ATTN_KEYS = 4 * MOBA_BLOCK
LOG2E = 1.4426950408889634
EPS = 1e-6
NEG = -1e30
MIB = 1024 * 1024

_NT = (((1,), (1,)), ((), ()))


def _params(sem, vmem_mib, flags=None):
    return pltpu.CompilerParams(dimension_semantics=sem, vmem_limit_bytes=vmem_mib * MIB, flags=flags)


def _resident(shape, index_map):
    return pl.BlockSpec(shape, index_map, pipeline_mode=pl.Buffered(1))


def _rms(x, g):
    return x * lax.rsqrt(jnp.mean(x * x, axis=-1, keepdims=True) + EPS) * g


def _silu(x):
    return x * jax.nn.sigmoid(x)


def _mod2d(ref):
    v = ref[...]
    return v.reshape(v.shape[-2], v.shape[-1])


def _mod_spec(mod, which, tm, rows_per_group):
    if rows_per_group is None:
        return mod[which], pl.BlockSpec((tm, mod[which].shape[1]), lambda i, *_: (i, 0))
    tiles = rows_per_group // tm
    return mod, pl.BlockSpec((1, 1, mod.shape[2]), lambda i, *_: ((i // tiles) * 6 + which, 0, 0))


def _ada_kernel(c_ref, w_ref, b_ref, o_ref):
    s = _silu(c_ref[...]).astype(BF16)
    o_ref[...] = jnp.dot(s, w_ref[...].astype(BF16), preferred_element_type=F32) + b_ref[...]


def _ada(c_all, w, b):
    rows, d = c_all.shape
    n = w.shape[1]
    tn = 1024
    return pl.pallas_call(
        _ada_kernel,
        out_shape=jax.ShapeDtypeStruct((rows, n), F32),
        grid=(n // tn,),
        in_specs=[pl.BlockSpec((rows, d), lambda j: (0, 0)),
                  pl.BlockSpec((d, tn), lambda j: (0, j)),
                  pl.BlockSpec((1, tn), lambda j: (0, j))],
        out_specs=pl.BlockSpec((rows, tn), lambda j: (0, j)),
        compiler_params=_params(("arbitrary",), 40),
        name="ada_mod",
    )(c_all, w, b)


def _modnorm_kernel(x_ref, g_ref, sc_ref, sh_ref, o_ref):
    h = _rms(x_ref[...], g_ref[...]) * (1.0 + _mod2d(sc_ref)) + _mod2d(sh_ref)
    o_ref[...] = h.astype(BF16)


def _modnorm(x, g, mod, which_sc, which_sh, tm, rows_per_group):
    m, d = x.shape
    sc, sc_spec = _mod_spec(mod, which_sc, tm, rows_per_group)
    sh, sh_spec = _mod_spec(mod, which_sh, tm, rows_per_group)
    return pl.pallas_call(
        _modnorm_kernel,
        out_shape=jax.ShapeDtypeStruct((m, d), BF16),
        grid=(m // tm,),
        in_specs=[pl.BlockSpec((tm, d), lambda i: (i, 0)),
                  pl.BlockSpec((1, d), lambda i: (0, 0)), sc_spec, sh_spec],
        out_specs=pl.BlockSpec((tm, d), lambda i: (i, 0)),
        compiler_params=_params(("parallel",), 32),
        name="modnorm",
    )(x, g, sc, sh)


def _kv_proj_kernel(*refs, tm, with_mean, with_cast):
    it = iter(refs)
    h_ref, w_ref, hs_ref = next(it), next(it), next(it)
    cast_ref = next(it) if with_cast else None
    conv_in = [next(it) for _ in range(7)]
    pages_ref, heads_ref, s_ref = next(it), next(it), next(it)
    mean_ref = next(it) if with_mean else None
    cast_out_ref = next(it) if with_cast else None
    conv_out = [next(it), next(it)]
    wbf_ref = next(it)
    conv_scratch = [next(it), next(it), next(it)]
    i = pl.program_id(0)

    @pl.when(i == 0)
    def _():
        wbf_ref[...] = w_ref[...].astype(BF16)
        s_ref[...] = jnp.dot(hs_ref[...], wbf_ref[...], preferred_element_type=F32)

    for pair in range(N_HEADS // 2):
        ps = slice(pair * 2 * HEAD_DIM, (pair + 1) * 2 * HEAD_DIM)
        acc = jnp.dot(h_ref[...], wbf_ref[:, ps], preferred_element_type=F32)
        for sub in range(2):
            hh = pair * 2 + sub
            cs = slice(sub * HEAD_DIM, (sub + 1) * HEAD_DIM)
            heads_ref[0, hh] = acc[:, cs].astype(BF16)
            for p in range(tm // PAGE):
                pages_ref[0, p, hh] = acc[p * PAGE:(p + 1) * PAGE, cs]
        if with_mean:
            for mb in range(tm // MOBA_BLOCK):
                mean_ref[mb, :, ps] = jnp.mean(acc[mb * MOBA_BLOCK:(mb + 1) * MOBA_BLOCK, :], axis=0, keepdims=True)
    if with_cast:
        cast_out_ref[...] = cast_ref[...].astype(BF16)
    _conv_tile(i == 0, i == pl.num_programs(0) - 1, *conv_in, *conv_out, *conv_scratch)


def _kv_proj(h, w_in, col_block, hs, batch, seq, with_mean, conv_seq, u, conv_params, cast_src=None):
    tm = PROJ_TILE
    m, d = h.shape
    ms = hs.shape[0]
    tiles = seq // tm
    steps = m // tm
    width = u.shape[1]
    tt = seq // steps
    assert tt * steps == seq and tt % CONV_HALO == 0 and tt >= CONV_BUF
    halo_per_tile = tt // CONV_HALO
    in_specs = [pl.BlockSpec((tm, d), lambda i: (i, 0)),
                _resident((d, ATTN_WIDTH), lambda i: (0, col_block)),
                _resident((ms, d), lambda i: (0, 0))]
    args = [h, w_in, hs]
    out_shape = [jax.ShapeDtypeStruct((batch, seq // PAGE, N_HEADS, PAGE, HEAD_DIM), F32),
                 jax.ShapeDtypeStruct((batch, N_HEADS, seq, HEAD_DIM), BF16),
                 jax.ShapeDtypeStruct((ms, ATTN_WIDTH), F32)]
    out_specs = [pl.BlockSpec((1, tm // PAGE, N_HEADS, PAGE, HEAD_DIM), lambda i: (i // tiles, i % tiles, 0, 0, 0)),
                 pl.BlockSpec((1, N_HEADS, tm, HEAD_DIM), lambda i: (i // tiles, 0, i % tiles, 0)),
                 pl.BlockSpec((ms, ATTN_WIDTH), lambda i: (0, 0))]
    if with_mean:
        out_shape.append(jax.ShapeDtypeStruct((m // MOBA_BLOCK, 1, ATTN_WIDTH), F32))
        out_specs.append(pl.BlockSpec((tm // MOBA_BLOCK, 1, ATTN_WIDTH), lambda i: (i, 0, 0)))
    if cast_src is not None:
        rows, cols = cast_src.shape
        slab = rows // steps
        in_specs.append(pl.BlockSpec((slab, cols), lambda i: (i, 0)))
        args.append(cast_src)
        out_shape.append(jax.ShapeDtypeStruct((rows, cols), BF16))
        out_specs.append(pl.BlockSpec((slab, cols), lambda i: (i, 0)))
    vec = pl.BlockSpec((1, width), lambda i: (0, 0))
    in_specs += [pl.BlockSpec((tt, width), lambda i: (conv_seq * steps + i, 0)),
                 pl.BlockSpec((CONV_HALO, width),
                              lambda i: (jnp.maximum((conv_seq * steps + i) * halo_per_tile - 1, 0), 0)),
                 pl.BlockSpec((CONV_K, width), lambda i: (0, 0)), vec, vec, vec, vec]
    args += [u, u, *conv_params]
    out_shape += [jax.ShapeDtypeStruct((seq, width), BF16), jax.ShapeDtypeStruct((1, CONV_BUF, width), F32)]
    out_specs += [pl.BlockSpec((tt, width), lambda i: (i, 0)),
                  pl.BlockSpec((1, CONV_BUF, width), lambda i: (0, 0, 0))]
    return pl.pallas_call(
        functools.partial(_kv_proj_kernel, tm=tm, with_mean=with_mean, with_cast=cast_src is not None),
        out_shape=out_shape,
        grid=(steps,),
        in_specs=in_specs,
        out_specs=out_specs,
        scratch_shapes=[pltpu.VMEM((d, ATTN_WIDTH), BF16),
                        pltpu.VMEM((CONV_HALO + tt + SUBLANES, width), F32),
                        pltpu.VMEM((CONV_CHUNK + SUBLANES, 128), F32),
                        pltpu.VMEM((tt, width), F32)],
        compiler_params=_params(("arbitrary",), 56),
        name="kv_proj",
    )(*args)


def _q_proj_kernel(h_ref, w_ref, hs_ref, km_ref, q_ref, pen_ref, qs_ref, wbf_ref, sm_ref, *, tm, tiles):
    i = pl.program_id(0)

    @pl.when(i == 0)
    def _():
        wbf_ref[...] = w_ref[...].astype(BF16)
        qs_ref[...] = jnp.dot(hs_ref[...], wbf_ref[...], preferred_element_type=F32)

    acc = jnp.dot(h_ref[...], wbf_ref[...], preferred_element_type=F32)
    km = km_ref[0]
    nb = km.shape[0]
    jrow = lax.broadcasted_iota(jnp.int32, (nb, tm), 0)
    tcol = lax.broadcasted_iota(jnp.int32, (nb, tm), 1)
    n_full = (i % tiles) * (tm // MOBA_BLOCK) + lax.shift_right_logical(tcol, MOBA_BLOCK.bit_length() - 1)
    valid = jrow < n_full
    own = jrow == n_full
    pens = []
    for hh in range(N_HEADS):
        cs = slice(hh * HEAD_DIM, (hh + 1) * HEAD_DIM)
        qh = acc[:, cs]
        q_ref[0, hh] = (qh * (HEAD_DIM ** -0.5 * LOG2E)).astype(BF16)
        s = lax.dot_general(km[:, cs], qh, _NT, precision=lax.Precision.HIGHEST, preferred_element_type=F32)
        sm = jnp.where(valid, s, NEG)
        sm_ref[...] = sm
        rank = jnp.zeros((nb, tm), jnp.int32)
        for jp in range(nb):
            row = sm_ref[jp:jp + 1, :]
            gt = jnp.where(row > sm, 1, 0)
            ge = jnp.where(row >= sm, 1, 0)
            rank = rank + jnp.where(jrow > jp, ge, gt)
        keep = (valid & (rank < MOBA_TOPK)) | own
        pens.append(jnp.where(keep, 0.0, NEG))
    pen = jnp.concatenate(pens, axis=0)
    pen_ref[0] = pen.T.astype(BF16)


def _q_proj(h, w_in, hs, kmean, batch, seq):
    tm = PROJ_TILE
    m, d = h.shape
    ms = hs.shape[0]
    tiles = seq // tm
    nb = kmean.shape[1]
    return pl.pallas_call(
        functools.partial(_q_proj_kernel, tm=tm, tiles=tiles),
        out_shape=[jax.ShapeDtypeStruct((batch, N_HEADS, seq, HEAD_DIM), BF16),
                   jax.ShapeDtypeStruct((batch, seq, N_HEADS * nb), BF16),
                   jax.ShapeDtypeStruct((ms, ATTN_WIDTH), F32)],
        grid=(m // tm,),
        in_specs=[pl.BlockSpec((tm, d), lambda i: (i, 0)),
                  _resident((d, ATTN_WIDTH), lambda i: (0, 0)),
                  _resident((ms, d), lambda i: (0, 0)),
                  pl.BlockSpec((1, nb, ATTN_WIDTH), lambda i: (i // tiles, 0, 0))],
        out_specs=[pl.BlockSpec((1, N_HEADS, tm, HEAD_DIM), lambda i: (i // tiles, 0, i % tiles, 0)),
                   pl.BlockSpec((1, tm, N_HEADS * nb), lambda i: (i // tiles, i % tiles, 0)),
                   pl.BlockSpec((ms, ATTN_WIDTH), lambda i: (0, 0))],
        scratch_shapes=[pltpu.VMEM((d, ATTN_WIDTH), BF16), pltpu.VMEM((nb, tm), F32)],
        compiler_params=_params(("arbitrary",), 48),
        name="q_proj",
    )(h, w_in, hs, kmean)


def _fold_lanes(x, op):
    out = x[:, 0:128]
    for c in range(1, x.shape[1] // 128):
        out = op(out, x[:, c * 128:(c + 1) * 128])
    return out


def _block_key_means(page_refs, o_ref):
    for blk in range(len(page_refs) // PAGES_PER_BLOCK):
        tot = jnp.zeros((N_HEADS, HEAD_DIM), F32)
        for pp in range(PAGES_PER_BLOCK):
            tot = tot + jnp.sum(page_refs[blk * PAGES_PER_BLOCK + pp][...], axis=1)
        o_ref[0, blk] = tot * (1.0 / MOBA_BLOCK)


def _moba_prompt_kernel(slopes_ref, pt_ref, q_ref, pen_ref, k_ref, v_ref, pool_ref, o_ref, km_ref,
                        ke_ref, ve_ref, s_ref, pbuf, sem, *, nb, n_side):
    h = pl.program_id(1)
    g = pl.program_id(2)
    step = (pl.program_id(0) * pl.num_programs(1) + h) * pl.num_programs(2) + g
    n_steps = pl.num_programs(0) * pl.num_programs(1) * pl.num_programs(2)
    slot = step % 2

    def page_copies(st, sl):
        return [pltpu.make_async_copy(pool_ref.at[pt_ref[st * n_side + n]], pbuf.at[sl, n], sem.at[sl])
                for n in range(n_side)]

    @pl.when(step == 0)
    def _():
        for cp in page_copies(step, slot):
            cp.start()

    @pl.when(step + 1 < n_steps)
    def _():
        for cp in page_copies(step + 1, 1 - slot):
            cp.start()

    for cp in page_copies(step, slot):
        cp.wait()
    _block_key_means([pbuf.at[slot, n] for n in range(n_side)], km_ref)

    tq, tk = ATTN_TILE, ATTN_KEYS
    slope = slopes_ref[h] * LOG2E
    seq = k_ref.shape[2]

    @pl.when(g == 0)
    def _():
        blk = lax.shift_right_logical(lax.broadcasted_iota(jnp.int32, (seq, HEAD_DIM), 0),
                                      MOBA_BLOCK.bit_length() - 1)
        lane = lax.broadcasted_iota(jnp.int32, (seq, HEAD_DIM), 1)
        ke_ref[:, 0:HEAD_DIM] = k_ref[0, 0]
        ke_ref[:, HEAD_DIM:2 * HEAD_DIM] = jnp.where(lane == h * nb + blk, 1.0, 0.0).astype(BF16)
        ve_ref[:, 0:HEAD_DIM] = v_ref[0, 0]
        ve_ref[:, HEAD_DIM:2 * HEAD_DIM] = jnp.where(lane == 0, 1.0, 0.0).astype(BF16)

    qe = jnp.concatenate([q_ref[0, 0], pen_ref[0]], axis=1)
    key_lane = lax.broadcasted_iota(jnp.int32, (1, tk), 1)

    def scores(c):
        ke = ke_ref[pl.ds(pl.multiple_of(c * tk, tk), tk), :]
        s = lax.dot_general(qe, ke, _NT, preferred_element_type=F32)
        return s + slope * (c * tk + key_lane).astype(F32)

    last = lax.shift_right_logical(g * tq, tk.bit_length() - 1)
    qpos = g * tq + lax.broadcasted_iota(jnp.int32, (tq, tk), 0)
    kpos = last * tk + lax.broadcasted_iota(jnp.int32, (tq, tk), 1)
    s = jnp.where(kpos <= qpos, scores(last), NEG)
    s_ref[last] = s
    mx = _fold_lanes(s, jnp.maximum)

    def pass1(c, mx):
        s = scores(c)
        s_ref[c] = s
        return jnp.maximum(mx, _fold_lanes(s, jnp.maximum))

    mx = lax.fori_loop(0, last, pass1, mx)
    m = jnp.max(mx, axis=1, keepdims=True)

    def pass2(c, acc):
        p = jnp.exp2(s_ref[c] - m)
        ve = ve_ref[pl.ds(pl.multiple_of(c * tk, tk), tk), :]
        return acc + jnp.dot(p.astype(BF16), ve, preferred_element_type=F32)

    acc = lax.fori_loop(0, last + 1, pass2, jnp.zeros((tq, 2 * HEAD_DIM), F32))
    o_ref[0] = acc[:, 0:HEAD_DIM] / acc[:, HEAD_DIM:HEAD_DIM + 1]


def _moba_prompt(slopes, page_table_flat, q, pen, k, v, pool_k, side_batch, n_pages):
    batch, _, seq, _ = q.shape
    nb = seq // MOBA_BLOCK
    nt = seq // ATTN_TILE
    steps = batch * N_HEADS * nt
    n_side = side_batch * n_pages // steps
    assert n_side * steps == side_batch * n_pages and n_side % PAGES_PER_BLOCK == 0 and n_pages % n_side == 0
    steps_per_seq = n_pages // n_side

    def step(b, h, g):
        return (b * N_HEADS + h) * nt + g

    grid_spec = pltpu.PrefetchScalarGridSpec(
        num_scalar_prefetch=2,
        grid=(batch, N_HEADS, nt),
        in_specs=[pl.BlockSpec((1, 1, ATTN_TILE, HEAD_DIM), lambda b, h, g, sl, pt: (b, h, g, 0)),
                  pl.BlockSpec((1, ATTN_TILE, N_HEADS * nb), lambda b, h, g, sl, pt: (b, g, 0)),
                  pl.BlockSpec((1, 1, seq, HEAD_DIM), lambda b, h, g, sl, pt: (b, h, 0, 0)),
                  pl.BlockSpec((1, 1, seq, HEAD_DIM), lambda b, h, g, sl, pt: (b, h, 0, 0)),
                  pl.BlockSpec(memory_space=pl.ANY)],
        out_specs=[pl.BlockSpec((1, ATTN_TILE, HEAD_DIM), lambda b, h, g, sl, pt: (b, g, h)),
                   pl.BlockSpec((1, n_side // PAGES_PER_BLOCK, N_HEADS, HEAD_DIM),
                                lambda b, h, g, sl, pt: (step(b, h, g) // steps_per_seq,
                                                         step(b, h, g) % steps_per_seq, 0, 0))],
        scratch_shapes=[pltpu.VMEM((seq, 2 * HEAD_DIM), BF16),
                        pltpu.VMEM((seq, 2 * HEAD_DIM), BF16),
                        pltpu.VMEM((seq // ATTN_KEYS, ATTN_TILE, ATTN_KEYS), F32),
                        pltpu.VMEM((2, n_side, N_HEADS, PAGE, HEAD_DIM), F32),
                        pltpu.SemaphoreType.DMA((2,))],
    )
    return pl.pallas_call(
        functools.partial(_moba_prompt_kernel, nb=nb, n_side=n_side),
        out_shape=[jax.ShapeDtypeStruct((batch, seq, ATTN_WIDTH), F32),
                   jax.ShapeDtypeStruct((side_batch, n_pages // PAGES_PER_BLOCK, N_HEADS, HEAD_DIM), F32)],
        grid_spec=grid_spec,
        compiler_params=_params(("arbitrary", "arbitrary", "arbitrary"), 56),
        name="moba_prompt",
    )(slopes, page_table_flat, q, pen, k, v, pool_k)


def _glu_proj_kernel(x_ref, gpm_ref, sc_ref, sh_ref, wa_ref, wg_ref, hs_ref, h_ref, u_ref, us_ref, wbf_ref):
    width = wa_ref.shape[1]

    def glu(rows):
        ag = jnp.dot(rows, wbf_ref[...], preferred_element_type=F32)
        return ag[:, 0:width] * jax.nn.sigmoid(ag[:, width:2 * width])

    @pl.when(pl.program_id(0) == 0)
    def _():
        wbf_ref[:, 0:width] = wa_ref[...].astype(BF16)
        wbf_ref[:, width:2 * width] = wg_ref[...].astype(BF16)
        us_ref[...] = glu(hs_ref[...])

    sc, sh = _mod2d(sc_ref), _mod2d(sh_ref)
    tm = x_ref.shape[0]
    chunk = tm // 2
    for r0 in range(0, tm, chunk):
        rs = slice(r0, r0 + chunk)
        h = (_rms(x_ref[rs, :], gpm_ref[...]) * (1.0 + sc) + sh).astype(BF16)
        h_ref[rs, :] = h
        u_ref[rs, :] = glu(h)


def _glu_proj(x, g_pm, mod, seq, w_in, hs, a_col0, g_col0, width):
    tm = PROJ_TILE
    m, d = x.shape
    ms = hs.shape[0]
    sc, sc_spec = _mod_spec(mod, 1, tm, seq)
    sh, sh_spec = _mod_spec(mod, 0, tm, seq)
    return pl.pallas_call(
        _glu_proj_kernel,
        out_shape=[jax.ShapeDtypeStruct((m, d), BF16), jax.ShapeDtypeStruct((m, width), F32),
                   jax.ShapeDtypeStruct((ms, width), F32)],
        grid=(m // tm,),
        in_specs=[pl.BlockSpec((tm, d), lambda i: (i, 0)),
                  pl.BlockSpec((1, d), lambda i: (0, 0)), sc_spec, sh_spec,
                  _resident((d, width), lambda i: (0, a_col0 // width)),
                  _resident((d, width), lambda i: (0, g_col0 // width)),
                  _resident((ms, d), lambda i: (0, 0))],
        out_specs=[pl.BlockSpec((tm, d), lambda i: (i, 0)),
                   pl.BlockSpec((tm, width), lambda i: (i, 0)),
                   pl.BlockSpec((ms, width), lambda i: (0, 0))],
        scratch_shapes=[pltpu.VMEM((d, 2 * width), BF16)],
        compiler_params=_params(("arbitrary",), 56),
        name="glu_proj",
    )(x, g_pm, sc, sh, w_in, w_in, hs)


def _conv_tail(y, gln, bln, gco):
    mu = jnp.mean(y, axis=-1, keepdims=True)
    var = jnp.mean(jnp.square(y - mu), axis=-1, keepdims=True)
    z = _silu((y - mu) * lax.rsqrt(var + EPS) * gln + bln)
    return _rms(z, gco)


def _conv_tile(first, last, u_ref, halo_ref, w_ref, bdw_ref, gln_ref, bln_ref, gco_ref, cn_ref, buf_ref,
               ext_ref, z_ref, y_ref):
    tt = u_ref.shape[0]
    ext_ref[0:CONV_HALO, :] = jnp.where(first, 0.0, halo_ref[...])
    ext_ref[CONV_HALO:CONV_HALO + tt, :] = u_ref[...]
    width = u_ref.shape[1]
    ext_ref[CONV_HALO + tt:CONV_HALO + tt + SUBLANES, :] = jnp.zeros((SUBLANES, width), F32)
    lead = CONV_HALO - CONV_BUF
    rc = z_ref.shape[0] - SUBLANES
    zr = rc + SUBLANES
    for cb in range(width // 128):
        cs = slice(cb * 128, (cb + 1) * 128)
        for c0 in range(0, tt, rc):
            y = None
            for b in range(SUBLANES):
                taps = [k for k in range(CONV_K) if (lead + k) % SUBLANES == b]
                z = None
                for k in taps:
                    r0 = c0 + lead + k - b
                    term = w_ref[k:k + 1, cs] * ext_ref[r0:r0 + zr, cs]
                    z = term if z is None else z + term
                if b == 0:
                    y = z[0:rc, :]
                else:
                    z_ref[...] = z
                    y = y + z_ref[b:b + rc, :]
            y_ref[c0:c0 + rc, cs] = y + bdw_ref[:, cs]
    cn_ref[...] = _conv_tail(y_ref[...], gln_ref[...], bln_ref[...], gco_ref[...]).astype(BF16)

    @pl.when(last)
    def _():
        buf_ref[0] = ext_ref[CONV_HALO + tt - CONV_BUF:CONV_HALO + tt, :]


def _conv_sample_kernel(st_ref, u_ref, w_ref, bdw_ref, gln_ref, bln_ref, gco_ref, cn_ref, buf_ref,
                        ext_ref, *, rows):
    ext_ref[0:CONV_BUF, :] = st_ref[0]
    ext_ref[CONV_BUF:CONV_BUF + rows, :] = u_ref[0]
    acc = jnp.zeros((rows, u_ref.shape[2]), F32)
    for k in range(CONV_K):
        acc = acc + w_ref[k:k + 1, :] * ext_ref[k:k + rows, :]
    y = acc + bdw_ref[...]
    cn_ref[0] = _conv_tail(y, gln_ref[...], bln_ref[...], gco_ref[...]).astype(BF16)
    buf_ref[0] = ext_ref[rows:rows + CONV_BUF, :]


def _conv_sample(state, u, w_dw, b_dw, g_ln, b_ln, g_co):
    batch, rows, width = u.shape
    vec = pl.BlockSpec((1, width), lambda b: (0, 0))
    tok = pl.BlockSpec((1, rows, width), lambda b: (b, 0, 0))
    buf = pl.BlockSpec((1, CONV_BUF, width), lambda b: (b, 0, 0))
    return pl.pallas_call(
        functools.partial(_conv_sample_kernel, rows=rows),
        out_shape=[jax.ShapeDtypeStruct((batch, rows, width), BF16),
                   jax.ShapeDtypeStruct((batch, CONV_BUF, width), F32)],
        grid=(batch,),
        in_specs=[buf, tok, pl.BlockSpec((CONV_K, width), lambda b: (0, 0)), vec, vec, vec, vec],
        out_specs=[tok, buf],
        scratch_shapes=[pltpu.VMEM((CONV_BUF + rows + 6, width), F32)],
        compiler_params=_params(("parallel",), 32),
        name="conv_sample",
    )(state, u, w_dw, b_dw, g_ln, b_ln, g_co)


def _out_proj_kernel(o_ref, *refs, tiles_per_seq):
    n_conv = len(refs) - 10
    cn_refs = refs[:n_conv]
    w_ref, x_ref, gao_ref, gpm_ref, gpf_ref, gt_ref, sc_ref, sh_ref, x1_ref, h2_ref = refs[n_conv:]
    tm = o_ref.shape[0]
    gt, sc, sh = _mod2d(gt_ref), _mod2d(sc_ref), _mod2d(sh_ref)
    chunk = tm // 2 if tm % 512 == 0 else tm
    for r0 in range(0, tm, chunk):
        rs = slice(r0, r0 + chunk)
        rows = lambda v: v if v.shape[0] == 1 else v[rs]
        an = _rms(o_ref[rs, :], gao_ref[...]).astype(BF16)
        cn = cn_refs[0][rs, :]
        for s in range(1, n_conv):
            cn = jnp.where(pl.program_id(0) >= s * tiles_per_seq, cn_refs[s][rs, :], cn)
        mix = jnp.concatenate([an, cn], axis=1)
        merged = jnp.dot(mix, w_ref[...], preferred_element_type=F32)
        x1 = x_ref[rs, :] + rows(gt) * _rms(merged, gpm_ref[...])
        x1_ref[rs, :] = x1
        h2_ref[rs, :] = (_rms(x1, gpf_ref[...]) * (1.0 + rows(sc)) + rows(sh)).astype(BF16)


def _out_proj(o_attn, conv_n, w_bf, x, g_ao, g_pm, g_pf, mod, tm, rows_per_group):
    m, d = x.shape
    aw = o_attn.shape[1]
    cw = conv_n[0].shape[1]
    tiles_per_seq = conv_n[0].shape[0] // tm
    assert tiles_per_seq * tm == conv_n[0].shape[0] and tiles_per_seq * len(conv_n) == m // tm
    gt, gt_spec = _mod_spec(mod, 2, tm, rows_per_group)
    sc, sc_spec = _mod_spec(mod, 4, tm, rows_per_group)
    sh, sh_spec = _mod_spec(mod, 3, tm, rows_per_group)
    row = lambda w: pl.BlockSpec((tm, w), lambda i: (i, 0))
    vec = lambda w: pl.BlockSpec((1, w), lambda i: (0, 0))

    def conv_spec(s):
        return pl.BlockSpec((tm, cw), lambda i: (jnp.clip(i - s * tiles_per_seq, 0, tiles_per_seq - 1), 0))

    return pl.pallas_call(
        functools.partial(_out_proj_kernel, tiles_per_seq=tiles_per_seq),
        out_shape=[jax.ShapeDtypeStruct((m, d), F32), jax.ShapeDtypeStruct((m, d), BF16)],
        grid=(m // tm,),
        in_specs=[row(aw)] + [conv_spec(s) for s in range(len(conv_n))]
                 + [_resident((aw + cw, d), lambda i: (0, 0)), row(d),
                    vec(aw), vec(d), vec(d), gt_spec, sc_spec, sh_spec],
        out_specs=[row(d), row(d)],
        compiler_params=_params(("parallel",), 56),
        name="out_proj",
    )(o_attn, *conv_n, w_bf, x, g_ao, g_pm, g_pf, gt, sc, sh)


def _ffn_up_kernel(h_ref, wg_ref, wu_ref, hs_ref, wd_ref, a_ref, as_ref, wdbf_ref, wbf_ref):
    tn = wg_ref.shape[1]

    def swiglu(rows):
        gu = jnp.dot(rows, wbf_ref[...], preferred_element_type=F32)
        return (_silu(gu[:, 0:tn]) * gu[:, tn:2 * tn]).astype(BF16)

    @pl.when(pl.program_id(1) == 0)
    def _():
        wbf_ref[:, 0:tn] = wg_ref[...].astype(BF16)
        wbf_ref[:, tn:2 * tn] = wu_ref[...].astype(BF16)
        as_ref[...] = swiglu(hs_ref[...])

    a_ref[...] = swiglu(h_ref[...])
    wdbf_ref[...] = wd_ref[...].astype(BF16)


def _ffn_up(h, w_gate_up, hs, w_down):
    tm, tn = 1024, 512
    m, d = h.shape
    ms = hs.shape[0]
    d_ff = w_gate_up.shape[1] // 2
    n_blocks, m_tiles = d_ff // tn, m // tm
    slab = w_down.shape[0] // (n_blocks * m_tiles)
    return pl.pallas_call(
        _ffn_up_kernel,
        out_shape=[jax.ShapeDtypeStruct((m, d_ff), BF16), jax.ShapeDtypeStruct((ms, d_ff), BF16),
                   jax.ShapeDtypeStruct(w_down.shape, BF16)],
        grid=(n_blocks, m_tiles),
        in_specs=[pl.BlockSpec((tm, d), lambda n, i: (i, 0)),
                  pl.BlockSpec((d, tn), lambda n, i: (0, n)),
                  pl.BlockSpec((d, tn), lambda n, i: (0, n_blocks + n)),
                  _resident((ms, d), lambda n, i: (0, 0)),
                  pl.BlockSpec((slab, w_down.shape[1]), lambda n, i: (n * m_tiles + i, 0))],
        out_specs=[pl.BlockSpec((tm, tn), lambda n, i: (i, n)),
                   pl.BlockSpec((ms, tn), lambda n, i: (0, n)),
                   pl.BlockSpec((slab, w_down.shape[1]), lambda n, i: (n * m_tiles + i, 0))],
        scratch_shapes=[pltpu.VMEM((d, 2 * tn), BF16)],
        compiler_params=_params(("arbitrary", "arbitrary"), 56),
        name="ffn_up",
    )(h, w_gate_up, w_gate_up, hs, w_down)


def _ffn_down_kernel(a_ref, w_ref, x1_ref, g_ref, gt_ref, y_ref):
    z = jnp.dot(a_ref[...], w_ref[...], preferred_element_type=F32)
    y_ref[...] = x1_ref[...] + _mod2d(gt_ref) * _rms(z, g_ref[...])


def _ffn_down(act, w_bf, x1, g_post, mod, tm, rows_per_group):
    m, d_ff = act.shape
    d = w_bf.shape[1]
    gt, gt_spec = _mod_spec(mod, 5, tm, rows_per_group)
    return pl.pallas_call(
        _ffn_down_kernel,
        out_shape=jax.ShapeDtypeStruct((m, d), F32),
        grid=(m // tm,),
        in_specs=[pl.BlockSpec((tm, d_ff), lambda i: (i, 0)),
                  _resident((d_ff, d), lambda i: (0, 0)),
                  pl.BlockSpec((tm, d), lambda i: (i, 0)),
                  pl.BlockSpec((1, d), lambda i: (0, 0)), gt_spec],
        out_specs=pl.BlockSpec((tm, d), lambda i: (i, 0)),
        compiler_params=_params(("parallel",), 48),
        name="ffn_down",
    )(act, w_bf, x1, g_post, gt)


SELECT_ROWS = 16


def _select_sample_kernel(pt_ref, q_ref, km_ref, idx_ref, *, n_q, n_pages):
    b = pl.program_id(0)
    nb = km_ref.shape[2]
    lane = lax.broadcasted_iota(jnp.int32, (nb, 128), 1)
    rowi = lax.broadcasted_iota(jnp.int32, (nb, 128), 0)
    s = jnp.full((nb, 128), NEG, F32)
    for hh in range(N_HEADS):
        km = km_ref[0, hh]
        for t in range(n_q):
            col = jnp.sum(km * q_ref[0, hh, t:t + 1, :], axis=1, keepdims=True)
            s = jnp.where(lane == hh * n_q + t, col, s)
    out_row = lax.broadcasted_iota(jnp.int32, (SELECT_ROWS, 128), 0)
    out = jnp.zeros((SELECT_ROWS, 128), jnp.int32)
    for r in range(MOBA_TOPK):
        top = jnp.max(s, axis=0, keepdims=True)
        arg = jnp.min(jnp.where(s == top, rowi, nb), axis=0, keepdims=True)
        out = jnp.where(out_row == r, arg, out)
        s = jnp.where(rowi == arg, -jnp.inf, s)
        for pp in range(PAGES_PER_BLOCK):
            page = jnp.zeros((1, 128), jnp.int32)
            for j in range(nb):
                page = jnp.where(arg == j, pt_ref[b * n_pages + j * PAGES_PER_BLOCK + pp], page)
            out = jnp.where(out_row == 8 + r * PAGES_PER_BLOCK + pp, page, out)
    idx_ref[0] = out


def _select_sample(page_table_flat, q, kmean, n_pages):
    batch, _, n_q, _ = q.shape
    nb = kmean.shape[2]
    grid_spec = pltpu.PrefetchScalarGridSpec(
        num_scalar_prefetch=1,
        grid=(batch,),
        in_specs=[pl.BlockSpec((1, N_HEADS, n_q, HEAD_DIM), lambda b, pt: (b, 0, 0, 0)),
                  pl.BlockSpec((1, N_HEADS, nb, HEAD_DIM), lambda b, pt: (b, 0, 0, 0))],
        out_specs=pl.BlockSpec((1, SELECT_ROWS, 128), lambda b, pt: (b, 0, 0)),
    )
    return pl.pallas_call(
        functools.partial(_select_sample_kernel, n_q=n_q, n_pages=n_pages),
        out_shape=jax.ShapeDtypeStruct((batch, SELECT_ROWS, 128), jnp.int32),
        grid_spec=grid_spec,
        compiler_params=_params(("arbitrary",), 32),
        name="select_sample",
    )(page_table_flat, q, kmean)


def _moba_sample_kernel(page_ref, idx_ref, slopes_ref, q_ref, kn_ref, vn_ref, pool_k, pool_v, o_ref,
                        kbuf, vbuf, sem, *, n_q, past):
    n_sel = n_q * MOBA_TOPK * PAGES_PER_BLOCK
    slots = kbuf.shape[0]
    ahead = slots - 1
    step = pl.program_id(0)
    n_steps = pl.num_programs(0)
    slot = step % slots

    def page_copies(st, sl):
        hh = st % N_HEADS
        copies = []
        for n in range(n_sel):
            page = page_ref[st * n_sel + n]
            copies.append(pltpu.make_async_copy(pool_k.at[page, hh], kbuf.at[sl, n], sem.at[sl, 0]))
            copies.append(pltpu.make_async_copy(pool_v.at[page, hh], vbuf.at[sl, n], sem.at[sl, 1]))
        return copies

    @pl.when(step == 0)
    def _():
        for first in range(ahead):
            for cp in page_copies(first, first):
                cp.start()

    @pl.when(step + ahead < n_steps)
    def _():
        for cp in page_copies(step + ahead, (step + ahead) % slots):
            cp.start()

    for cp in page_copies(step, slot):
        cp.wait()

    b = step // N_HEADS
    h = step % N_HEADS
    slope = slopes_ref[h]
    key = lax.broadcasted_iota(jnp.int32, (PAGE, 1), 0)
    k_refs = [kbuf.at[slot, n] for n in range(n_sel)]
    v_refs = [vbuf.at[slot, n] for n in range(n_sel)]
    for t in range(n_q):
        q = q_ref[0, 0, t:t + 1, :] * (HEAD_DIM ** -0.5)
        scores = []
        for s in range(MOBA_TOPK):
            blk = idx_ref[((b * N_HEADS + h) * n_q + t) * MOBA_TOPK + s]
            for pp in range(PAGES_PER_BLOCK):
                kp = k_refs[(t * MOBA_TOPK + s) * PAGES_PER_BLOCK + pp][...]
                dist = (past + t - blk * MOBA_BLOCK - pp * PAGE - key).astype(F32)
                scores.append(jnp.sum(kp * q, axis=1, keepdims=True) - slope * dist)
        own = []
        for t2 in range(t + 1):
            own.append(jnp.sum(kn_ref[0, 0, t2:t2 + 1, :] * q, axis=1, keepdims=True) - slope * float(t - t2))
        m = own[0]
        for sc in scores:
            m = jnp.maximum(m, jnp.max(sc, axis=0, keepdims=True))
        for sc in own[1:]:
            m = jnp.maximum(m, sc)
        l = jnp.zeros((1, 1), F32)
        acc = jnp.zeros((1, HEAD_DIM), F32)
        for n, sc in enumerate(scores):
            p = jnp.exp(sc - m)
            l = l + jnp.sum(p, axis=0, keepdims=True)
            acc = acc + jnp.sum(p * v_refs[n + t * MOBA_TOPK * PAGES_PER_BLOCK][...], axis=0, keepdims=True)
        for t2, sc in enumerate(own):
            p = jnp.exp(sc - m)
            l = l + p
            acc = acc + p * vn_ref[0, 0, t2:t2 + 1, :]
        o_ref[0, 0, t:t + 1, :] = acc / l


def _moba_sample(pages_flat, idx_flat, slopes, q, k_new, v_new, pool_k, pool_v, n_pages):
    batch, _, n_q, _ = q.shape
    past = n_pages * PAGE
    n_sel = n_q * MOBA_TOPK * PAGES_PER_BLOCK
    slots = 3
    assert batch * N_HEADS >= slots
    tok = pl.BlockSpec((1, 1, n_q, HEAD_DIM), lambda i, pg, idx, sl: (i // N_HEADS, i % N_HEADS, 0, 0))
    hbm = pl.BlockSpec(memory_space=pl.ANY)
    grid_spec = pltpu.PrefetchScalarGridSpec(
        num_scalar_prefetch=3,
        grid=(batch * N_HEADS,),
        in_specs=[tok, tok, tok, hbm, hbm],
        out_specs=tok,
        scratch_shapes=[pltpu.VMEM((slots, n_sel, PAGE, HEAD_DIM), F32),
                        pltpu.VMEM((slots, n_sel, PAGE, HEAD_DIM), F32),
                        pltpu.SemaphoreType.DMA((slots, 2))],
    )
    return pl.pallas_call(
        functools.partial(_moba_sample_kernel, n_q=n_q, past=past),
        out_shape=jax.ShapeDtypeStruct((batch, N_HEADS, n_q, HEAD_DIM), F32),
        grid_spec=grid_spec,
        compiler_params=_params(("arbitrary",), 32),
        name="moba_sample",
    )(pages_flat, idx_flat, slopes, q, k_new, v_new, pool_k, pool_v)


def kernel(x_prompt, x_sample, cache_k, cache_v, state_conv, page_table, c_prompt, c_sample, w_ada, b_ada, g_pre_mix, w_in, w_dw, b_dw, g_conv_ln, b_conv_ln, g_attn_out, g_conv_out, w_out, g_post_mix, g_pre_ffn, w_gate_up, w_down, g_post_ffn):
    depth = w_ada.shape[0]
    assert depth == 1, "single layer: the prompt and sample residual streams are not chained across layers here"
    batch, seq, d = x_prompt.shape
    dec_batch, dec_seq, _ = x_sample.shape
    n_pages = page_table.shape[1]
    past = n_pages * PAGE
    conv_w = w_dw.shape[-1]
    assert past % MOBA_BLOCK == 0 and dec_seq <= MOBA_BLOCK and past // MOBA_BLOCK >= MOBA_TOPK
    assert seq % ATTN_TILE == 0 and d == ATTN_WIDTH + conv_w

    slopes = 2.0 ** (-8.0 * (jnp.arange(N_HEADS, dtype=F32) + 1.0) / N_HEADS)
    l = 0
    vec = lambda a: a[l].reshape(1, -1)
    g_pm, g_ao, g_co = vec(g_pre_mix), vec(g_attn_out), vec(g_conv_out)
    g_post, g_pf, g_pffn = vec(g_post_mix), vec(g_pre_ffn), vec(g_post_ffn)
    bdw, gln, bln = vec(b_dw), vec(g_conv_ln), vec(b_conv_ln)
    wdw = w_dw[l].reshape(CONV_K, conv_w)

    n_c = batch + dec_batch
    c_rows = -(-n_c // 8) * 8
    c_all = jnp.concatenate([c_prompt, c_sample, jnp.zeros((c_rows - n_c, d), F32)], axis=0)
    mod = _ada(c_all, w_ada[l], vec(b_ada))
    mod_p = mod.reshape(c_rows * 6, 1, d)
    mod_rows = jnp.repeat(mod[batch:n_c].reshape(dec_batch, 6, d), dec_seq, axis=0)
    mod_s = [mod_rows[:, w] for w in range(6)]

    m_s = dec_batch * dec_seq
    xp = x_prompt.reshape(batch * seq, d)
    xs = x_sample.reshape(m_s, d)
    hs = _modnorm(xs, g_pm, mod_s, 1, 0, m_s, None)

    assert batch == 2, "the conv side jobs are mapped one prompt sequence per K/V projection call"
    h, u, u_s = _glu_proj(xp, g_pm, mod_p, seq, w_in[l], hs, 3 * ATTN_WIDTH, 3 * ATTN_WIDTH + conv_w, conv_w)
    conv_params = (wdw, bdw, gln, bln, g_co)
    k_pages, k_heads, k_rows_s, kmean, w_out_bf, conv_n0, conv_buf0 = _kv_proj(
        h, w_in[l], 1, hs, batch, seq, True, 0, u, conv_params, w_out[l])
    v_pages, v_heads, v_rows_s, conv_n1, conv_buf1 = _kv_proj(
        h, w_in[l], 2, hs, batch, seq, False, 1, u, conv_params)
    conv_buf_p = jnp.concatenate([conv_buf0, conv_buf1], axis=0)
    q_heads, pen, q_rows_s = _q_proj(h, w_in[l], hs, kmean.reshape(batch, seq // MOBA_BLOCK, ATTN_WIDTH),
                                     batch, seq)

    pt_flat = page_table.reshape(-1)
    o_attn, kmean_s = _moba_prompt(slopes, pt_flat, q_heads, pen, k_heads, v_heads, cache_k[l], dec_batch, n_pages)
    o_attn = o_attn.reshape(batch * seq, ATTN_WIDTH)
    x1, h2 = _out_proj(o_attn, [conv_n0, conv_n1], w_out_bf, xp, g_ao, g_post, g_pf, mod_p, 512, seq)

    to_heads = lambda t: t.reshape(dec_batch, dec_seq, N_HEADS, HEAD_DIM).transpose(0, 2, 1, 3)
    q_s, k_s, v_s = to_heads(q_rows_s), to_heads(k_rows_s), to_heads(v_rows_s)
    kmean_s = kmean_s.transpose(0, 2, 1, 3)
    sel = _select_sample(pt_flat, q_s, kmean_s, n_pages)[:, :, :N_HEADS * dec_seq]
    per_query = lambda rows: rows.reshape(dec_batch, -1, N_HEADS, dec_seq).transpose(0, 2, 3, 1).reshape(-1)
    idx_flat = per_query(sel[:, :MOBA_TOPK])
    pages_flat = per_query(sel[:, 8:8 + MOBA_TOPK * PAGES_PER_BLOCK])
    o_s = _moba_sample(pages_flat, idx_flat, slopes, q_s, k_s, v_s, cache_k[l], cache_v[l], n_pages)
    o_attn_s = o_s.transpose(0, 2, 1, 3).reshape(m_s, ATTN_WIDTH)
    conv_n_s, conv_buf_s = _conv_sample(state_conv[l], u_s.reshape(dec_batch, dec_seq, conv_w),
                                        wdw, bdw, gln, bln, g_co)
    x1_s, h2_s = _out_proj(o_attn_s, [conv_n_s.reshape(m_s, conv_w)], w_out_bf, xs, g_ao, g_post, g_pf,
                           mod_s, m_s, None)

    act, act_s, w_down_bf = _ffn_up(h2, w_gate_up[l], h2_s, w_down[l])
    y_p = _ffn_down(act, w_down_bf, x1, g_pffn, mod_p, 256, seq).reshape(batch, seq, d)
    y_s = _ffn_down(act_s, w_down_bf, x1_s, g_pffn, mod_s, m_s, None).reshape(dec_batch, dec_seq, d)

    return (y_p, y_s, k_pages[None], v_pages[None], conv_buf_p[None],
            k_s[None], v_s[None], conv_buf_s[None])
```

```python
import functools

import jax
import jax.numpy as jnp
from jax import lax
from jax.experimental import pallas as pl
from jax.experimental.pallas import tpu as pltpu

F32 = jnp.float32
BF16 = jnp.bfloat16

N_HEADS = 8
HEAD_DIM = 128
ATTN_WIDTH = N_HEADS * HEAD_DIM
CONV_K = 31
CONV_BUF = CONV_K - 1
CONV_HALO = 32
CONV_CHUNK = 64
SUBLANES = 8
PAGE = 128
MOBA_BLOCK = 256
MOBA_TOPK = 3
PAGES_PER_BLOCK = MOBA_BLOCK // PAGE
ATTN_TILE = 4 * MOBA_BLOCK
PROJ_TILE = 512
ATTN_KEYS = 4 * MOBA_BLOCK
LOG2E = 1.4426950408889634
EPS = 1e-6
NEG = -1e30
MIB = 1024 * 1024

_NT = (((1,), (1,)), ((), ()))


def _params(sem, vmem_mib, flags=None):
    return pltpu.CompilerParams(dimension_semantics=sem, vmem_limit_bytes=vmem_mib * MIB, flags=flags)


def _resident(shape, index_map):
    return pl.BlockSpec(shape, index_map, pipeline_mode=pl.Buffered(1))


def _rms(x, g):
    return x * lax.rsqrt(jnp.mean(x * x, axis=-1, keepdims=True) + EPS) * g


def _silu(x):
    return x * jax.nn.sigmoid(x)


def _mod2d(ref):
    v = ref[...]
    return v.reshape(v.shape[-2], v.shape[-1])


def _mod_spec(mod, which, tm, rows_per_group):
    if rows_per_group is None:
        return mod[which], pl.BlockSpec((tm, mod[which].shape[1]), lambda i, *_: (i, 0))
    tiles = rows_per_group // tm
    return mod, pl.BlockSpec((1, 1, mod.shape[2]), lambda i, *_: ((i // tiles) * 6 + which, 0, 0))


def _ada_kernel(c_ref, w_ref, b_ref, o_ref):
    s = _silu(c_ref[...]).astype(BF16)
    o_ref[...] = jnp.dot(s, w_ref[...].astype(BF16), preferred_element_type=F32) + b_ref[...]


def _ada(c_all, w, b):
    rows, d = c_all.shape
    n = w.shape[1]
    tn = 1024
    return pl.pallas_call(
        _ada_kernel,
        out_shape=jax.ShapeDtypeStruct((rows, n), F32),
        grid=(n // tn,),
        in_specs=[pl.BlockSpec((rows, d), lambda j: (0, 0)),
                  pl.BlockSpec((d, tn), lambda j: (0, j)),
                  pl.BlockSpec((1, tn), lambda j: (0, j))],
        out_specs=pl.BlockSpec((rows, tn), lambda j: (0, j)),
        compiler_params=_params(("arbitrary",), 40),
        name="ada_mod",
    )(c_all, w, b)


def _modnorm_kernel(x_ref, g_ref, sc_ref, sh_ref, o_ref):
    h = _rms(x_ref[...], g_ref[...]) * (1.0 + _mod2d(sc_ref)) + _mod2d(sh_ref)
    o_ref[...] = h.astype(BF16)


def _modnorm(x, g, mod, which_sc, which_sh, tm, rows_per_group):
    m, d = x.shape
    sc, sc_spec = _mod_spec(mod, which_sc, tm, rows_per_group)
    sh, sh_spec = _mod_spec(mod, which_sh, tm, rows_per_group)
    return pl.pallas_call(
        _modnorm_kernel,
        out_shape=jax.ShapeDtypeStruct((m, d), BF16),
        grid=(m // tm,),
        in_specs=[pl.BlockSpec((tm, d), lambda i: (i, 0)),
                  pl.BlockSpec((1, d), lambda i: (0, 0)), sc_spec, sh_spec],
        out_specs=pl.BlockSpec((tm, d), lambda i: (i, 0)),
        compiler_params=_params(("parallel",), 32),
        name="modnorm",
    )(x, g, sc, sh)


def _kv_proj_kernel(*refs, tm, with_mean, with_cast):
    it = iter(refs)
    h_ref, w_ref, hs_ref = next(it), next(it), next(it)
    cast_ref = next(it) if with_cast else None
    conv_in = [next(it) for _ in range(7)]
    pages_ref, heads_ref, s_ref = next(it), next(it), next(it)
    mean_ref = next(it) if with_mean else None
    cast_out_ref = next(it) if with_cast else None
    conv_out = [next(it), next(it)]
    wbf_ref = next(it)
    conv_scratch = [next(it), next(it), next(it)]
    i = pl.program_id(0)

    @pl.when(i == 0)
    def _():
        wbf_ref[...] = w_ref[...].astype(BF16)
        s_ref[...] = jnp.dot(hs_ref[...], wbf_ref[...], preferred_element_type=F32)

    for pair in range(N_HEADS // 2):
        ps = slice(pair * 2 * HEAD_DIM, (pair + 1) * 2 * HEAD_DIM)
        acc = jnp.dot(h_ref[...], wbf_ref[:, ps], preferred_element_type=F32)
        for sub in range(2):
            hh = pair * 2 + sub
            cs = slice(sub * HEAD_DIM, (sub + 1) * HEAD_DIM)
            heads_ref[0, hh] = acc[:, cs].astype(BF16)
            for p in range(tm // PAGE):
                pages_ref[0, p, hh] = acc[p * PAGE:(p + 1) * PAGE, cs]
        if with_mean:
            for mb in range(tm // MOBA_BLOCK):
                mean_ref[mb, :, ps] = jnp.mean(acc[mb * MOBA_BLOCK:(mb + 1) * MOBA_BLOCK, :], axis=0, keepdims=True)
    if with_cast:
        cast_out_ref[...] = cast_ref[...].astype(BF16)
    _conv_tile(i == 0, i == pl.num_programs(0) - 1, *conv_in, *conv_out, *conv_scratch)


def _kv_proj(h, w_in, col_block, hs, batch, seq, with_mean, conv_seq, u, conv_params, cast_src=None):
    tm = PROJ_TILE
    m, d = h.shape
    ms = hs.shape[0]
    tiles = seq // tm
    steps = m // tm
    width = u.shape[1]
    tt = seq // steps
    assert tt * steps == seq and tt % CONV_HALO == 0 and tt >= CONV_BUF
    halo_per_tile = tt // CONV_HALO
    in_specs = [pl.BlockSpec((tm, d), lambda i: (i, 0)),
                _resident((d, ATTN_WIDTH), lambda i: (0, col_block)),
                _resident((ms, d), lambda i: (0, 0))]
    args = [h, w_in, hs]
    out_shape = [jax.ShapeDtypeStruct((batch, seq // PAGE, N_HEADS, PAGE, HEAD_DIM), F32),
                 jax.ShapeDtypeStruct((batch, N_HEADS, seq, HEAD_DIM), BF16),
                 jax.ShapeDtypeStruct((ms, ATTN_WIDTH), F32)]
    out_specs = [pl.BlockSpec((1, tm // PAGE, N_HEADS, PAGE, HEAD_DIM), lambda i: (i // tiles, i % tiles, 0, 0, 0)),
                 pl.BlockSpec((1, N_HEADS, tm, HEAD_DIM), lambda i: (i // tiles, 0, i % tiles, 0)),
                 pl.BlockSpec((ms, ATTN_WIDTH), lambda i: (0, 0))]
    if with_mean:
        out_shape.append(jax.ShapeDtypeStruct((m // MOBA_BLOCK, 1, ATTN_WIDTH), F32))
        out_specs.append(pl.BlockSpec((tm // MOBA_BLOCK, 1, ATTN_WIDTH), lambda i: (i, 0, 0)))
    if cast_src is not None:
        rows, cols = cast_src.shape
        slab = rows // steps
        in_specs.append(pl.BlockSpec((slab, cols), lambda i: (i, 0)))
        args.append(cast_src)
        out_shape.append(jax.ShapeDtypeStruct((rows, cols), BF16))
        out_specs.append(pl.BlockSpec((slab, cols), lambda i: (i, 0)))
    vec = pl.BlockSpec((1, width), lambda i: (0, 0))
    in_specs += [pl.BlockSpec((tt, width), lambda i: (conv_seq * steps + i, 0)),
                 pl.BlockSpec((CONV_HALO, width),
                              lambda i: (jnp.maximum((conv_seq * steps + i) * halo_per_tile - 1, 0), 0)),
                 pl.BlockSpec((CONV_K, width), lambda i: (0, 0)), vec, vec, vec, vec]
    args += [u, u, *conv_params]
    out_shape += [jax.ShapeDtypeStruct((seq, width), BF16), jax.ShapeDtypeStruct((1, CONV_BUF, width), F32)]
    out_specs += [pl.BlockSpec((tt, width), lambda i: (i, 0)),
                  pl.BlockSpec((1, CONV_BUF, width), lambda i: (0, 0, 0))]
    return pl.pallas_call(
        functools.partial(_kv_proj_kernel, tm=tm, with_mean=with_mean, with_cast=cast_src is not None),
        out_shape=out_shape,
        grid=(steps,),
        in_specs=in_specs,
        out_specs=out_specs,
        scratch_shapes=[pltpu.VMEM((d, ATTN_WIDTH), BF16),
                        pltpu.VMEM((CONV_HALO + tt + SUBLANES, width), F32),
                        pltpu.VMEM((CONV_CHUNK + SUBLANES, 128), F32),
                        pltpu.VMEM((tt, width), F32)],
        compiler_params=_params(("arbitrary",), 56),
        name="kv_proj",
    )(*args)


def _q_proj_kernel(h_ref, w_ref, hs_ref, km_ref, q_ref, pen_ref, qs_ref, wbf_ref, sm_ref, *, tm, tiles):
    i = pl.program_id(0)

    @pl.when(i == 0)
    def _():
        wbf_ref[...] = w_ref[...].astype(BF16)
        qs_ref[...] = jnp.dot(hs_ref[...], wbf_ref[...], preferred_element_type=F32)

    acc = jnp.dot(h_ref[...], wbf_ref[...], preferred_element_type=F32)
    km = km_ref[0]
    nb = km.shape[0]
    jrow = lax.broadcasted_iota(jnp.int32, (nb, tm), 0)
    tcol = lax.broadcasted_iota(jnp.int32, (nb, tm), 1)
    n_full = (i % tiles) * (tm // MOBA_BLOCK) + lax.shift_right_logical(tcol, MOBA_BLOCK.bit_length() - 1)
    valid = jrow < n_full
    own = jrow == n_full
    pens = []
    for hh in range(N_HEADS):
        cs = slice(hh * HEAD_DIM, (hh + 1) * HEAD_DIM)
        qh = acc[:, cs]
        q_ref[0, hh] = (qh * (HEAD_DIM ** -0.5 * LOG2E)).astype(BF16)
        s = lax.dot_general(km[:, cs], qh, _NT, precision=lax.Precision.HIGHEST, preferred_element_type=F32)
        sm = jnp.where(valid, s, NEG)
        sm_ref[...] = sm
        rank = jnp.zeros((nb, tm), jnp.int32)
        for jp in range(nb):
            row = sm_ref[jp:jp + 1, :]
            gt = jnp.where(row > sm, 1, 0)
            ge = jnp.where(row >= sm, 1, 0)
            rank = rank + jnp.where(jrow > jp, ge, gt)
        keep = (valid & (rank < MOBA_TOPK)) | own
        pens.append(jnp.where(keep, 0.0, NEG))
    pen = jnp.concatenate(pens, axis=0)
    pen_ref[0] = pen.T.astype(BF16)


def _q_proj(h, w_in, hs, kmean, batch, seq):
    tm = PROJ_TILE
    m, d = h.shape
    ms = hs.shape[0]
    tiles = seq // tm
    nb = kmean.shape[1]
    return pl.pallas_call(
        functools.partial(_q_proj_kernel, tm=tm, tiles=tiles),
        out_shape=[jax.ShapeDtypeStruct((batch, N_HEADS, seq, HEAD_DIM), BF16),
                   jax.ShapeDtypeStruct((batch, seq, N_HEADS * nb), BF16),
                   jax.ShapeDtypeStruct((ms, ATTN_WIDTH), F32)],
        grid=(m // tm,),
        in_specs=[pl.BlockSpec((tm, d), lambda i: (i, 0)),
                  _resident((d, ATTN_WIDTH), lambda i: (0, 0)),
                  _resident((ms, d), lambda i: (0, 0)),
                  pl.BlockSpec((1, nb, ATTN_WIDTH), lambda i: (i // tiles, 0, 0))],
        out_specs=[pl.BlockSpec((1, N_HEADS, tm, HEAD_DIM), lambda i: (i // tiles, 0, i % tiles, 0)),
                   pl.BlockSpec((1, tm, N_HEADS * nb), lambda i: (i // tiles, i % tiles, 0)),
                   pl.BlockSpec((ms, ATTN_WIDTH), lambda i: (0, 0))],
        scratch_shapes=[pltpu.VMEM((d, ATTN_WIDTH), BF16), pltpu.VMEM((nb, tm), F32)],
        compiler_params=_params(("arbitrary",), 48),
        name="q_proj",
    )(h, w_in, hs, kmean)


def _fold_lanes(x, op):
    out = x[:, 0:128]
    for c in range(1, x.shape[1] // 128):
        out = op(out, x[:, c * 128:(c + 1) * 128])
    return out


def _block_key_means(page_refs, o_ref):
    for blk in range(len(page_refs) // PAGES_PER_BLOCK):
        tot = jnp.zeros((N_HEADS, HEAD_DIM), F32)
        for pp in range(PAGES_PER_BLOCK):
            tot = tot + jnp.sum(page_refs[blk * PAGES_PER_BLOCK + pp][...], axis=1)
        o_ref[0, blk] = tot * (1.0 / MOBA_BLOCK)


def _moba_prompt_kernel(slopes_ref, pt_ref, q_ref, pen_ref, k_ref, v_ref, pool_ref, o_ref, km_ref,
                        ke_ref, ve_ref, s_ref, pbuf, sem, *, nb, n_side):
    h = pl.program_id(1)
    g = pl.program_id(2)
    step = (pl.program_id(0) * pl.num_programs(1) + h) * pl.num_programs(2) + g
    n_steps = pl.num_programs(0) * pl.num_programs(1) * pl.num_programs(2)
    slot = step % 2

    def page_copies(st, sl):
        return [pltpu.make_async_copy(pool_ref.at[pt_ref[st * n_side + n]], pbuf.at[sl, n], sem.at[sl])
                for n in range(n_side)]

    @pl.when(step == 0)
    def _():
        for cp in page_copies(step, slot):
            cp.start()

    @pl.when(step + 1 < n_steps)
    def _():
        for cp in page_copies(step + 1, 1 - slot):
            cp.start()

    for cp in page_copies(step, slot):
        cp.wait()
    _block_key_means([pbuf.at[slot, n] for n in range(n_side)], km_ref)

    tq, tk = ATTN_TILE, ATTN_KEYS
    slope = slopes_ref[h] * LOG2E
    seq = k_ref.shape[2]

    @pl.when(g == 0)
    def _():
        blk = lax.shift_right_logical(lax.broadcasted_iota(jnp.int32, (seq, HEAD_DIM), 0),
                                      MOBA_BLOCK.bit_length() - 1)
        lane = lax.broadcasted_iota(jnp.int32, (seq, HEAD_DIM), 1)
        ke_ref[:, 0:HEAD_DIM] = k_ref[0, 0]
        ke_ref[:, HEAD_DIM:2 * HEAD_DIM] = jnp.where(lane == h * nb + blk, 1.0, 0.0).astype(BF16)
        ve_ref[:, 0:HEAD_DIM] = v_ref[0, 0]
        ve_ref[:, HEAD_DIM:2 * HEAD_DIM] = jnp.where(lane == 0, 1.0, 0.0).astype(BF16)

    qe = jnp.concatenate([q_ref[0, 0], pen_ref[0]], axis=1)
    key_lane = lax.broadcasted_iota(jnp.int32, (1, tk), 1)

    def scores(c):
        ke = ke_ref[pl.ds(pl.multiple_of(c * tk, tk), tk), :]
        s = lax.dot_general(qe, ke, _NT, preferred_element_type=F32)
        return s + slope * (c * tk + key_lane).astype(F32)

    last = lax.shift_right_logical(g * tq, tk.bit_length() - 1)
    qpos = g * tq + lax.broadcasted_iota(jnp.int32, (tq, tk), 0)
    kpos = last * tk + lax.broadcasted_iota(jnp.int32, (tq, tk), 1)
    s = jnp.where(kpos <= qpos, scores(last), NEG)
    s_ref[last] = s
    mx = _fold_lanes(s, jnp.maximum)

    def pass1(c, mx):
        s = scores(c)
        s_ref[c] = s
        return jnp.maximum(mx, _fold_lanes(s, jnp.maximum))

    mx = lax.fori_loop(0, last, pass1, mx)
    m = jnp.max(mx, axis=1, keepdims=True)

    def pass2(c, acc):
        p = jnp.exp2(s_ref[c] - m)
        ve = ve_ref[pl.ds(pl.multiple_of(c * tk, tk), tk), :]
        return acc + jnp.dot(p.astype(BF16), ve, preferred_element_type=F32)

    acc = lax.fori_loop(0, last + 1, pass2, jnp.zeros((tq, 2 * HEAD_DIM), F32))
    o_ref[0] = acc[:, 0:HEAD_DIM] / acc[:, HEAD_DIM:HEAD_DIM + 1]


def _moba_prompt(slopes, page_table_flat, q, pen, k, v, pool_k, side_batch, n_pages):
    batch, _, seq, _ = q.shape
    nb = seq // MOBA_BLOCK
    nt = seq // ATTN_TILE
    steps = batch * N_HEADS * nt
    n_side = side_batch * n_pages // steps
    assert n_side * steps == side_batch * n_pages and n_side % PAGES_PER_BLOCK == 0 and n_pages % n_side == 0
    steps_per_seq = n_pages // n_side

    def step(b, h, g):
        return (b * N_HEADS + h) * nt + g

    grid_spec = pltpu.PrefetchScalarGridSpec(
        num_scalar_prefetch=2,
        grid=(batch, N_HEADS, nt),
        in_specs=[pl.BlockSpec((1, 1, ATTN_TILE, HEAD_DIM), lambda b, h, g, sl, pt: (b, h, g, 0)),
                  pl.BlockSpec((1, ATTN_TILE, N_HEADS * nb), lambda b, h, g, sl, pt: (b, g, 0)),
                  pl.BlockSpec((1, 1, seq, HEAD_DIM), lambda b, h, g, sl, pt: (b, h, 0, 0)),
                  pl.BlockSpec((1, 1, seq, HEAD_DIM), lambda b, h, g, sl, pt: (b, h, 0, 0)),
                  pl.BlockSpec(memory_space=pl.ANY)],
        out_specs=[pl.BlockSpec((1, ATTN_TILE, HEAD_DIM), lambda b, h, g, sl, pt: (b, g, h)),
                   pl.BlockSpec((1, n_side // PAGES_PER_BLOCK, N_HEADS, HEAD_DIM),
                                lambda b, h, g, sl, pt: (step(b, h, g) // steps_per_seq,
                                                         step(b, h, g) % steps_per_seq, 0, 0))],
        scratch_shapes=[pltpu.VMEM((seq, 2 * HEAD_DIM), BF16),
                        pltpu.VMEM((seq, 2 * HEAD_DIM), BF16),
                        pltpu.VMEM((seq // ATTN_KEYS, ATTN_TILE, ATTN_KEYS), F32),
                        pltpu.VMEM((2, n_side, N_HEADS, PAGE, HEAD_DIM), F32),
                        pltpu.SemaphoreType.DMA((2,))],
    )
    return pl.pallas_call(
        functools.partial(_moba_prompt_kernel, nb=nb, n_side=n_side),
        out_shape=[jax.ShapeDtypeStruct((batch, seq, ATTN_WIDTH), F32),
                   jax.ShapeDtypeStruct((side_batch, n_pages // PAGES_PER_BLOCK, N_HEADS, HEAD_DIM), F32)],
        grid_spec=grid_spec,
        compiler_params=_params(("arbitrary", "arbitrary", "arbitrary"), 56),
        name="moba_prompt",
    )(slopes, page_table_flat, q, pen, k, v, pool_k)


def _glu_proj_kernel(x_ref, gpm_ref, sc_ref, sh_ref, wa_ref, wg_ref, hs_ref, h_ref, u_ref, us_ref, wbf_ref):
    width = wa_ref.shape[1]

    def glu(rows):
        ag = jnp.dot(rows, wbf_ref[...], preferred_element_type=F32)
        return ag[:, 0:width] * jax.nn.sigmoid(ag[:, width:2 * width])

    @pl.when(pl.program_id(0) == 0)
    def _():
        wbf_ref[:, 0:width] = wa_ref[...].astype(BF16)
        wbf_ref[:, width:2 * width] = wg_ref[...].astype(BF16)
        us_ref[...] = glu(hs_ref[...])

    sc, sh = _mod2d(sc_ref), _mod2d(sh_ref)
    tm = x_ref.shape[0]
    chunk = tm // 2
    for r0 in range(0, tm, chunk):
        rs = slice(r0, r0 + chunk)
        h = (_rms(x_ref[rs, :], gpm_ref[...]) * (1.0 + sc) + sh).astype(BF16)
        h_ref[rs, :] = h
        u_ref[rs, :] = glu(h)


def _glu_proj(x, g_pm, mod, seq, w_in, hs, a_col0, g_col0, width):
    tm = PROJ_TILE
    m, d = x.shape
    ms = hs.shape[0]
    sc, sc_spec = _mod_spec(mod, 1, tm, seq)
    sh, sh_spec = _mod_spec(mod, 0, tm, seq)
    return pl.pallas_call(
        _glu_proj_kernel,
        out_shape=[jax.ShapeDtypeStruct((m, d), BF16), jax.ShapeDtypeStruct((m, width), F32),
                   jax.ShapeDtypeStruct((ms, width), F32)],
        grid=(m // tm,),
        in_specs=[pl.BlockSpec((tm, d), lambda i: (i, 0)),
                  pl.BlockSpec((1, d), lambda i: (0, 0)), sc_spec, sh_spec,
                  _resident((d, width), lambda i: (0, a_col0 // width)),
                  _resident((d, width), lambda i: (0, g_col0 // width)),
                  _resident((ms, d), lambda i: (0, 0))],
        out_specs=[pl.BlockSpec((tm, d), lambda i: (i, 0)),
                   pl.BlockSpec((tm, width), lambda i: (i, 0)),
                   pl.BlockSpec((ms, width), lambda i: (0, 0))],
        scratch_shapes=[pltpu.VMEM((d, 2 * width), BF16)],
        compiler_params=_params(("arbitrary",), 56),
        name="glu_proj",
    )(x, g_pm, sc, sh, w_in, w_in, hs)


def _conv_tail(y, gln, bln, gco):
    mu = jnp.mean(y, axis=-1, keepdims=True)
    var = jnp.mean(jnp.square(y - mu), axis=-1, keepdims=True)
    z = _silu((y - mu) * lax.rsqrt(var + EPS) * gln + bln)
    return _rms(z, gco)


def _conv_tile(first, last, u_ref, halo_ref, w_ref, bdw_ref, gln_ref, bln_ref, gco_ref, cn_ref, buf_ref,
               ext_ref, z_ref, y_ref):
    tt = u_ref.shape[0]
    ext_ref[0:CONV_HALO, :] = jnp.where(first, 0.0, halo_ref[...])
    ext_ref[CONV_HALO:CONV_HALO + tt, :] = u_ref[...]
    width = u_ref.shape[1]
    ext_ref[CONV_HALO + tt:CONV_HALO + tt + SUBLANES, :] = jnp.zeros((SUBLANES, width), F32)
    lead = CONV_HALO - CONV_BUF
    rc = z_ref.shape[0] - SUBLANES
    zr = rc + SUBLANES
    for cb in range(width // 128):
        cs = slice(cb * 128, (cb + 1) * 128)
        for c0 in range(0, tt, rc):
            y = None
            for b in range(SUBLANES):
                taps = [k for k in range(CONV_K) if (lead + k) % SUBLANES == b]
                z = None
                for k in taps:
                    r0 = c0 + lead + k - b
                    term = w_ref[k:k + 1, cs] * ext_ref[r0:r0 + zr, cs]
                    z = term if z is None else z + term
                if b == 0:
                    y = z[0:rc, :]
                else:
                    z_ref[...] = z
                    y = y + z_ref[b:b + rc, :]
            y_ref[c0:c0 + rc, cs] = y + bdw_ref[:, cs]
    cn_ref[...] = _conv_tail(y_ref[...], gln_ref[...], bln_ref[...], gco_ref[...]).astype(BF16)

    @pl.when(last)
    def _():
        buf_ref[0] = ext_ref[CONV_HALO + tt - CONV_BUF:CONV_HALO + tt, :]


def _conv_sample_kernel(st_ref, u_ref, w_ref, bdw_ref, gln_ref, bln_ref, gco_ref, cn_ref, buf_ref,
                        ext_ref, *, rows):
    ext_ref[0:CONV_BUF, :] = st_ref[0]
    ext_ref[CONV_BUF:CONV_BUF + rows, :] = u_ref[0]
    acc = jnp.zeros((rows, u_ref.shape[2]), F32)
    for k in range(CONV_K):
        acc = acc + w_ref[k:k + 1, :] * ext_ref[k:k + rows, :]
    y = acc + bdw_ref[...]
    cn_ref[0] = _conv_tail(y, gln_ref[...], bln_ref[...], gco_ref[...]).astype(BF16)
    buf_ref[0] = ext_ref[rows:rows + CONV_BUF, :]


def _conv_sample(state, u, w_dw, b_dw, g_ln, b_ln, g_co):
    batch, rows, width = u.shape
    vec = pl.BlockSpec((1, width), lambda b: (0, 0))
    tok = pl.BlockSpec((1, rows, width), lambda b: (b, 0, 0))
    buf = pl.BlockSpec((1, CONV_BUF, width), lambda b: (b, 0, 0))
    return pl.pallas_call(
        functools.partial(_conv_sample_kernel, rows=rows),
        out_shape=[jax.ShapeDtypeStruct((batch, rows, width), BF16),
                   jax.ShapeDtypeStruct((batch, CONV_BUF, width), F32)],
        grid=(batch,),
        in_specs=[buf, tok, pl.BlockSpec((CONV_K, width), lambda b: (0, 0)), vec, vec, vec, vec],
        out_specs=[tok, buf],
        scratch_shapes=[pltpu.VMEM((CONV_BUF + rows + 6, width), F32)],
        compiler_params=_params(("parallel",), 32),
        name="conv_sample",
    )(state, u, w_dw, b_dw, g_ln, b_ln, g_co)


def _out_proj_kernel(o_ref, *refs, tiles_per_seq):
    n_conv = len(refs) - 10
    cn_refs = refs[:n_conv]
    w_ref, x_ref, gao_ref, gpm_ref, gpf_ref, gt_ref, sc_ref, sh_ref, x1_ref, h2_ref = refs[n_conv:]
    tm = o_ref.shape[0]
    gt, sc, sh = _mod2d(gt_ref), _mod2d(sc_ref), _mod2d(sh_ref)
    chunk = tm // 2 if tm % 512 == 0 else tm
    for r0 in range(0, tm, chunk):
        rs = slice(r0, r0 + chunk)
        rows = lambda v: v if v.shape[0] == 1 else v[rs]
        an = _rms(o_ref[rs, :], gao_ref[...]).astype(BF16)
        cn = cn_refs[0][rs, :]
        for s in range(1, n_conv):
            cn = jnp.where(pl.program_id(0) >= s * tiles_per_seq, cn_refs[s][rs, :], cn)
        mix = jnp.concatenate([an, cn], axis=1)
        merged = jnp.dot(mix, w_ref[...], preferred_element_type=F32)
        x1 = x_ref[rs, :] + rows(gt) * _rms(merged, gpm_ref[...])
        x1_ref[rs, :] = x1
        h2_ref[rs, :] = (_rms(x1, gpf_ref[...]) * (1.0 + rows(sc)) + rows(sh)).astype(BF16)


def _out_proj(o_attn, conv_n, w_bf, x, g_ao, g_pm, g_pf, mod, tm, rows_per_group):
    m, d = x.shape
    aw = o_attn.shape[1]
    cw = conv_n[0].shape[1]
    tiles_per_seq = conv_n[0].shape[0] // tm
    assert tiles_per_seq * tm == conv_n[0].shape[0] and tiles_per_seq * len(conv_n) == m // tm
    gt, gt_spec = _mod_spec(mod, 2, tm, rows_per_group)
    sc, sc_spec = _mod_spec(mod, 4, tm, rows_per_group)
    sh, sh_spec = _mod_spec(mod, 3, tm, rows_per_group)
    row = lambda w: pl.BlockSpec((tm, w), lambda i: (i, 0))
    vec = lambda w: pl.BlockSpec((1, w), lambda i: (0, 0))

    def conv_spec(s):
        return pl.BlockSpec((tm, cw), lambda i: (jnp.clip(i - s * tiles_per_seq, 0, tiles_per_seq - 1), 0))

    return pl.pallas_call(
        functools.partial(_out_proj_kernel, tiles_per_seq=tiles_per_seq),
        out_shape=[jax.ShapeDtypeStruct((m, d), F32), jax.ShapeDtypeStruct((m, d), BF16)],
        grid=(m // tm,),
        in_specs=[row(aw)] + [conv_spec(s) for s in range(len(conv_n))]
                 + [_resident((aw + cw, d), lambda i: (0, 0)), row(d),
                    vec(aw), vec(d), vec(d), gt_spec, sc_spec, sh_spec],
        out_specs=[row(d), row(d)],
        compiler_params=_params(("parallel",), 56),
        name="out_proj",
    )(o_attn, *conv_n, w_bf, x, g_ao, g_pm, g_pf, gt, sc, sh)


def _ffn_up_kernel(h_ref, wg_ref, wu_ref, hs_ref, wd_ref, a_ref, as_ref, wdbf_ref, wbf_ref):
    tn = wg_ref.shape[1]

    def swiglu(rows):
        gu = jnp.dot(rows, wbf_ref[...], preferred_element_type=F32)
        return (_silu(gu[:, 0:tn]) * gu[:, tn:2 * tn]).astype(BF16)

    @pl.when(pl.program_id(1) == 0)
    def _():
        wbf_ref[:, 0:tn] = wg_ref[...].astype(BF16)
        wbf_ref[:, tn:2 * tn] = wu_ref[...].astype(BF16)
        as_ref[...] = swiglu(hs_ref[...])

    a_ref[...] = swiglu(h_ref[...])
    wdbf_ref[...] = wd_ref[...].astype(BF16)


def _ffn_up(h, w_gate_up, hs, w_down):
    tm, tn = 1024, 512
    m, d = h.shape
    ms = hs.shape[0]
    d_ff = w_gate_up.shape[1] // 2
    n_blocks, m_tiles = d_ff // tn, m // tm
    slab = w_down.shape[0] // (n_blocks * m_tiles)
    return pl.pallas_call(
        _ffn_up_kernel,
        out_shape=[jax.ShapeDtypeStruct((m, d_ff), BF16), jax.ShapeDtypeStruct((ms, d_ff), BF16),
                   jax.ShapeDtypeStruct(w_down.shape, BF16)],
        grid=(n_blocks, m_tiles),
        in_specs=[pl.BlockSpec((tm, d), lambda n, i: (i, 0)),
                  pl.BlockSpec((d, tn), lambda n, i: (0, n)),
                  pl.BlockSpec((d, tn), lambda n, i: (0, n_blocks + n)),
                  _resident((ms, d), lambda n, i: (0, 0)),
                  pl.BlockSpec((slab, w_down.shape[1]), lambda n, i: (n * m_tiles + i, 0))],
        out_specs=[pl.BlockSpec((tm, tn), lambda n, i: (i, n)),
                   pl.BlockSpec((ms, tn), lambda n, i: (0, n)),
                   pl.BlockSpec((slab, w_down.shape[1]), lambda n, i: (n * m_tiles + i, 0))],
        scratch_shapes=[pltpu.VMEM((d, 2 * tn), BF16)],
        compiler_params=_params(("arbitrary", "arbitrary"), 56),
        name="ffn_up",
    )(h, w_gate_up, w_gate_up, hs, w_down)


def _ffn_down_kernel(a_ref, w_ref, x1_ref, g_ref, gt_ref, y_ref):
    z = jnp.dot(a_ref[...], w_ref[...], preferred_element_type=F32)
    y_ref[...] = x1_ref[...] + _mod2d(gt_ref) * _rms(z, g_ref[...])


def _ffn_down(act, w_bf, x1, g_post, mod, tm, rows_per_group):
    m, d_ff = act.shape
    d = w_bf.shape[1]
    gt, gt_spec = _mod_spec(mod, 5, tm, rows_per_group)
    return pl.pallas_call(
        _ffn_down_kernel,
        out_shape=jax.ShapeDtypeStruct((m, d), F32),
        grid=(m // tm,),
        in_specs=[pl.BlockSpec((tm, d_ff), lambda i: (i, 0)),
                  _resident((d_ff, d), lambda i: (0, 0)),
                  pl.BlockSpec((tm, d), lambda i: (i, 0)),
                  pl.BlockSpec((1, d), lambda i: (0, 0)), gt_spec],
        out_specs=pl.BlockSpec((tm, d), lambda i: (i, 0)),
        compiler_params=_params(("parallel",), 48),
        name="ffn_down",
    )(act, w_bf, x1, g_post, gt)


SELECT_ROWS = 16


def _select_sample_kernel(pt_ref, q_ref, km_ref, idx_ref, *, n_q, n_pages):
    b = pl.program_id(0)
    nb = km_ref.shape[2]
    lane = lax.broadcasted_iota(jnp.int32, (nb, 128), 1)
    rowi = lax.broadcasted_iota(jnp.int32, (nb, 128), 0)
    s = jnp.full((nb, 128), NEG, F32)
    for hh in range(N_HEADS):
        km = km_ref[0, hh]
        for t in range(n_q):
            col = jnp.sum(km * q_ref[0, hh, t:t + 1, :], axis=1, keepdims=True)
            s = jnp.where(lane == hh * n_q + t, col, s)
    out_row = lax.broadcasted_iota(jnp.int32, (SELECT_ROWS, 128), 0)
    out = jnp.zeros((SELECT_ROWS, 128), jnp.int32)
    for r in range(MOBA_TOPK):
        top = jnp.max(s, axis=0, keepdims=True)
        arg = jnp.min(jnp.where(s == top, rowi, nb), axis=0, keepdims=True)
        out = jnp.where(out_row == r, arg, out)
        s = jnp.where(rowi == arg, -jnp.inf, s)
        for pp in range(PAGES_PER_BLOCK):
            page = jnp.zeros((1, 128), jnp.int32)
            for j in range(nb):
                page = jnp.where(arg == j, pt_ref[b * n_pages + j * PAGES_PER_BLOCK + pp], page)
            out = jnp.where(out_row == 8 + r * PAGES_PER_BLOCK + pp, page, out)
    idx_ref[0] = out


def _select_sample(page_table_flat, q, kmean, n_pages):
    batch, _, n_q, _ = q.shape
    nb = kmean.shape[2]
    grid_spec = pltpu.PrefetchScalarGridSpec(
        num_scalar_prefetch=1,
        grid=(batch,),
        in_specs=[pl.BlockSpec((1, N_HEADS, n_q, HEAD_DIM), lambda b, pt: (b, 0, 0, 0)),
                  pl.BlockSpec((1, N_HEADS, nb, HEAD_DIM), lambda b, pt: (b, 0, 0, 0))],
        out_specs=pl.BlockSpec((1, SELECT_ROWS, 128), lambda b, pt: (b, 0, 0)),
    )
    return pl.pallas_call(
        functools.partial(_select_sample_kernel, n_q=n_q, n_pages=n_pages),
        out_shape=jax.ShapeDtypeStruct((batch, SELECT_ROWS, 128), jnp.int32),
        grid_spec=grid_spec,
        compiler_params=_params(("arbitrary",), 32),
        name="select_sample",
    )(page_table_flat, q, kmean)


def _moba_sample_kernel(page_ref, idx_ref, slopes_ref, q_ref, kn_ref, vn_ref, pool_k, pool_v, o_ref,
                        kbuf, vbuf, sem, *, n_q, past):
    n_sel = n_q * MOBA_TOPK * PAGES_PER_BLOCK
    slots = kbuf.shape[0]
    ahead = slots - 1
    step = pl.program_id(0)
    n_steps = pl.num_programs(0)
    slot = step % slots

    def page_copies(st, sl):
        hh = st % N_HEADS
        copies = []
        for n in range(n_sel):
            page = page_ref[st * n_sel + n]
            copies.append(pltpu.make_async_copy(pool_k.at[page, hh], kbuf.at[sl, n], sem.at[sl, 0]))
            copies.append(pltpu.make_async_copy(pool_v.at[page, hh], vbuf.at[sl, n], sem.at[sl, 1]))
        return copies

    @pl.when(step == 0)
    def _():
        for first in range(ahead):
            for cp in page_copies(first, first):
                cp.start()

    @pl.when(step + ahead < n_steps)
    def _():
        for cp in page_copies(step + ahead, (step + ahead) % slots):
            cp.start()

    for cp in page_copies(step, slot):
        cp.wait()

    b = step // N_HEADS
    h = step % N_HEADS
    slope = slopes_ref[h]
    key = lax.broadcasted_iota(jnp.int32, (PAGE, 1), 0)
    k_refs = [kbuf.at[slot, n] for n in range(n_sel)]
    v_refs = [vbuf.at[slot, n] for n in range(n_sel)]
    for t in range(n_q):
        q = q_ref[0, 0, t:t + 1, :] * (HEAD_DIM ** -0.5)
        scores = []
        for s in range(MOBA_TOPK):
            blk = idx_ref[((b * N_HEADS + h) * n_q + t) * MOBA_TOPK + s]
            for pp in range(PAGES_PER_BLOCK):
                kp = k_refs[(t * MOBA_TOPK + s) * PAGES_PER_BLOCK + pp][...]
                dist = (past + t - blk * MOBA_BLOCK - pp * PAGE - key).astype(F32)
                scores.append(jnp.sum(kp * q, axis=1, keepdims=True) - slope * dist)
        own = []
        for t2 in range(t + 1):
            own.append(jnp.sum(kn_ref[0, 0, t2:t2 + 1, :] * q, axis=1, keepdims=True) - slope * float(t - t2))
        m = own[0]
        for sc in scores:
            m = jnp.maximum(m, jnp.max(sc, axis=0, keepdims=True))
        for sc in own[1:]:
            m = jnp.maximum(m, sc)
        l = jnp.zeros((1, 1), F32)
        acc = jnp.zeros((1, HEAD_DIM), F32)
        for n, sc in enumerate(scores):
            p = jnp.exp(sc - m)
            l = l + jnp.sum(p, axis=0, keepdims=True)
            acc = acc + jnp.sum(p * v_refs[n + t * MOBA_TOPK * PAGES_PER_BLOCK][...], axis=0, keepdims=True)
        for t2, sc in enumerate(own):
            p = jnp.exp(sc - m)
            l = l + p
            acc = acc + p * vn_ref[0, 0, t2:t2 + 1, :]
        o_ref[0, 0, t:t + 1, :] = acc / l


def _moba_sample(pages_flat, idx_flat, slopes, q, k_new, v_new, pool_k, pool_v, n_pages):
    batch, _, n_q, _ = q.shape
    past = n_pages * PAGE
    n_sel = n_q * MOBA_TOPK * PAGES_PER_BLOCK
    slots = 4
    assert batch * N_HEADS >= slots
    tok = pl.BlockSpec((1, 1, n_q, HEAD_DIM), lambda i, pg, idx, sl: (i // N_HEADS, i % N_HEADS, 0, 0))
    hbm = pl.BlockSpec(memory_space=pl.ANY)
    grid_spec = pltpu.PrefetchScalarGridSpec(
        num_scalar_prefetch=3,
        grid=(batch * N_HEADS,),
        in_specs=[tok, tok, tok, hbm, hbm],
        out_specs=tok,
        scratch_shapes=[pltpu.VMEM((slots, n_sel, PAGE, HEAD_DIM), F32),
                        pltpu.VMEM((slots, n_sel, PAGE, HEAD_DIM), F32),
                        pltpu.SemaphoreType.DMA((slots, 2))],
    )
    return pl.pallas_call(
        functools.partial(_moba_sample_kernel, n_q=n_q, past=past),
        out_shape=jax.ShapeDtypeStruct((batch, N_HEADS, n_q, HEAD_DIM), F32),
        grid_spec=grid_spec,
        compiler_params=_params(("arbitrary",), 32),
        name="moba_sample",
    )(pages_flat, idx_flat, slopes, q, k_new, v_new, pool_k, pool_v)


def kernel(x_prompt, x_sample, cache_k, cache_v, state_conv, page_table, c_prompt, c_sample, w_ada, b_ada, g_pre_mix, w_in, w_dw, b_dw, g_conv_ln, b_conv_ln, g_attn_out, g_conv_out, w_out, g_post_mix, g_pre_ffn, w_gate_up, w_down, g_post_ffn):
    depth = w_ada.shape[0]
    assert depth == 1, "single layer: the prompt and sample residual streams are not chained across layers here"
    batch, seq, d = x_prompt.shape
    dec_batch, dec_seq, _ = x_sample.shape
    n_pages = page_table.shape[1]
    past = n_pages * PAGE
    conv_w = w_dw.shape[-1]
    assert past % MOBA_BLOCK == 0 and dec_seq <= MOBA_BLOCK and past // MOBA_BLOCK >= MOBA_TOPK
    assert seq % ATTN_TILE == 0 and d == ATTN_WIDTH + conv_w

    slopes = 2.0 ** (-8.0 * (jnp.arange(N_HEADS, dtype=F32) + 1.0) / N_HEADS)
    l = 0
    vec = lambda a: a[l].reshape(1, -1)
    g_pm, g_ao, g_co = vec(g_pre_mix), vec(g_attn_out), vec(g_conv_out)
    g_post, g_pf, g_pffn = vec(g_post_mix), vec(g_pre_ffn), vec(g_post_ffn)
    bdw, gln, bln = vec(b_dw), vec(g_conv_ln), vec(b_conv_ln)
    wdw = w_dw[l].reshape(CONV_K, conv_w)

    n_c = batch + dec_batch
    c_rows = -(-n_c // 8) * 8
    c_all = jnp.concatenate([c_prompt, c_sample, jnp.zeros((c_rows - n_c, d), F32)], axis=0)
    mod = _ada(c_all, w_ada[l], vec(b_ada))
    mod_p = mod.reshape(c_rows * 6, 1, d)
    mod_rows = jnp.repeat(mod[batch:n_c].reshape(dec_batch, 6, d), dec_seq, axis=0)
    mod_s = [mod_rows[:, w] for w in range(6)]

    m_s = dec_batch * dec_seq
    xp = x_prompt.reshape(batch * seq, d)
    xs = x_sample.reshape(m_s, d)
    hs = _modnorm(xs, g_pm, mod_s, 1, 0, m_s, None)

    assert batch == 2, "the conv side jobs are mapped one prompt sequence per K/V projection call"
    h, u, u_s = _glu_proj(xp, g_pm, mod_p, seq, w_in[l], hs, 3 * ATTN_WIDTH, 3 * ATTN_WIDTH + conv_w, conv_w)
    conv_params = (wdw, bdw, gln, bln, g_co)
    k_pages, k_heads, k_rows_s, kmean, w_out_bf, conv_n0, conv_buf0 = _kv_proj(
        h, w_in[l], 1, hs, batch, seq, True, 0, u, conv_params, w_out[l])
    v_pages, v_heads, v_rows_s, conv_n1, conv_buf1 = _kv_proj(
        h, w_in[l], 2, hs, batch, seq, False, 1, u, conv_params)
    conv_buf_p = jnp.concatenate([conv_buf0, conv_buf1], axis=0)
    q_heads, pen, q_rows_s = _q_proj(h, w_in[l], hs, kmean.reshape(batch, seq // MOBA_BLOCK, ATTN_WIDTH),
                                     batch, seq)

    pt_flat = page_table.reshape(-1)
    o_attn, kmean_s = _moba_prompt(slopes, pt_flat, q_heads, pen, k_heads, v_heads, cache_k[l], dec_batch, n_pages)
    o_attn = o_attn.reshape(batch * seq, ATTN_WIDTH)
    x1, h2 = _out_proj(o_attn, [conv_n0, conv_n1], w_out_bf, xp, g_ao, g_post, g_pf, mod_p, 512, seq)

    to_heads = lambda t: t.reshape(dec_batch, dec_seq, N_HEADS, HEAD_DIM).transpose(0, 2, 1, 3)
    q_s, k_s, v_s = to_heads(q_rows_s), to_heads(k_rows_s), to_heads(v_rows_s)
    kmean_s = kmean_s.transpose(0, 2, 1, 3)
    sel = _select_sample(pt_flat, q_s, kmean_s, n_pages)[:, :, :N_HEADS * dec_seq]
    per_query = lambda rows: rows.reshape(dec_batch, -1, N_HEADS, dec_seq).transpose(0, 2, 3, 1).reshape(-1)
    idx_flat = per_query(sel[:, :MOBA_TOPK])
    pages_flat = per_query(sel[:, 8:8 + MOBA_TOPK * PAGES_PER_BLOCK])
    o_s = _moba_sample(pages_flat, idx_flat, slopes, q_s, k_s, v_s, cache_k[l], cache_v[l], n_pages)
    o_attn_s = o_s.transpose(0, 2, 1, 3).reshape(m_s, ATTN_WIDTH)
    conv_n_s, conv_buf_s = _conv_sample(state_conv[l], u_s.reshape(dec_batch, dec_seq, conv_w),
                                        wdw, bdw, gln, bln, g_co)
    x1_s, h2_s = _out_proj(o_attn_s, [conv_n_s.reshape(m_s, conv_w)], w_out_bf, xs, g_ao, g_post, g_pf,
                           mod_s, m_s, None)

    act, act_s, w_down_bf = _ffn_up(h2, w_gate_up[l], h2_s, w_down[l])
    y_p = _ffn_down(act, w_down_bf, x1, g_pffn, mod_p, 256, seq).reshape(batch, seq, d)
    y_s = _ffn_down(act_s, w_down_bf, x1_s, g_pffn, mod_s, m_s, None).reshape(dec_batch, dec_seq, d)

    return (y_p, y_s, k_pages[None], v_pages[None], conv_buf_p[None],
            k_s[None], v_s[None], conv_buf_s[None])
```
